```python
import jax, jax.numpy as jnp
from jax import lax
import numpy as np

D_MODEL = 1024
BATCH = 1
SEQ = 16384
DEPTH = 4

GRID_W = 64
CTX_LEN = 256
EPS = 1e-6
MOD_INIT = 0.5

MLA_HEADS = 6
QK_NOPE = 64
QK_ROPE = 32
QK_DIM = QK_NOPE + QK_ROPE
V_HEAD = 64
Q_LORA = 256
KV_LORA = 128
MLA_WIDTH = MLA_HEADS * V_HEAD
ROPE_THETA = 10000.0
Q_BLOCK = 128

SSD_HEADS = 6
SSD_HEAD_DIM = 64
SSD_WIDTH = SSD_HEADS * SSD_HEAD_DIM
SSD_GROUPS = 2
SSD_STATE = 64
SSD_CONV = 4
SSD_CONV_DIM = SSD_WIDTH + 2 * SSD_GROUPS * SSD_STATE
SSD_CHUNK = 128

POOL_WINDOWS = (2, 4, 8, 16)
POOL_GROUP = 64
POOL_WIDTH = len(POOL_WINDOWS) * POOL_GROUP

MIX_WIDTH = MLA_WIDTH + SSD_WIDTH + POOL_WIDTH
IN_SPLITS = (Q_LORA, KV_LORA, QK_ROPE, SSD_WIDTH, SSD_CONV_DIM, 2 * SSD_HEADS, POOL_WIDTH)
IN_COLS = sum(IN_SPLITS)

N_EXPERTS = 32
TOP_K = 4
D_FF = 1024
SWIGLU_LIMIT = 7.0
SWIGLU_ALPHA = 1.702
MOE_BLOCK = 128

kernel_name = 'hybrid_mla_ssd_pool_moe_dit'


def rms_norm(x, g):
    xf = x.astype(jnp.float32)
    y = xf * lax.rsqrt(jnp.mean(xf * xf, axis=-1, keepdims=True) + EPS)
    return (y * g.astype(jnp.float32)).astype(x.dtype)


def modulate(x, g, shift, scale):
    return rms_norm(x, g) * (1 + scale) + shift


def axial_rope_angles(n):
    rows = n // GRID_W
    row = jnp.repeat(jnp.arange(rows), GRID_W).astype(jnp.float32)
    col = jnp.tile(jnp.arange(GRID_W), rows).astype(jnp.float32)
    n_freq = QK_ROPE // 4
    inv_freq = ROPE_THETA ** (-jnp.arange(n_freq, dtype=jnp.float32) / n_freq)
    ang = jnp.stack([row[:, None] * inv_freq, col[:, None] * inv_freq], axis=1)
    return jnp.cos(ang), jnp.sin(ang)


def apply_axial_rope(x, cos, sin):
    n_freq = QK_ROPE // 4
    xr = x.astype(jnp.float32).reshape(*x.shape[:-1], 2, 2, n_freq)
    x1, x2 = xr[..., 0, :], xr[..., 1, :]
    out = jnp.stack([x1 * cos - x2 * sin, x2 * cos + x1 * sin], axis=-2)
    return out.reshape(x.shape).astype(x.dtype)


def mla_queries(q_lat, lp, rope):
    B, n, _ = q_lat.shape
    q = (rms_norm(q_lat, lp['q_lat_g']) @ lp['w_uq']).reshape(B, n, MLA_HEADS, QK_DIM)
    q = rms_norm(q, lp['qn_g'])
    if rope is not None:
        q = jnp.concatenate([q[..., :QK_NOPE], apply_axial_rope(q[..., QK_NOPE:], *rope)], axis=-1)
    return q


def mla_keys_values(kv_lat, k_rope, lp, rope):
    B, n, _ = kv_lat.shape
    kv = (rms_norm(kv_lat, lp['kv_lat_g']) @ lp['w_ukv']).reshape(B, n, MLA_HEADS, QK_NOPE + V_HEAD)
    k_nope, v = kv[..., :QK_NOPE], kv[..., QK_NOPE:]
    k_r = jnp.broadcast_to(k_rope[:, :, None, :], (B, n, MLA_HEADS, QK_ROPE))
    k = rms_norm(jnp.concatenate([k_nope, k_r], axis=-1), lp['kn_g'])
    if rope is not None:
        k = jnp.concatenate([k[..., :QK_NOPE], apply_axial_rope(k[..., QK_NOPE:], *rope)], axis=-1)
    return k, v


def softmax_attend(q, k, v):
    s = jnp.einsum('bqhd,bkhd->bhqk', q, k).astype(jnp.float32) * (QK_DIM ** -0.5)
    p = jax.nn.softmax(s, axis=-1).astype(v.dtype)
    return jnp.einsum('bhqk,bkhd->bqhd', p, v)


def blocked_attend(q, k, v):
    B, n, H, dq = q.shape
    nb = n // Q_BLOCK
    qb = jnp.moveaxis(q.reshape(B, nb, Q_BLOCK, H, dq), 1, 0)
    o = lax.map(lambda qq: softmax_attend(qq, k, v), qb)
    return jnp.moveaxis(o, 0, 1).reshape(B, n, H * V_HEAD)


def centered_depthwise_conv(u, w, b):
    left = SSD_CONV // 2
    y = lax.conv_general_dilated(u, w[:, None, :].astype(u.dtype), window_strides=(1,),
                                 padding=[(left, SSD_CONV - 1 - left)],
                                 dimension_numbers=('NWC', 'WIO', 'NWC'),
                                 feature_group_count=u.shape[-1])
    return jax.nn.silu(y + b)


def ssd_chunk_scan(xs, dt, a, Bm, Cm, init):
    f32 = jnp.float32
    B, n, H, P = xs.shape
    L = SSD_CHUNK
    nc = n // L
    rep = H // SSD_GROUPS
    X = (xs.astype(f32) * dt[..., None]).reshape(B, nc, L, H, P)
    Bh = jnp.repeat(Bm.astype(f32), rep, axis=2).reshape(B, nc, L, H, SSD_STATE)
    Ch = jnp.repeat(Cm.astype(f32), rep, axis=2).reshape(B, nc, L, H, SSD_STATE)
    A = jnp.moveaxis((dt * a).reshape(B, nc, L, H), 3, 1)
    A_cs = jnp.cumsum(A, axis=-1)
    lower = jnp.arange(L)[:, None] >= jnp.arange(L)[None, :]
    decay = jnp.exp(jnp.where(lower, A_cs[..., :, None] - A_cs[..., None, :], -jnp.inf))
    scores = jnp.einsum('bclhn,bcshn->bhcls', Ch, Bh) * decay
    y_diag = jnp.einsum('bhcls,bcshp->bclhp', scores, X)
    states = jnp.einsum('bclhn,bhcl,bclhp->bchpn', Bh, jnp.exp(A_cs[..., -1:] - A_cs), X)
    chunk_decay = jnp.exp(A_cs[..., -1])

    def step(state, inp):
        st, dec = inp
        return state * dec[..., None, None] + st, state

    final, prev = lax.scan(step, init.astype(f32), (jnp.moveaxis(states, 1, 0), jnp.moveaxis(chunk_decay, 2, 0)))
    y_off = jnp.einsum('bclhn,cbhpn,bhcl->bclhp', Ch, prev, jnp.exp(A_cs))
    y = (y_diag + y_off).reshape(B, n, H, P)
    return y.astype(xs.dtype), final


def ssd_mix(xbc, dt_raw, z, init, lp):
    B, n, _ = xbc.shape
    gn = SSD_GROUPS * SSD_STATE
    xs = xbc[..., :SSD_WIDTH].reshape(B, n, SSD_HEADS, SSD_HEAD_DIM)
    Bm = xbc[..., SSD_WIDTH:SSD_WIDTH + gn].reshape(B, n, SSD_GROUPS, SSD_STATE)
    Cm = xbc[..., SSD_WIDTH + gn:].reshape(B, n, SSD_GROUPS, SSD_STATE)
    dt = jax.nn.softplus(dt_raw.astype(jnp.float32).reshape(B, n, 2, SSD_HEADS) + lp['dt_bias'].astype(jnp.float32))
    a = -jnp.exp(lp['a_log'].astype(jnp.float32))
    flip = lambda t: jnp.flip(t, axis=1)
    y_f, s_f = ssd_chunk_scan(xs, dt[:, :, 0], a[0], Bm, Cm, init[0])
    y_b, s_b = ssd_chunk_scan(flip(xs), flip(dt[:, :, 1]), a[1], flip(Bm), flip(Cm), init[1])
    y = y_f + flip(y_b) + lp['d_skip'][:, None] * xs
    y = rms_norm(y.reshape(B, n, SSD_WIDTH) * jax.nn.silu(z), lp['ssd_norm_g'])
    return y, (s_f, s_b)


def multiscale_pool(u, w, scale):
    B, n, _ = u.shape
    uf = u.astype(jnp.float32)
    cs = jnp.pad(jnp.cumsum(uf, axis=1), ((0, 0), (1, 0), (0, 0)))
    t = jnp.arange(n)
    outs = []
    for gi, win in enumerate(POOL_WINDOWS):
        lo = jnp.clip(t - win // 2, 0, n)
        hi = jnp.clip(t - win // 2 + win, 0, n)
        csg = cs[..., gi * POOL_GROUP:(gi + 1) * POOL_GROUP]
        mean = (csg[:, hi] - csg[:, lo]) / (hi - lo).astype(jnp.float32)[None, :, None]
        outs.append(mean - uf[..., gi * POOL_GROUP:(gi + 1) * POOL_GROUP])
    p = jnp.stack(outs, axis=2).astype(u.dtype)
    y = jnp.einsum('bngc,gcd->bngd', p, w).reshape(B, n, POOL_WIDTH)
    return y * scale


def token_mixers(h_lat, h_ctx, rope, lp, need_ctx):
    B, n, _ = h_lat.shape
    pts = np.cumsum(IN_SPLITS)[:-1].tolist()
    pl = jnp.split(h_lat @ lp['w_in'], pts, axis=-1)
    pc = jnp.split(h_ctx @ lp['w_in'], pts, axis=-1)
    k_c, v_c = mla_keys_values(pc[1], pc[2], lp, None)
    k_l, v_l = mla_keys_values(pl[1], pl[2], lp, rope)
    q_l = mla_queries(pl[0], lp, rope)
    att_l = blocked_attend(q_l, jnp.concatenate([k_c, k_l], axis=1), jnp.concatenate([v_c, v_l], axis=1))
    zero = jnp.zeros((B, SSD_HEADS, SSD_HEAD_DIM, SSD_STATE), jnp.float32)
    ssd_c, st_c = ssd_mix(centered_depthwise_conv(pc[4], lp['conv_w'], lp['conv_b']), pc[5], pc[3], (zero, zero), lp)
    ssd_l, _ = ssd_mix(centered_depthwise_conv(pl[4], lp['conv_w'], lp['conv_b']), pl[5], pl[3], st_c, lp)
    pool_l = multiscale_pool(pl[6], lp['pool_w'], lp['pool_scale'])
    out_l = jnp.concatenate([att_l, ssd_l, pool_l], axis=-1) @ lp['w_out']
    if not need_ctx:
        return out_l, None
    q_c = mla_queries(pc[0], lp, None)
    att_c = softmax_attend(q_c, k_c, v_c).reshape(B, h_ctx.shape[1], MLA_WIDTH)
    pool_c = multiscale_pool(pc[6], lp['pool_w'], lp['pool_scale'])
    out_c = jnp.concatenate([att_c, ssd_c, pool_c], axis=-1) @ lp['w_out']
    return out_l, out_c


def moe_ffn(h, router_w, router_b, w_gate, b_gate, w_up, b_up, w_down, b_down):
    tokens = h.reshape(-1, D_MODEL)
    n_assign = tokens.shape[0] * TOP_K
    logits = (tokens @ router_w).astype(jnp.float32) + router_b.astype(jnp.float32)
    top_logit, top_idx = lax.top_k(logits, TOP_K)
    top_w = jax.nn.softmax(top_logit, axis=-1)
    expert = top_idx.reshape(-1)
    token = jnp.arange(n_assign, dtype=jnp.int32) // TOP_K
    order = jnp.argsort(expert)
    e_sorted = expert[order]
    counts = jnp.bincount(expert, length=N_EXPERTS)
    padded = (counts + MOE_BLOCK - 1) // MOE_BLOCK * MOE_BLOCK
    start = jnp.cumsum(counts) - counts
    pad_end = jnp.cumsum(padded)
    pad_start = pad_end - padded
    dest = pad_start[e_sorted] + jnp.arange(n_assign) - start[e_sorted]
    n_blocks = -(-n_assign // MOE_BLOCK) + N_EXPERTS
    n_rows = n_blocks * MOE_BLOCK
    row_token = jnp.zeros((n_rows,), jnp.int32).at[dest].set(token[order])
    row_weight = jnp.zeros((n_rows,), jnp.float32).at[dest].set(top_w.reshape(-1)[order])
    block_expert = jnp.minimum(jnp.searchsorted(pad_end, jnp.arange(n_blocks) * MOE_BLOCK, side='right'), N_EXPERTS - 1)

    def expert_block(args):
        tok, e = args
        xb = tokens[tok]
        g = jnp.minimum(xb @ w_gate[e] + b_gate[e], SWIGLU_LIMIT)
        u = jnp.clip(xb @ w_up[e] + b_up[e], -SWIGLU_LIMIT, SWIGLU_LIMIT)
        return (g * jax.nn.sigmoid(SWIGLU_ALPHA * g) * (u + 1)) @ w_down[e] + b_down[e]

    out = lax.map(expert_block, (row_token.reshape(n_blocks, MOE_BLOCK), block_expert))
    out = out.reshape(n_rows, D_MODEL) * row_weight[:, None].astype(out.dtype)
    y = jnp.zeros_like(tokens).at[row_token].add(out)
    return y.reshape(h.shape)


def setup_inputs(seed: int = 0) -> dict:
    key = jax.random.key(seed)
    ks = iter(jax.random.split(key, 40))
    f32 = jnp.float32
    L = DEPTH

    def normal(shape, scale):
        return jax.random.normal(next(ks), shape, f32) * scale

    def gain(shape):
        return 1.0 + normal(shape, 0.1)

    x = normal((BATCH, SEQ, D_MODEL), 1.0)
    c = normal((BATCH, D_MODEL), 1.0)
    ctx = normal((BATCH, CTX_LEN, D_MODEL), 1.0)
    c_ctx = normal((D_MODEL,), 1.0)
    norm1_g = gain((L, D_MODEL))
    norm2_g = gain((L, D_MODEL))
    mod_w = normal((L, D_MODEL, 6 * D_MODEL), MOD_INIT * D_MODEL ** -0.5)
    mod_b = normal((L, 6 * D_MODEL), 0.02)
    w_in = normal((L, D_MODEL, IN_COLS), D_MODEL ** -0.5)
    dt_lo = sum(IN_SPLITS[:5])
    w_in = w_in.at[:, :, dt_lo:dt_lo + 2 * SSD_HEADS].multiply(0.1)
    mla_q_norm_g = gain((L, Q_LORA))
    mla_kv_norm_g = gain((L, KV_LORA))
    mla_w_uq = normal((L, Q_LORA, MLA_HEADS * QK_DIM), Q_LORA ** -0.5)
    mla_w_ukv = normal((L, KV_LORA, MLA_HEADS * (QK_NOPE + V_HEAD)), KV_LORA ** -0.5)
    mla_qn_g = gain((L, QK_DIM))
    mla_kn_g = gain((L, QK_DIM))
    ssd_conv_w = normal((L, SSD_CONV, SSD_CONV_DIM), SSD_CONV ** -0.5)
    ssd_conv_b = normal((L, SSD_CONV_DIM), 0.01)
    ssd_a_log = jnp.log(jax.random.uniform(next(ks), (L, 2, SSD_HEADS), f32, 1.0, 16.0))
    dt0 = jnp.exp(jax.random.uniform(next(ks), (L, 2, SSD_HEADS), f32, float(np.log(1e-3)), float(np.log(1e-1))))
    ssd_dt_bias = dt0 + jnp.log(-jnp.expm1(-dt0))
    ssd_d = gain((L, SSD_HEADS))
    ssd_norm_g = gain((L, SSD_WIDTH))
    pool_w = normal((L, len(POOL_WINDOWS), POOL_GROUP, POOL_GROUP), POOL_GROUP ** -0.5)
    pool_scale = gain((L, POOL_WIDTH))
    w_out = normal((L, MIX_WIDTH, D_MODEL), MIX_WIDTH ** -0.5)
    router_w = normal((L, D_MODEL, N_EXPERTS), D_MODEL ** -0.5)
    router_b = normal((L, N_EXPERTS), 0.01)
    moe_w_gate = normal((L, N_EXPERTS, D_MODEL, D_FF), D_MODEL ** -0.5)
    moe_b_gate = normal((L, N_EXPERTS, D_FF), 0.01)
    moe_w_up = normal((L, N_EXPERTS, D_MODEL, D_FF), D_MODEL ** -0.5)
    moe_b_up = normal((L, N_EXPERTS, D_FF), 0.01)
    moe_w_down = normal((L, N_EXPERTS, D_FF, D_MODEL), D_FF ** -0.5)
    moe_b_down = normal((L, N_EXPERTS, D_MODEL), 0.01)
    return {'x': x, 'c': c, 'ctx': ctx, 'c_ctx': c_ctx, 'norm1_g': norm1_g, 'norm2_g': norm2_g,
            'mod_w': mod_w, 'mod_b': mod_b, 'w_in': w_in, 'mla_q_norm_g': mla_q_norm_g,
            'mla_kv_norm_g': mla_kv_norm_g, 'mla_w_uq': mla_w_uq, 'mla_w_ukv': mla_w_ukv,
            'mla_qn_g': mla_qn_g, 'mla_kn_g': mla_kn_g, 'ssd_conv_w': ssd_conv_w, 'ssd_conv_b': ssd_conv_b,
            'ssd_a_log': ssd_a_log, 'ssd_dt_bias': ssd_dt_bias, 'ssd_d': ssd_d, 'ssd_norm_g': ssd_norm_g,
            'pool_w': pool_w, 'pool_scale': pool_scale, 'w_out': w_out, 'router_w': router_w,
            'router_b': router_b, 'moe_w_gate': moe_w_gate, 'moe_b_gate': moe_b_gate,
            'moe_w_up': moe_w_up, 'moe_b_up': moe_b_up, 'moe_w_down': moe_w_down, 'moe_b_down': moe_b_down}


def reference(x, c, ctx, c_ctx, norm1_g, norm2_g, mod_w, mod_b, w_in, mla_q_norm_g, mla_kv_norm_g,
              mla_w_uq, mla_w_ukv, mla_qn_g, mla_kn_g, ssd_conv_w, ssd_conv_b, ssd_a_log, ssd_dt_bias,
              ssd_d, ssd_norm_g, pool_w, pool_scale, w_out, router_w, router_b, moe_w_gate, moe_b_gate,
              moe_w_up, moe_b_up, moe_w_down, moe_b_down):
    n = x.shape[1]
    n_ctx = ctx.shape[1]
    cos, sin = axial_rope_angles(n)
    rope = (cos[None, :, None], sin[None, :, None])
    silu_c = jax.nn.silu(c)
    silu_cc = jax.nn.silu(c_ctx)
    for i in range(DEPTH):
        need_ctx = i < DEPTH - 1
        lp = {'w_in': w_in[i], 'q_lat_g': mla_q_norm_g[i], 'kv_lat_g': mla_kv_norm_g[i],
              'w_uq': mla_w_uq[i], 'w_ukv': mla_w_ukv[i], 'qn_g': mla_qn_g[i], 'kn_g': mla_kn_g[i],
              'conv_w': ssd_conv_w[i], 'conv_b': ssd_conv_b[i], 'a_log': ssd_a_log[i],
              'dt_bias': ssd_dt_bias[i], 'd_skip': ssd_d[i], 'ssd_norm_g': ssd_norm_g[i],
              'pool_w': pool_w[i], 'pool_scale': pool_scale[i], 'w_out': w_out[i]}
        m_l = [m[:, None, :] for m in jnp.split(silu_c @ mod_w[i] + mod_b[i], 6, axis=-1)]
        m_c = jnp.split(silu_cc @ mod_w[i] + mod_b[i], 6, axis=-1)
        h_l = modulate(x, norm1_g[i], m_l[0], m_l[1])
        h_c = modulate(ctx, norm1_g[i], m_c[0], m_c[1])
        mix_l, mix_c = token_mixers(h_l, h_c, rope, lp, need_ctx)
        x = x + m_l[2] * mix_l
        h_l = modulate(x, norm2_g[i], m_l[3], m_l[4])
        moe_args = (router_w[i], router_b[i], moe_w_gate[i], moe_b_gate[i], moe_w_up[i], moe_b_up[i],
                    moe_w_down[i], moe_b_down[i])
        if need_ctx:
            ctx = ctx + m_c[2] * mix_c
            h_c = modulate(ctx, norm2_g[i], m_c[3], m_c[4])
            f = moe_ffn(jnp.concatenate([h_c, h_l], axis=1), *moe_args)
            ctx = ctx + m_c[5] * f[:, :n_ctx]
            x = x + m_l[5] * f[:, n_ctx:]
        else:
            x = x + m_l[5] * moe_ffn(h_l, *moe_args)
    return x
```

```python
import functools

import jax
import jax.numpy as jnp
from jax import lax
from jax.experimental import pallas as pl
from jax.experimental.pallas import tpu as pltpu

F32 = jnp.float32
BF16 = jnp.bfloat16
HIGHEST = lax.Precision.HIGHEST

D_MODEL = 1024
GRID_W = 64
EPS = 1e-6
MLA_HEADS = 6
QK_NOPE = 64
QK_ROPE = 32
QK_DIM = QK_NOPE + QK_ROPE
V_HEAD = 64
Q_LORA = 256
KV_LORA = 128
MLA_WIDTH = MLA_HEADS * V_HEAD
ROPE_THETA = 10000.0
SSD_HEADS = 6
SSD_HEAD_DIM = 64
SSD_WIDTH = SSD_HEADS * SSD_HEAD_DIM
SSD_GROUPS = 2
SSD_STATE = 64
SSD_CONV = 4
SSD_CONV_DIM = SSD_WIDTH + 2 * SSD_GROUPS * SSD_STATE
POOL_WINDOWS = (2, 4, 8, 16)
POOL_GROUP = 64
POOL_WIDTH = len(POOL_WINDOWS) * POOL_GROUP
N_EXPERTS = 32
TOP_K = 4
D_FF = 1024
SWIGLU_LIMIT = 7.0
SWIGLU_ALPHA = 1.702

LANES = 128
SUBLANES = 8
HEAD_PAD = LANES

ROW_TILE = 256
SSD_CHUNK = 256
HALO = SUBLANES
ATT_TQ = 1024
ATT_TK = 1024
MOE_TILE = 256
VMEM_LIMIT = 56 * 1024 * 1024

COL_QLAT = 0
COL_KVLAT = COL_QLAT + Q_LORA
COL_KROPE = COL_KVLAT + KV_LORA
COL_Z = COL_KROPE + LANES
COL_XBC = COL_Z + SSD_WIDTH
COL_DT = COL_XBC + SSD_CONV_DIM
COL_POOL = COL_DT + LANES
IN_COLS_PACKED = COL_POOL + POOL_WIDTH


def _rms(x):
    return x * lax.rsqrt(jnp.mean(x * x, axis=-1, keepdims=True) + EPS)


def _params(*sem):
    return pltpu.CompilerParams(dimension_semantics=sem, vmem_limit_bytes=VMEM_LIMIT)


def _mod_kernel(c_ref, w_ref, b_ref, o_ref):
    c = c_ref[...]
    s = c / (1.0 + jnp.exp(-c))
    o_ref[0] = jnp.dot(s, w_ref[0], precision=HIGHEST, preferred_element_type=F32) + b_ref[0]


def _modulation(cvec, mod_w, mod_b):
    depth, d, cols = mod_w.shape
    tn = 1536
    return pl.pallas_call(
        _mod_kernel,
        grid=(depth, cols // tn),
        in_specs=[pl.BlockSpec((SUBLANES, d), lambda l, j: (0, 0)),
                  pl.BlockSpec((1, d, tn), lambda l, j: (l, 0, j)),
                  pl.BlockSpec((1, 1, tn), lambda l, j: (l, 0, j))],
        out_specs=pl.BlockSpec((1, SUBLANES, tn), lambda l, j: (l, 0, j)),
        out_shape=jax.ShapeDtypeStruct((depth, SUBLANES, cols), F32),
        compiler_params=_params("arbitrary", "arbitrary"),
        name="modulation",
    )(cvec, mod_w, mod_b.reshape(depth, 1, cols))


def _inproj_kernel(x_ref, mod_ref, g1_ref, win_ref, qg_ref, kvg_ref, wuq_ref, wuk_ref, wuv_ref,
                   qng_ref, kng_ref, rc_ref, rsa_ref, rsb_ref,
                   q_ref, k_ref, v_ref, z_ref, xbc_ref, dt_ref, pool_ref):
    mod = mod_ref[0]
    shift, scale = mod[0:1], mod[1:2]
    h = _rms(x_ref[...]) * g1_ref[...] * (1.0 + scale) + shift
    proj = jnp.dot(h.astype(BF16), win_ref[...], preferred_element_type=F32)
    z_ref[...] = proj[:, COL_Z:COL_XBC]
    xbc_ref[...] = proj[:, COL_XBC:COL_DT]
    dt_ref[...] = proj[:, COL_DT:COL_POOL]
    pool_ref[...] = proj[:, COL_POOL:IN_COLS_PACKED]

    rc, rsa, rsb = rc_ref[...], rsa_ref[...], rsb_ref[...]

    def head_norm_rope(t, gain):
        ms = jnp.sum(t * t, axis=-1, keepdims=True) * (1.0 / QK_DIM)
        t = t * lax.rsqrt(ms + EPS) * gain
        return (t * rc + pltpu.roll(t, QK_ROPE // 4, axis=1) * rsa
                + pltpu.roll(t, HEAD_PAD - QK_ROPE // 4, axis=1) * rsb)

    qn = (_rms(proj[:, COL_QLAT:COL_KVLAT]) * qg_ref[...]).astype(BF16)
    q_all = jnp.dot(qn, wuq_ref[...], preferred_element_type=F32)
    kvn = (_rms(proj[:, COL_KVLAT:COL_KROPE]) * kvg_ref[...]).astype(BF16)
    k_all = jnp.dot(kvn, wuk_ref[...], preferred_element_type=F32)
    k_rope = proj[:, COL_KROPE:COL_Z]
    v_ref[...] = jnp.dot(kvn, wuv_ref[...], preferred_element_type=F32).astype(BF16)
    qng, kng = qng_ref[...], kng_ref[...]
    for hd in range(MLA_HEADS):
        sl = slice(hd * HEAD_PAD, (hd + 1) * HEAD_PAD)
        q_ref[:, sl] = (head_norm_rope(q_all[:, sl], qng) * (QK_DIM ** -0.5)).astype(BF16)
        k_ref[:, sl] = head_norm_rope(k_all[:, sl] + k_rope, kng).astype(BF16)


def _in_projection(x, modtab, g1, w_in_p, qg, kvg, wuq_p, wuk_p, wuv, qng_p, kng_p, rope_c, rope_sa, rope_sb,
                   n_lat_tiles):
    t_rows, d = x.shape
    nt = t_rows // ROW_TILE
    row = lambda cols: pl.BlockSpec((ROW_TILE, cols), lambda i: (i, 0))
    full = lambda a: pl.BlockSpec(a.shape, lambda i: (0,) * a.ndim)
    out_cols = (MLA_HEADS * HEAD_PAD, MLA_HEADS * HEAD_PAD, MLA_WIDTH, SSD_WIDTH, SSD_CONV_DIM, LANES, POOL_WIDTH)
    out_dt = (BF16, BF16, BF16, F32, F32, F32, F32)
    return pl.pallas_call(
        _inproj_kernel,
        grid=(nt,),
        in_specs=[row(d),
                  pl.BlockSpec((1, SUBLANES, d), lambda i: (i // n_lat_tiles, 0, 0)),
                  full(g1), full(w_in_p), full(qg), full(kvg), full(wuq_p), full(wuk_p), full(wuv),
                  full(qng_p), full(kng_p), row(HEAD_PAD), row(HEAD_PAD), row(HEAD_PAD)],
        out_specs=[row(c) for c in out_cols],
        out_shape=[jax.ShapeDtypeStruct((t_rows, c), dt) for c, dt in zip(out_cols, out_dt)],
        compiler_params=_params("arbitrary"),
        name="in_projection",
    )(x, modtab, g1, w_in_p, qg, kvg, wuq_p, wuk_p, wuv, qng_p, kng_p, rope_c, rope_sa, rope_sb)


def _attention_kernel(q_ref, k_ref, v_ref, o_ref, m_ref, l_ref, acc_ref, *, chunks):
    tq = q_ref.shape[0]
    lane = lax.broadcasted_iota(jnp.int32, (tq, 2 * V_HEAD), 1)
    first = lane < V_HEAD
    m_ref[...] = jnp.full(m_ref.shape, -jnp.inf, F32)
    l_ref[...] = jnp.zeros(l_ref.shape, F32)
    acc_ref[...] = jnp.zeros(acc_ref.shape, F32)

    def step(start, size):
        v = v_ref[pl.ds(start, size), :]
        alphas, pvs = [], []
        for hh in range(2):
            q = q_ref[:, hh * HEAD_PAD:(hh + 1) * HEAD_PAD]
            k = k_ref[pl.ds(start, size), hh * HEAD_PAD:(hh + 1) * HEAD_PAD]
            s = lax.dot_general(q, k, (((1,), (1,)), ((), ())), preferred_element_type=F32)
            m_old = m_ref[hh]
            m_new = jnp.maximum(m_old, jnp.max(s, axis=-1, keepdims=True))
            p = jnp.exp(s - m_new)
            alpha = jnp.exp(m_old - m_new)
            l_ref[hh] = alpha * l_ref[hh] + jnp.sum(p, axis=-1, keepdims=True)
            m_ref[hh] = m_new
            alphas.append(alpha)
            pvs.append(jnp.dot(p.astype(BF16), v, preferred_element_type=F32))
        acc_ref[...] = (jnp.where(first, alphas[0], alphas[1]) * acc_ref[...]
                        + jnp.where(first, pvs[0], pvs[1]))

    (start0, size0, count0), tail = chunks
    if count0 > 0:
        def body(c, carry):
            step(pl.multiple_of(start0 + c * size0, size0), size0)
            return carry
        lax.fori_loop(0, count0, body, 0)
    for start, size in tail:
        step(start, size)
    inv = jnp.where(first, 1.0 / l_ref[0], 1.0 / l_ref[1])
    o_ref[...] = (acc_ref[...] * inv).astype(o_ref.dtype)


def _attention(q, k, v, n_lat, n_ctx):
    t_rows = q.shape[0]
    pairs = MLA_HEADS // 2
    scratch = lambda tq: [pltpu.VMEM((2, tq, 1), F32), pltpu.VMEM((2, tq, 1), F32),
                          pltpu.VMEM((tq, 2 * V_HEAD), F32)]
    lat = pl.pallas_call(
        functools.partial(_attention_kernel, chunks=((0, ATT_TK, n_lat // ATT_TK), ((n_lat, n_ctx),))),
        grid=(pairs, n_lat // ATT_TQ),
        in_specs=[pl.BlockSpec((ATT_TQ, 2 * HEAD_PAD), lambda p, i: (i, p)),
                  pl.BlockSpec((t_rows, 2 * HEAD_PAD), lambda p, i: (0, p)),
                  pl.BlockSpec((t_rows, 2 * V_HEAD), lambda p, i: (0, p))],
        out_specs=pl.BlockSpec((ATT_TQ, 2 * V_HEAD), lambda p, i: (i, p)),
        out_shape=jax.ShapeDtypeStruct((n_lat, MLA_WIDTH), BF16),
        scratch_shapes=scratch(ATT_TQ),
        compiler_params=_params("arbitrary", "arbitrary"),
        name="attention_latent",
    )(q, k, v)
    cblk = n_lat // n_ctx
    ctx = pl.pallas_call(
        functools.partial(_attention_kernel, chunks=((0, n_ctx, 0), ((0, n_ctx),))),
        grid=(pairs,),
        in_specs=[pl.BlockSpec((n_ctx, 2 * HEAD_PAD), lambda p: (cblk, p)),
                  pl.BlockSpec((n_ctx, 2 * HEAD_PAD), lambda p: (cblk, p)),
                  pl.BlockSpec((n_ctx, 2 * V_HEAD), lambda p: (cblk, p))],
        out_specs=pl.BlockSpec((n_ctx, 2 * V_HEAD), lambda p: (0, p)),
        out_shape=jax.ShapeDtypeStruct((n_ctx, MLA_WIDTH), BF16),
        scratch_shapes=scratch(n_ctx),
        compiler_params=_params("arbitrary"),
        name="attention_context",
    )(q, k, v)
    return jnp.concatenate([lat, ctx], axis=0)


def _halo_specs(cols, tile_rows, n_lat_tiles, n_tiles):
    per = tile_rows // HALO
    last = n_tiles * per - 1
    return [pl.BlockSpec((HALO, cols), lambda i: (jnp.maximum(i * per - 1, 0), 0)),
            pl.BlockSpec((tile_rows, cols), lambda i: (i, 0)),
            pl.BlockSpec((HALO, cols), lambda i: (jnp.minimum((i + 1) * per, last), 0))]


def _with_halo(prev_ref, cur_ref, next_ref, i, n_lat_tiles):
    has_prev = jnp.logical_and(i != 0, i != n_lat_tiles)
    has_next = jnp.logical_and(i != n_lat_tiles - 1, i != pl.num_programs(0) - 1)
    prev = jnp.where(has_prev, prev_ref[...], 0.0)
    nxt = jnp.where(has_next, next_ref[...], 0.0)
    return jnp.concatenate([prev, cur_ref[...], nxt], axis=0)


def _conv_kernel(prev_ref, cur_ref, next_ref, w_ref, b_ref, o_ref, *, n_lat_tiles):
    i = pl.program_id(0)
    ext = _with_halo(prev_ref, cur_ref, next_ref, i, n_lat_tiles)
    rows = ext.shape[0]
    w = w_ref[...]
    y = (pltpu.roll(ext, 2, axis=0) * w[0:1] + pltpu.roll(ext, 1, axis=0) * w[1:2]
         + ext * w[2:3] + pltpu.roll(ext, rows - 1, axis=0) * w[3:4])
    y = y[HALO:rows - HALO] + b_ref[...]
    o_ref[...] = y / (1.0 + jnp.exp(-y))


def _conv(xbc, conv_w, conv_b, n_lat_tiles):
    t_rows, cols = xbc.shape
    nt = t_rows // ROW_TILE
    return pl.pallas_call(
        functools.partial(_conv_kernel, n_lat_tiles=n_lat_tiles),
        grid=(nt,),
        in_specs=_halo_specs(cols, ROW_TILE, n_lat_tiles, nt) + [
            pl.BlockSpec(conv_w.shape, lambda i: (0, 0)), pl.BlockSpec(conv_b.shape, lambda i: (0, 0))],
        out_specs=pl.BlockSpec((ROW_TILE, cols), lambda i: (i, 0)),
        out_shape=jax.ShapeDtypeStruct((t_rows, cols), F32),
        compiler_params=_params("arbitrary"),
        name="ssd_conv",
    )(xbc, xbc, xbc, conv_w, conv_b)


def _softplus(v):
    return jnp.maximum(v, 0.0) + jnp.log(1.0 + jnp.exp(-jnp.abs(v)))


def _ssd_kernel(xf_ref, xb_ref, dtf_ref, dtb_ref, dttf_ref, dttb_ref, bias_ref, biast_ref, alog_ref, alogt_ref,
                dskip_ref, yf_ref, yb_ref, state_ref):
    L = xf_ref.shape[0]
    P, N, H = SSD_HEAD_DIM, SSD_STATE, SSD_HEADS

    @pl.when(pl.program_id(0) == 0)
    def _():
        state_ref[...] = jnp.zeros(state_ref.shape, F32)

    r = lax.broadcasted_iota(jnp.int32, (L, L), 0)
    c = lax.broadcasted_iota(jnp.int32, (L, L), 1)
    lower = r >= c
    upper = r <= c
    lower_f = lower.astype(F32)
    upper_f = upper.astype(F32)
    a_row = -jnp.exp(alog_ref[...])
    a_col = -jnp.exp(alogt_ref[...])

    def one_direction(x_ref, dt_ref, dtt_ref, y_ref, base, forward):
        x = x_ref[...]
        dt = _softplus(dt_ref[...] + bias_ref[...])
        dtt = _softplus(dtt_ref[...] + biast_ref[...])
        tri_col = lower_f if forward else upper_f
        tri_row = upper_f if forward else lower_f
        cs = jnp.dot(tri_col, dt * a_row, precision=HIGHEST, preferred_element_type=F32)
        cst = jnp.dot(dtt * a_col, tri_row, precision=HIGHEST, preferred_element_type=F32)
        mask = lower if forward else upper
        end = L - 1 if forward else 0
        outs = []
        cb = []
        for g in range(SSD_GROUPS):
            bg = x[:, SSD_WIDTH + g * N:SSD_WIDTH + (g + 1) * N].astype(BF16)
            cg = x[:, SSD_WIDTH + (SSD_GROUPS + g) * N:SSD_WIDTH + (SSD_GROUPS + g + 1) * N].astype(BF16)
            cb.append((bg, cg, lax.dot_general(cg, bg, (((1,), (1,)), ((), ())), preferred_element_type=F32)))
        for hd in range(H):
            j = base + hd
            bg, cg, cbg = cb[hd // (H // SSD_GROUPS)]
            col = cs[:, j:j + 1]
            rowv = cst[j:j + 1, :]
            total = cs[end:end + 1, j:j + 1]
            decay = jnp.where(mask, jnp.exp(jnp.minimum(col - rowv, 0.0)), 0.0)
            xh = (x[:, hd * P:(hd + 1) * P] * dt[:, j:j + 1]).astype(BF16)
            y = jnp.dot((cbg * decay).astype(BF16), xh, preferred_element_type=F32)
            st = state_ref[j]
            y = y + jnp.dot(cg, st.astype(BF16), preferred_element_type=F32) * jnp.exp(col)
            bw = (bg.astype(F32) * jnp.exp(total - col)).astype(BF16)
            state_ref[j] = st * jnp.exp(total) + lax.dot_general(
                bw, xh, (((0,), (0,)), ((), ())), preferred_element_type=F32)
            outs.append(y)
        y_all = jnp.concatenate(outs, axis=1)
        if forward:
            y_all = y_all + x[:, :SSD_WIDTH] * dskip_ref[...]
        y_ref[...] = y_all

    one_direction(xf_ref, dtf_ref, dttf_ref, yf_ref, 0, True)
    one_direction(xb_ref, dtb_ref, dttb_ref, yb_ref, H, False)


def _ssd(xc, dt, dtt, bias_row, bias_col, alog_row, alog_col, dskip_row, n_lat_chunks):
    t_rows, cols = xc.shape
    L = SSD_CHUNK
    nc = t_rows // L
    fwd = lambda j: jnp.where(j == 0, n_lat_chunks, j - 1)
    bwd = lambda j: jnp.where(j == 0, n_lat_chunks, n_lat_chunks - j)
    small = lambda a: pl.BlockSpec(a.shape, lambda j: (0,) * a.ndim)
    return pl.pallas_call(
        _ssd_kernel,
        grid=(nc,),
        in_specs=[pl.BlockSpec((L, cols), lambda j: (fwd(j), 0)),
                  pl.BlockSpec((L, cols), lambda j: (bwd(j), 0)),
                  pl.BlockSpec((L, LANES), lambda j: (fwd(j), 0)),
                  pl.BlockSpec((L, LANES), lambda j: (bwd(j), 0)),
                  pl.BlockSpec((2 * SUBLANES, L), lambda j: (0, fwd(j))),
                  pl.BlockSpec((2 * SUBLANES, L), lambda j: (0, bwd(j))),
                  small(bias_row), small(bias_col), small(alog_row), small(alog_col), small(dskip_row)],
        out_specs=[pl.BlockSpec((L, SSD_WIDTH), lambda j: (fwd(j), 0)),
                   pl.BlockSpec((L, SSD_WIDTH), lambda j: (bwd(j), 0))],
        out_shape=[jax.ShapeDtypeStruct((t_rows, SSD_WIDTH), F32)] * 2,
        scratch_shapes=[pltpu.VMEM((2 * SSD_HEADS, SSD_STATE, SSD_HEAD_DIM), F32)],
        compiler_params=_params("arbitrary"),
        name="ssd_scan",
    )(xc, xc, dt, dt, dtt, dtt, bias_row, bias_col, alog_row, alog_col, dskip_row)


def _mixout_kernel(x_ref, mod_ref, att_ref, yf_ref, yb_ref, z_ref, pprev_ref, pcur_ref, pnext_ref,
                   ssdg_ref, poolw_ref, pools_ref, wout_ref, g2_ref, rw_ref, rb_ref,
                   x1_ref, h2_ref, logit_ref, *, n_lat_tiles, n_lat, n_ctx):
    i = pl.program_id(0)
    mod = mod_ref[0]
    z = z_ref[...]
    ssd = _rms((yf_ref[...] + yb_ref[...]) * (z / (1.0 + jnp.exp(-z)))) * ssdg_ref[...]
    ext = _with_halo(pprev_ref, pcur_ref, pnext_ref, i, n_lat_tiles)
    rows = ext.shape[0]
    tm = rows - 2 * HALO
    w2 = ext + pltpu.roll(ext, 1, axis=0)
    w4 = pltpu.roll(w2, 1, axis=0) + pltpu.roll(w2, rows - 1, axis=0)
    w8 = pltpu.roll(w4, 2, axis=0) + pltpu.roll(w4, rows - 2, axis=0)
    w16 = pltpu.roll(w8, 4, axis=0) + pltpu.roll(w8, rows - 4, axis=0)
    lane = lax.broadcasted_iota(jnp.int32, (tm, POOL_WIDTH), 1)
    grp = lane // POOL_GROUP
    sl = slice(HALO, rows - HALO)
    wsum = jnp.where(grp == 0, w2[sl], jnp.where(grp == 1, w4[sl], jnp.where(grp == 2, w8[sl], w16[sl])))
    is_ctx = i >= n_lat_tiles
    seg_len = jnp.where(is_ctx, n_ctx, n_lat)
    t = lax.broadcasted_iota(jnp.int32, (tm, POOL_WIDTH), 0) + jnp.where(is_ctx, i - n_lat_tiles, i) * tm
    half = jnp.left_shift(1, grp)
    lo = jnp.clip(t - half, 0, seg_len)
    hi = jnp.clip(t + half, 0, seg_len)
    p = wsum / (hi - lo).astype(F32) - pcur_ref[...]
    pool = jnp.dot(p.astype(BF16), poolw_ref[...], preferred_element_type=F32) * pools_ref[...]
    mix = (jnp.dot(att_ref[...], wout_ref[0:MLA_WIDTH], preferred_element_type=F32)
           + jnp.dot(ssd.astype(BF16), wout_ref[MLA_WIDTH:MLA_WIDTH + SSD_WIDTH], preferred_element_type=F32)
           + jnp.dot(pool.astype(BF16), wout_ref[MLA_WIDTH + SSD_WIDTH:], preferred_element_type=F32))
    x1 = x_ref[...] + mod[2:3] * mix
    x1_ref[...] = x1
    h2 = _rms(x1) * g2_ref[...] * (1.0 + mod[4:5]) + mod[3:4]
    h2_ref[...] = h2.astype(BF16)
    logit_ref[...] = jnp.dot(h2, rw_ref[...], precision=HIGHEST, preferred_element_type=F32) + rb_ref[...]


def _mixer_output(x, modtab, att, yf, yb, z, pool_in, ssdg, poolw_bd, pools, wout, g2, rw_p, rb_p,
                  n_lat_tiles, n_lat, n_ctx):
    t_rows, d = x.shape
    nt = t_rows // ROW_TILE
    row = lambda cols: pl.BlockSpec((ROW_TILE, cols), lambda i: (i, 0))
    full = lambda a: pl.BlockSpec(a.shape, lambda i: (0,) * a.ndim)
    return pl.pallas_call(
        functools.partial(_mixout_kernel, n_lat_tiles=n_lat_tiles, n_lat=n_lat, n_ctx=n_ctx),
        grid=(nt,),
        in_specs=[row(d), pl.BlockSpec((1, SUBLANES, d), lambda i: (i // n_lat_tiles, 0, 0)),
                  row(MLA_WIDTH), row(SSD_WIDTH), row(SSD_WIDTH), row(SSD_WIDTH)]
                 + _halo_specs(POOL_WIDTH, ROW_TILE, n_lat_tiles, nt)
                 + [full(ssdg), full(poolw_bd), full(pools), full(wout), full(g2), full(rw_p), full(rb_p)],
        out_specs=[row(d), row(d), row(LANES)],
        out_shape=[jax.ShapeDtypeStruct((t_rows, d), F32), jax.ShapeDtypeStruct((t_rows, d), BF16),
                   jax.ShapeDtypeStruct((t_rows, LANES), F32)],
        compiler_params=_params("arbitrary"),
        name="mixer_output",
    )(x, modtab, att, yf, yb, z, pool_in, pool_in, pool_in, ssdg, poolw_bd, pools, wout, g2, rw_p, rb_p)


def _expert_kernel(be_ref, nu_ref, x_ref, rw_ref, wg_ref, bg_ref, wu_ref, bu_ref, wd_ref, bd_ref, o_ref):
    i = pl.program_id(0)

    @pl.when(i < nu_ref[0])
    def _():
        x = x_ref[...]
        g = jnp.minimum(jnp.dot(x, wg_ref[0], preferred_element_type=F32) + bg_ref[0], SWIGLU_LIMIT)
        u = jnp.clip(jnp.dot(x, wu_ref[0], preferred_element_type=F32) + bu_ref[0], -SWIGLU_LIMIT, SWIGLU_LIMIT)
        a = g / (1.0 + jnp.exp(-SWIGLU_ALPHA * g)) * (u + 1.0)
        y = jnp.dot(a.astype(BF16), wd_ref[0], preferred_element_type=F32) + bd_ref[0]
        o_ref[...] = (y * rw_ref[...]).astype(o_ref.dtype)

    @pl.when(i >= nu_ref[0])
    def _():
        o_ref[...] = jnp.zeros(o_ref.shape, o_ref.dtype)


def _experts(x_sorted, row_w, block_expert, n_used, wg, bg, wu, bu, wd, bd):
    n_rows, d = x_sorted.shape
    ne, _, dff = wg.shape
    rowspec = lambda cols: pl.BlockSpec((MOE_TILE, cols), lambda i, be, nu: (i, 0))
    wspec = lambda a, b: pl.BlockSpec((1, a, b), lambda i, be, nu: (be[i], 0, 0))
    grid_spec = pltpu.PrefetchScalarGridSpec(
        num_scalar_prefetch=2,
        grid=(n_rows // MOE_TILE,),
        in_specs=[rowspec(d), rowspec(1), wspec(d, dff), wspec(1, dff), wspec(d, dff), wspec(1, dff),
                  wspec(dff, d), wspec(1, d)],
        out_specs=rowspec(d),
    )
    return pl.pallas_call(
        _expert_kernel,
        grid_spec=grid_spec,
        out_shape=jax.ShapeDtypeStruct((n_rows, d), BF16),
        compiler_params=_params("arbitrary"),
        name="moe_experts",
    )(block_expert, n_used, x_sorted, row_w, wg, bg.reshape(ne, 1, dff), wu, bu.reshape(ne, 1, dff),
      wd, bd.reshape(ne, 1, d))


def _route(logits):
    t_rows = logits.shape[0]
    n_assign = t_rows * TOP_K
    top_logit, top_idx = lax.top_k(logits, TOP_K)
    top_w = jax.nn.softmax(top_logit, axis=-1)
    expert = top_idx.reshape(-1)
    onehot = (expert[:, None] == jnp.arange(N_EXPERTS, dtype=jnp.int32)[None, :]).astype(jnp.int32)
    csum = jnp.cumsum(onehot, axis=0)
    rank = jnp.take_along_axis(csum, expert[:, None], axis=1)[:, 0] - 1
    counts = csum[-1]
    padded = (counts + MOE_TILE - 1) // MOE_TILE * MOE_TILE
    pad_end = jnp.cumsum(padded)
    pad_start = pad_end - padded
    dest = pad_start[expert] + rank
    n_rows = n_assign + N_EXPERTS * MOE_TILE
    n_blocks = n_rows // MOE_TILE
    token = jnp.arange(n_assign, dtype=jnp.int32) // TOP_K
    row_token = jnp.zeros((n_rows,), jnp.int32).at[dest].set(token)
    row_w = jnp.zeros((n_rows,), F32).at[dest].set(top_w.reshape(-1))
    block_expert = jnp.minimum(
        jnp.searchsorted(pad_end, jnp.arange(n_blocks, dtype=jnp.int32) * MOE_TILE, side='right'),
        N_EXPERTS - 1).astype(jnp.int32)
    n_used = (pad_end[-1:] // MOE_TILE).astype(jnp.int32)
    return row_token, row_w, dest.reshape(t_rows, TOP_K), block_expert, n_used


def _rope_tables(n_lat, n_ctx):
    rows = n_lat // GRID_W
    row = jnp.repeat(jnp.arange(rows), GRID_W).astype(F32)
    col = jnp.tile(jnp.arange(GRID_W), rows).astype(F32)
    n_freq = QK_ROPE // 4
    inv_freq = ROPE_THETA ** (-jnp.arange(n_freq, dtype=F32) / n_freq)
    ang = jnp.stack([row[:, None] * inv_freq, col[:, None] * inv_freq], axis=1)
    cos, sin = jnp.cos(ang), jnp.sin(ang)
    zero = jnp.zeros_like(sin)
    ones = lambda w: jnp.ones((n_lat, w), F32)
    zeros = lambda w: jnp.zeros((n_lat, w), F32)
    per_axis = lambda a, b: jnp.stack([a, b], axis=2).reshape(n_lat, QK_ROPE)
    tail = HEAD_PAD - QK_DIM
    c = jnp.concatenate([ones(QK_NOPE), per_axis(cos, cos), ones(tail)], axis=1)
    sa = jnp.concatenate([zeros(QK_NOPE), per_axis(zero, sin), zeros(tail)], axis=1)
    sb = jnp.concatenate([zeros(QK_NOPE), per_axis(-sin, zero), zeros(tail)], axis=1)
    pad = lambda a, v: jnp.concatenate([a, jnp.full((n_ctx, HEAD_PAD), v, F32)], axis=0)
    return pad(c, 1.0), pad(sa, 0.0), pad(sb, 0.0)


def _pad_cols(a, width):
    return jnp.pad(a, ((0, 0), (0, width - a.shape[1])))


def _pack_layer(i, w_in, q_g, kv_g, w_uq, w_ukv, qn_g, kn_g, conv_w, conv_b, a_log, dt_bias, d_skip, ssd_g,
                pool_w, pool_scale, w_out, router_w, router_b):
    d = w_in.shape[1]
    pts = [0, Q_LORA, KV_LORA, QK_ROPE, SSD_WIDTH, SSD_CONV_DIM, 2 * SSD_HEADS, POOL_WIDTH]
    offs = [sum(pts[:j + 1]) for j in range(len(pts))]
    seg = lambda j: w_in[i][:, offs[j]:offs[j + 1]]
    zc = lambda w: jnp.zeros((d, w), F32)
    w_in_p = jnp.concatenate([
        seg(0), seg(1), zc(QK_NOPE), seg(2), zc(LANES - QK_DIM), seg(3), seg(4),
        seg(5), zc(LANES - 2 * SSD_HEADS), seg(6)], axis=1).astype(BF16)
    wuq = w_uq[i].reshape(Q_LORA, MLA_HEADS, QK_DIM)
    wuq_p = jnp.pad(wuq, ((0, 0), (0, 0), (0, HEAD_PAD - QK_DIM))).reshape(Q_LORA, MLA_HEADS * HEAD_PAD)
    wukv = w_ukv[i].reshape(KV_LORA, MLA_HEADS, QK_NOPE + V_HEAD)
    wuk_p = jnp.pad(wukv[:, :, :QK_NOPE], ((0, 0), (0, 0), (0, HEAD_PAD - QK_NOPE))).reshape(
        KV_LORA, MLA_HEADS * HEAD_PAD)
    wuv = wukv[:, :, QK_NOPE:].reshape(KV_LORA, MLA_WIDTH)
    flat12 = lambda a: a[i].reshape(1, 2 * SSD_HEADS)
    poolw_bd = jax.scipy.linalg.block_diag(*[pool_w[i][g] for g in range(len(POOL_WINDOWS))])
    return dict(
        w_in_p=w_in_p, qg=q_g[i][None], kvg=kv_g[i][None],
        wuq_p=wuq_p.astype(BF16), wuk_p=wuk_p.astype(BF16), wuv=wuv.astype(BF16),
        qng_p=_pad_cols(qn_g[i][None], HEAD_PAD), kng_p=_pad_cols(kn_g[i][None], HEAD_PAD),
        conv_w=conv_w[i], conv_b=conv_b[i][None],
        bias_row=_pad_cols(flat12(dt_bias), LANES), bias_col=_pad_cols(flat12(dt_bias), 2 * SUBLANES).T,
        alog_row=_pad_cols(flat12(a_log), LANES), alog_col=_pad_cols(flat12(a_log), 2 * SUBLANES).T,
        dskip_row=jnp.repeat(d_skip[i], SSD_HEAD_DIM)[None],
        ssdg=ssd_g[i][None], poolw_bd=poolw_bd.astype(BF16), pools=pool_scale[i][None],
        wout=w_out[i].astype(BF16), rw_p=_pad_cols(router_w[i], LANES), rb_p=_pad_cols(router_b[i][None], LANES))


def kernel(x, c, ctx, c_ctx, norm1_g, norm2_g, mod_w, mod_b, w_in, mla_q_norm_g, mla_kv_norm_g, mla_w_uq, mla_w_ukv, mla_qn_g, mla_kn_g, ssd_conv_w, ssd_conv_b, ssd_a_log, ssd_dt_bias, ssd_d, ssd_norm_g, pool_w, pool_scale, w_out, router_w, router_b, moe_w_gate, moe_b_gate, moe_w_up, moe_b_up, moe_w_down, moe_b_down):
    batch, n_lat, d = x.shape
    n_ctx = ctx.shape[1]
    depth = mod_w.shape[0]
    assert batch == 1 and d == D_MODEL and n_ctx == SSD_CHUNK == ROW_TILE
    assert n_lat % ATT_TQ == 0 and n_lat % ATT_TK == 0 and n_lat % GRID_W == 0
    n_lat_tiles = n_lat // ROW_TILE

    cvec = jnp.zeros((SUBLANES, d), F32).at[0].set(c[0]).at[1].set(c_ctx)
    mods = _modulation(cvec, mod_w, mod_b)
    rope_c, rope_sa, rope_sb = _rope_tables(n_lat, n_ctx)

    xs = jnp.concatenate([x[0], ctx[0]], axis=0)
    for i in range(depth):
        lp = _pack_layer(i, w_in, mla_q_norm_g, mla_kv_norm_g, mla_w_uq, mla_w_ukv, mla_qn_g, mla_kn_g,
                         ssd_conv_w, ssd_conv_b, ssd_a_log, ssd_dt_bias, ssd_d, ssd_norm_g, pool_w, pool_scale,
                         w_out, router_w, router_b)
        modtab = jnp.pad(mods[i, :2].reshape(2, 6, d), ((0, 0), (0, SUBLANES - 6), (0, 0)))
        q, k, v, z, xbc, dt, pool_in = _in_projection(
            xs, modtab, norm1_g[i][None], lp['w_in_p'], lp['qg'], lp['kvg'], lp['wuq_p'], lp['wuk_p'], lp['wuv'],
            lp['qng_p'], lp['kng_p'], rope_c, rope_sa, rope_sb, n_lat_tiles)
        att = _attention(q, k, v, n_lat, n_ctx)
        xc = _conv(xbc, lp['conv_w'], lp['conv_b'], n_lat_tiles)
        dtt = dt[:, :2 * SUBLANES].T
        yf, yb = _ssd(xc, dt, dtt, lp['bias_row'], lp['bias_col'], lp['alog_row'], lp['alog_col'],
                      lp['dskip_row'], n_lat // SSD_CHUNK)
        x1, h2, logits = _mixer_output(
            xs, modtab, att, yf, yb, z, pool_in, lp['ssdg'], lp['poolw_bd'], lp['pools'], lp['wout'],
            norm2_g[i][None], lp['rw_p'], lp['rb_p'], n_lat_tiles, n_lat, n_ctx)
        row_token, row_w, dest, block_expert, n_used = _route(logits[:, :N_EXPERTS])
        x_sorted = jnp.take(h2, row_token, axis=0)
        out = _experts(x_sorted, row_w[:, None], block_expert, n_used,
                       moe_w_gate[i].astype(BF16), moe_b_gate[i], moe_w_up[i].astype(BF16), moe_b_up[i],
                       moe_w_down[i].astype(BF16), moe_b_down[i])
        y = jnp.sum(jnp.take(out, dest, axis=0).astype(F32), axis=1)
        gate2 = jnp.where(jnp.arange(xs.shape[0])[:, None] < n_lat, modtab[0, 5][None], modtab[1, 5][None])
        xs = x1 + gate2 * y
    return xs[:n_lat][None]
```

```python
import functools

import jax
import jax.numpy as jnp
from jax import lax
from jax.experimental import pallas as pl
from jax.experimental.pallas import tpu as pltpu

F32 = jnp.float32
BF16 = jnp.bfloat16
HIGHEST = lax.Precision.HIGHEST

D_MODEL = 1024
GRID_W = 64
EPS = 1e-6
MLA_HEADS = 6
QK_NOPE = 64
QK_ROPE = 32
QK_DIM = QK_NOPE + QK_ROPE
V_HEAD = 64
Q_LORA = 256
KV_LORA = 128
MLA_WIDTH = MLA_HEADS * V_HEAD
ROPE_THETA = 10000.0
SSD_HEADS = 6
SSD_HEAD_DIM = 64
SSD_WIDTH = SSD_HEADS * SSD_HEAD_DIM
SSD_GROUPS = 2
SSD_STATE = 64
SSD_CONV = 4
SSD_CONV_DIM = SSD_WIDTH + 2 * SSD_GROUPS * SSD_STATE
POOL_WINDOWS = (2, 4, 8, 16)
POOL_GROUP = 64
POOL_WIDTH = len(POOL_WINDOWS) * POOL_GROUP
N_EXPERTS = 32
TOP_K = 4
D_FF = 1024
SWIGLU_LIMIT = 7.0
SWIGLU_ALPHA = 1.702

LANES = 128
SUBLANES = 8
HEAD_PAD = LANES
Q_SCALE = QK_DIM ** -0.5 * 1.4426950408889634

ROW_TILE = 256
SSD_CHUNK = 256
HALO = SUBLANES
ATT_TQ = 1024
ATT_TK = 512
MOE_TILE = 256
VMEM_LIMIT = 56 * 1024 * 1024

COL_QLAT = 0
COL_KVLAT = COL_QLAT + Q_LORA
COL_KROPE = COL_KVLAT + KV_LORA
COL_Z = COL_KROPE + LANES
COL_XBC = COL_Z + SSD_WIDTH
COL_DT = COL_XBC + SSD_CONV_DIM
COL_POOL = COL_DT + LANES
IN_COLS_PACKED = COL_POOL + POOL_WIDTH


def _rms(x):
    return x * lax.rsqrt(jnp.mean(x * x, axis=-1, keepdims=True) + EPS)


def _params(*sem):
    return pltpu.CompilerParams(dimension_semantics=sem, vmem_limit_bytes=VMEM_LIMIT)


def _mod_kernel(c_ref, w_ref, b_ref, o_ref):
    c = c_ref[...]
    s = c / (1.0 + jnp.exp(-c))
    o_ref[0] = jnp.dot(s, w_ref[0], precision=HIGHEST, preferred_element_type=F32) + b_ref[0]


def _modulation(cvec, mod_w, mod_b):
    depth, d, cols = mod_w.shape
    tn = 1536
    return pl.pallas_call(
        _mod_kernel,
        grid=(depth, cols // tn),
        in_specs=[pl.BlockSpec((SUBLANES, d), lambda l, j: (0, 0)),
                  pl.BlockSpec((1, d, tn), lambda l, j: (l, 0, j)),
                  pl.BlockSpec((1, 1, tn), lambda l, j: (l, 0, j))],
        out_specs=pl.BlockSpec((1, SUBLANES, tn), lambda l, j: (l, 0, j)),
        out_shape=jax.ShapeDtypeStruct((depth, SUBLANES, cols), F32),
        compiler_params=_params("arbitrary", "arbitrary"),
        name="modulation",
    )(cvec, mod_w, mod_b.reshape(depth, 1, cols))


def _inproj_kernel(x_ref, mod_ref, g1_ref, win_ref, qg_ref, kvg_ref, wuq_ref, wuk_ref, wuv_ref,
                   qng_ref, kng_ref, rc_ref, rsa_ref, rsb_ref,
                   q_ref, k_ref, v_ref, z_ref, xbc_ref, dt_ref, pool_ref):
    mod = mod_ref[0]
    shift, scale = mod[0:1], mod[1:2]
    h = _rms(x_ref[...]) * g1_ref[...] * (1.0 + scale) + shift
    proj = jnp.dot(h.astype(BF16), win_ref[...], preferred_element_type=F32)
    z_ref[...] = proj[:, COL_Z:COL_XBC]
    xbc_ref[...] = proj[:, COL_XBC:COL_DT]
    dt_ref[...] = proj[:, COL_DT:COL_POOL]
    pool_ref[...] = proj[:, COL_POOL:IN_COLS_PACKED]

    rc, rsa, rsb = rc_ref[...], rsa_ref[...], rsb_ref[...]

    def head_norm_rope(t, gain):
        ms = jnp.sum(t * t, axis=-1, keepdims=True) * (1.0 / QK_DIM)
        t = t * lax.rsqrt(ms + EPS) * gain
        return (t * rc + pltpu.roll(t, QK_ROPE // 4, axis=1) * rsa
                + pltpu.roll(t, HEAD_PAD - QK_ROPE // 4, axis=1) * rsb)

    qn = (_rms(proj[:, COL_QLAT:COL_KVLAT]) * qg_ref[...]).astype(BF16)
    q_all = jnp.dot(qn, wuq_ref[...], preferred_element_type=F32)
    kvn = (_rms(proj[:, COL_KVLAT:COL_KROPE]) * kvg_ref[...]).astype(BF16)
    k_all = jnp.dot(kvn, wuk_ref[...], preferred_element_type=F32)
    k_rope = proj[:, COL_KROPE:COL_Z]
    v_all = jnp.dot(kvn, wuv_ref[...], preferred_element_type=F32)
    vlane = lax.broadcasted_iota(jnp.int32, v_all.shape, 1) % HEAD_PAD
    v_ref[...] = jnp.where(vlane == V_HEAD, 1.0, v_all).astype(BF16)
    qng, kng = qng_ref[...], kng_ref[...]
    for hd in range(MLA_HEADS):
        sl = slice(hd * HEAD_PAD, (hd + 1) * HEAD_PAD)
        q_ref[:, sl] = (head_norm_rope(q_all[:, sl], qng) * Q_SCALE).astype(BF16)
        k_ref[:, sl] = head_norm_rope(k_all[:, sl] + k_rope, kng).astype(BF16)


def _in_projection(x, modtab, g1, w_in_p, qg, kvg, wuq_p, wuk_p, wuv, qng_p, kng_p, rope_c, rope_sa, rope_sb,
                   n_lat_tiles):
    t_rows, d = x.shape
    nt = t_rows // ROW_TILE
    row = lambda cols: pl.BlockSpec((ROW_TILE, cols), lambda i: (i, 0))
    full = lambda a: pl.BlockSpec(a.shape, lambda i: (0,) * a.ndim)
    out_cols = (MLA_HEADS * HEAD_PAD,) * 3 + (SSD_WIDTH, SSD_CONV_DIM, LANES, POOL_WIDTH)
    out_dt = (BF16, BF16, BF16, F32, F32, F32, F32)
    return pl.pallas_call(
        _inproj_kernel,
        grid=(nt,),
        in_specs=[row(d),
                  pl.BlockSpec((1, SUBLANES, d), lambda i: (i // n_lat_tiles, 0, 0)),
                  full(g1), full(w_in_p), full(qg), full(kvg), full(wuq_p), full(wuk_p), full(wuv),
                  full(qng_p), full(kng_p), row(HEAD_PAD), row(HEAD_PAD), row(HEAD_PAD)],
        out_specs=[row(c) for c in out_cols],
        out_shape=[jax.ShapeDtypeStruct((t_rows, c), dt) for c, dt in zip(out_cols, out_dt)],
        compiler_params=_params("arbitrary"),
        name="in_projection",
    )(x, modtab, g1, w_in_p, qg, kvg, wuq_p, wuk_p, wuv, qng_p, kng_p, rope_c, rope_sa, rope_sb)


def _attention_kernel(q_ref, k_ref, v_ref, o_ref, sa_ref, sb_ref, st_ref, m_ref, acc_ref, *, n_main, tk, tail):
    tq = q_ref.shape[0]
    m_ref[...] = jnp.full(m_ref.shape, -jnp.inf, F32)
    acc_ref[...] = jnp.zeros(acc_ref.shape, F32)

    def scores(s_ref, start, size):
        for hh in range(2):
            q = q_ref[:, hh * HEAD_PAD:(hh + 1) * HEAD_PAD]
            k = k_ref[pl.ds(start, size), hh * HEAD_PAD:(hh + 1) * HEAD_PAD]
            s_ref[hh] = lax.dot_general(q, k, (((1,), (1,)), ((), ())), preferred_element_type=F32)

    def consume(s_ref, start, size):
        for hh in range(2):
            s = s_ref[hh]
            m_old = m_ref[hh]
            m_new = jnp.maximum(m_old, jnp.max(s, axis=-1, keepdims=True))
            p = jnp.exp2(s - m_new).astype(BF16)
            v = v_ref[pl.ds(start, size), hh * HEAD_PAD:(hh + 1) * HEAD_PAD]
            acc_ref[hh] = jnp.exp2(m_old - m_new) * acc_ref[hh] + jnp.dot(p, v, preferred_element_type=F32)
            m_ref[hh] = m_new

    chunk = lambda c: pl.multiple_of(c * tk, tk)
    t_start, t_size = tail
    scores(st_ref, t_start, t_size)
    if n_main > 0:
        scores(sa_ref, 0, tk)
    consume(st_ref, t_start, t_size)
    if n_main > 0:
        def body(j, carry):
            scores(sb_ref, chunk(2 * j + 1), tk)
            consume(sa_ref, chunk(2 * j), tk)
            scores(sa_ref, chunk(2 * j + 2), tk)
            consume(sb_ref, chunk(2 * j + 1), tk)
            return carry
        lax.fori_loop(0, n_main // 2 - 1, body, 0)
        scores(sb_ref, (n_main - 1) * tk, tk)
        consume(sa_ref, (n_main - 2) * tk, tk)
        consume(sb_ref, (n_main - 1) * tk, tk)
    outs = []
    for hh in range(2):
        acc = acc_ref[hh]
        outs.append(acc / acc[:, V_HEAD:V_HEAD + 1])
    lane = lax.broadcasted_iota(jnp.int32, (tq, HEAD_PAD), 1)
    o_ref[...] = jnp.where(lane < V_HEAD, outs[0], pltpu.roll(outs[1], V_HEAD, axis=1)).astype(o_ref.dtype)


def _attention(q, k, v, n_lat, n_ctx):
    t_rows = q.shape[0]
    pairs = MLA_HEADS // 2
    n_main = n_lat // ATT_TK
    assert n_main % 2 == 0 and n_main >= 2

    def scratch(tq, tk, t_size):
        return [pltpu.VMEM((2, tq, tk), F32), pltpu.VMEM((2, tq, tk), F32), pltpu.VMEM((2, tq, t_size), F32),
                pltpu.VMEM((2, tq, 1), F32), pltpu.VMEM((2, tq, HEAD_PAD), F32)]

    resident = lambda: pl.BlockSpec((t_rows, 2 * HEAD_PAD), lambda p, i: (0, p), pipeline_mode=pl.Buffered(1))
    lat = pl.pallas_call(
        functools.partial(_attention_kernel, n_main=n_main, tk=ATT_TK, tail=(n_lat, n_ctx)),
        grid=(pairs, n_lat // ATT_TQ),
        in_specs=[pl.BlockSpec((ATT_TQ, 2 * HEAD_PAD), lambda p, i: (i, p)), resident(), resident()],
        out_specs=pl.BlockSpec((ATT_TQ, 2 * V_HEAD), lambda p, i: (i, p)),
        out_shape=jax.ShapeDtypeStruct((n_lat, MLA_WIDTH), BF16),
        scratch_shapes=scratch(ATT_TQ, ATT_TK, n_ctx),
        compiler_params=_params("arbitrary", "arbitrary"),
        name="attention_latent",
    )(q, k, v)
    cblk = n_lat // n_ctx
    cspec = lambda: pl.BlockSpec((n_ctx, 2 * HEAD_PAD), lambda p: (cblk, p))
    ctx = pl.pallas_call(
        functools.partial(_attention_kernel, n_main=0, tk=SUBLANES, tail=(0, n_ctx)),
        grid=(pairs,),
        in_specs=[cspec(), cspec(), cspec()],
        out_specs=pl.BlockSpec((n_ctx, 2 * V_HEAD), lambda p: (0, p)),
        out_shape=jax.ShapeDtypeStruct((n_ctx, MLA_WIDTH), BF16),
        scratch_shapes=scratch(n_ctx, SUBLANES, n_ctx),
        compiler_params=_params("arbitrary"),
        name="attention_context",
    )(q, k, v)
    return jnp.concatenate([lat, ctx], axis=0)


def _halo_specs(cols, tile_rows, n_lat_tiles, n_tiles):
    per = tile_rows // HALO
    last = n_tiles * per - 1
    return [pl.BlockSpec((HALO, cols), lambda i: (jnp.maximum(i * per - 1, 0), 0)),
            pl.BlockSpec((tile_rows, cols), lambda i: (i, 0)),
            pl.BlockSpec((HALO, cols), lambda i: (jnp.minimum((i + 1) * per, last), 0))]


def _with_halo(prev_ref, cur_ref, next_ref, i, n_lat_tiles):
    has_prev = jnp.logical_and(i != 0, i != n_lat_tiles)
    has_next = jnp.logical_and(i != n_lat_tiles - 1, i != pl.num_programs(0) - 1)
    prev = jnp.where(has_prev, prev_ref[...], 0.0)
    nxt = jnp.where(has_next, next_ref[...], 0.0)
    return jnp.concatenate([prev, cur_ref[...], nxt], axis=0)


def _conv_kernel(prev_ref, cur_ref, next_ref, w_ref, b_ref, o_ref, *, n_lat_tiles):
    i = pl.program_id(0)
    ext = _with_halo(prev_ref, cur_ref, next_ref, i, n_lat_tiles)
    rows = ext.shape[0]
    w = w_ref[...]
    y = (pltpu.roll(ext, 2, axis=0) * w[0:1] + pltpu.roll(ext, 1, axis=0) * w[1:2]
         + ext * w[2:3] + pltpu.roll(ext, rows - 1, axis=0) * w[3:4])
    y = y[HALO:rows - HALO] + b_ref[...]
    o_ref[...] = y / (1.0 + jnp.exp(-y))


def _conv(xbc, conv_w, conv_b, n_lat_tiles):
    t_rows, cols = xbc.shape
    nt = t_rows // ROW_TILE
    return pl.pallas_call(
        functools.partial(_conv_kernel, n_lat_tiles=n_lat_tiles),
        grid=(nt,),
        in_specs=_halo_specs(cols, ROW_TILE, n_lat_tiles, nt) + [
            pl.BlockSpec(conv_w.shape, lambda i: (0, 0)), pl.BlockSpec(conv_b.shape, lambda i: (0, 0))],
        out_specs=pl.BlockSpec((ROW_TILE, cols), lambda i: (i, 0)),
        out_shape=jax.ShapeDtypeStruct((t_rows, cols), F32),
        compiler_params=_params("arbitrary"),
        name="ssd_conv",
    )(xbc, xbc, xbc, conv_w, conv_b)


def _softplus(v):
    return jnp.maximum(v, 0.0) + jnp.log(1.0 + jnp.exp(-jnp.abs(v)))


def _ssd_kernel(xf_ref, xb_ref, dtf_ref, dtb_ref, dttf_ref, dttb_ref, bias_ref, biast_ref, alog_ref, alogt_ref,
                dskip_ref, yf_ref, yb_ref, state_ref):
    L = xf_ref.shape[0]
    P, N, H = SSD_HEAD_DIM, SSD_STATE, SSD_HEADS

    @pl.when(pl.program_id(0) == 0)
    def _():
        state_ref[...] = jnp.zeros(state_ref.shape, F32)

    r = lax.broadcasted_iota(jnp.int32, (L, L), 0)
    c = lax.broadcasted_iota(jnp.int32, (L, L), 1)
    lower = r >= c
    upper = r <= c
    lower_f = lower.astype(F32)
    upper_f = upper.astype(F32)
    a_row = -jnp.exp(alog_ref[...])
    a_col = -jnp.exp(alogt_ref[...])

    def one_direction(x_ref, dt_ref, dtt_ref, y_ref, base, forward):
        x = x_ref[...]
        dt = _softplus(dt_ref[...] + bias_ref[...])
        dtt = _softplus(dtt_ref[...] + biast_ref[...])
        tri_col = lower_f if forward else upper_f
        tri_row = upper_f if forward else lower_f
        cs = jnp.dot(tri_col, dt * a_row, precision=HIGHEST, preferred_element_type=F32)
        cst = jnp.dot(dtt * a_col, tri_row, precision=HIGHEST, preferred_element_type=F32)
        mask = lower if forward else upper
        end = L - 1 if forward else 0
        outs = []
        cb = []
        for g in range(SSD_GROUPS):
            bg = x[:, SSD_WIDTH + g * N:SSD_WIDTH + (g + 1) * N].astype(BF16)
            cg = x[:, SSD_WIDTH + (SSD_GROUPS + g) * N:SSD_WIDTH + (SSD_GROUPS + g + 1) * N].astype(BF16)
            cb.append((bg, cg, lax.dot_general(cg, bg, (((1,), (1,)), ((), ())), preferred_element_type=F32)))
        for hd in range(H):
            j = base + hd
            bg, cg, cbg = cb[hd // (H // SSD_GROUPS)]
            col = cs[:, j:j + 1]
            rowv = cst[j:j + 1, :]
            total = cs[end:end + 1, j:j + 1]
            decay = jnp.where(mask, jnp.exp(jnp.minimum(col - rowv, 0.0)), 0.0)
            xh = (x[:, hd * P:(hd + 1) * P] * dt[:, j:j + 1]).astype(BF16)
            y = jnp.dot((cbg * decay).astype(BF16), xh, preferred_element_type=F32)
            st = state_ref[j]
            y = y + jnp.dot(cg, st.astype(BF16), preferred_element_type=F32) * jnp.exp(col)
            bw = (bg.astype(F32) * jnp.exp(total - col)).astype(BF16)
            state_ref[j] = st * jnp.exp(total) + lax.dot_general(
                bw, xh, (((0,), (0,)), ((), ())), preferred_element_type=F32)
            outs.append(y)
        y_all = jnp.concatenate(outs, axis=1)
        if forward:
            y_all = y_all + x[:, :SSD_WIDTH] * dskip_ref[...]
        y_ref[...] = y_all

    one_direction(xf_ref, dtf_ref, dttf_ref, yf_ref, 0, True)
    one_direction(xb_ref, dtb_ref, dttb_ref, yb_ref, H, False)


def _ssd(xc, dt, dtt, bias_row, bias_col, alog_row, alog_col, dskip_row, n_lat_chunks):
    t_rows, cols = xc.shape
    L = SSD_CHUNK
    nc = t_rows // L
    fwd = lambda j: jnp.where(j == 0, n_lat_chunks, j - 1)
    bwd = lambda j: jnp.where(j == 0, n_lat_chunks, n_lat_chunks - j)
    small = lambda a: pl.BlockSpec(a.shape, lambda j: (0,) * a.ndim)
    return pl.pallas_call(
        _ssd_kernel,
        grid=(nc,),
        in_specs=[pl.BlockSpec((L, cols), lambda j: (fwd(j), 0)),
                  pl.BlockSpec((L, cols), lambda j: (bwd(j), 0)),
                  pl.BlockSpec((L, LANES), lambda j: (fwd(j), 0)),
                  pl.BlockSpec((L, LANES), lambda j: (bwd(j), 0)),
                  pl.BlockSpec((2 * SUBLANES, L), lambda j: (0, fwd(j))),
                  pl.BlockSpec((2 * SUBLANES, L), lambda j: (0, bwd(j))),
                  small(bias_row), small(bias_col), small(alog_row), small(alog_col), small(dskip_row)],
        out_specs=[pl.BlockSpec((L, SSD_WIDTH), lambda j: (fwd(j), 0)),
                   pl.BlockSpec((L, SSD_WIDTH), lambda j: (bwd(j), 0))],
        out_shape=[jax.ShapeDtypeStruct((t_rows, SSD_WIDTH), F32)] * 2,
        scratch_shapes=[pltpu.VMEM((2 * SSD_HEADS, SSD_STATE, SSD_HEAD_DIM), F32)],
        compiler_params=_params("arbitrary"),
        name="ssd_scan",
    )(xc, xc, dt, dt, dtt, dtt, bias_row, bias_col, alog_row, alog_col, dskip_row)


def _mixout_kernel(x_ref, mod_ref, att_ref, yf_ref, yb_ref, z_ref, pprev_ref, pcur_ref, pnext_ref,
                   ssdg_ref, poolw_ref, pools_ref, wout_ref, g2_ref, rw_ref, rb_ref,
                   x1_ref, h2_ref, logit_ref, *, n_lat_tiles, n_lat, n_ctx):
    i = pl.program_id(0)
    mod = mod_ref[0]
    z = z_ref[...]
    ssd = _rms((yf_ref[...] + yb_ref[...]) * (z / (1.0 + jnp.exp(-z)))) * ssdg_ref[...]
    ext = _with_halo(pprev_ref, pcur_ref, pnext_ref, i, n_lat_tiles)
    rows = ext.shape[0]
    tm = rows - 2 * HALO
    w2 = ext + pltpu.roll(ext, 1, axis=0)
    w4 = pltpu.roll(w2, 1, axis=0) + pltpu.roll(w2, rows - 1, axis=0)
    w8 = pltpu.roll(w4, 2, axis=0) + pltpu.roll(w4, rows - 2, axis=0)
    w16 = pltpu.roll(w8, 4, axis=0) + pltpu.roll(w8, rows - 4, axis=0)
    lane = lax.broadcasted_iota(jnp.int32, (tm, POOL_WIDTH), 1)
    grp = lane // POOL_GROUP
    sl = slice(HALO, rows - HALO)
    wsum = jnp.where(grp == 0, w2[sl], jnp.where(grp == 1, w4[sl], jnp.where(grp == 2, w8[sl], w16[sl])))
    is_ctx = i >= n_lat_tiles
    seg_len = jnp.where(is_ctx, n_ctx, n_lat)
    t = lax.broadcasted_iota(jnp.int32, (tm, POOL_WIDTH), 0) + jnp.where(is_ctx, i - n_lat_tiles, i) * tm
    half = jnp.left_shift(1, grp)
    lo = jnp.clip(t - half, 0, seg_len)
    hi = jnp.clip(t + half, 0, seg_len)
    p = wsum / (hi - lo).astype(F32) - pcur_ref[...]
    pool = jnp.dot(p.astype(BF16), poolw_ref[...], preferred_element_type=F32) * pools_ref[...]
    mix = (jnp.dot(att_ref[...], wout_ref[0:MLA_WIDTH], preferred_element_type=F32)
           + jnp.dot(ssd.astype(BF16), wout_ref[MLA_WIDTH:MLA_WIDTH + SSD_WIDTH], preferred_element_type=F32)
           + jnp.dot(pool.astype(BF16), wout_ref[MLA_WIDTH + SSD_WIDTH:], preferred_element_type=F32))
    x1 = x_ref[...] + mod[2:3] * mix
    x1_ref[...] = x1
    h2 = _rms(x1) * g2_ref[...] * (1.0 + mod[4:5]) + mod[3:4]
    h2_ref[...] = h2.astype(BF16)
    logit_ref[...] = jnp.dot(h2, rw_ref[...], precision=HIGHEST, preferred_element_type=F32) + rb_ref[...]


def _mixer_output(x, modtab, att, yf, yb, z, pool_in, ssdg, poolw_bd, pools, wout, g2, rw_p, rb_p,
                  n_lat_tiles, n_lat, n_ctx):
    t_rows, d = x.shape
    nt = t_rows // ROW_TILE
    row = lambda cols: pl.BlockSpec((ROW_TILE, cols), lambda i: (i, 0))
    full = lambda a: pl.BlockSpec(a.shape, lambda i: (0,) * a.ndim)
    return pl.pallas_call(
        functools.partial(_mixout_kernel, n_lat_tiles=n_lat_tiles, n_lat=n_lat, n_ctx=n_ctx),
        grid=(nt,),
        in_specs=[row(d), pl.BlockSpec((1, SUBLANES, d), lambda i: (i // n_lat_tiles, 0, 0)),
                  row(MLA_WIDTH), row(SSD_WIDTH), row(SSD_WIDTH), row(SSD_WIDTH)]
                 + _halo_specs(POOL_WIDTH, ROW_TILE, n_lat_tiles, nt)
                 + [full(ssdg), full(poolw_bd), full(pools), full(wout), full(g2), full(rw_p), full(rb_p)],
        out_specs=[row(d), row(d), row(LANES)],
        out_shape=[jax.ShapeDtypeStruct((t_rows, d), F32), jax.ShapeDtypeStruct((t_rows, d), BF16),
                   jax.ShapeDtypeStruct((t_rows, LANES), F32)],
        compiler_params=_params("arbitrary"),
        name="mixer_output",
    )(x, modtab, att, yf, yb, z, pool_in, pool_in, pool_in, ssdg, poolw_bd, pools, wout, g2, rw_p, rb_p)


def _expert_kernel(be_ref, nu_ref, x_ref, rw_ref, wg_ref, bg_ref, wu_ref, bu_ref, wd_ref, bd_ref, o_ref,
                   wg_s, wu_s, wd_s):
    i = pl.program_id(0)

    @pl.when(jnp.logical_or(i == 0, be_ref[i] != be_ref[jnp.maximum(i - 1, 0)]))
    def _():
        wg_s[...] = wg_ref[0, 0].astype(BF16)
        wu_s[...] = wu_ref[0, 0].astype(BF16)
        wd_s[...] = wd_ref[0, 0].astype(BF16)

    @pl.when(i < nu_ref[0])
    def _():
        x = x_ref[...]
        g = jnp.minimum(jnp.dot(x, wg_s[...], preferred_element_type=F32) + bg_ref[0, 0], SWIGLU_LIMIT)
        u = jnp.clip(jnp.dot(x, wu_s[...], preferred_element_type=F32) + bu_ref[0, 0],
                     -SWIGLU_LIMIT, SWIGLU_LIMIT)
        a = g / (1.0 + jnp.exp(-SWIGLU_ALPHA * g)) * (u + 1.0)
        y = jnp.dot(a.astype(BF16), wd_s[...], preferred_element_type=F32) + bd_ref[0, 0]
        o_ref[...] = (y * rw_ref[...]).astype(o_ref.dtype)

    @pl.when(i >= nu_ref[0])
    def _():
        o_ref[...] = jnp.zeros(o_ref.shape, o_ref.dtype)


def _experts(layer, x_sorted, row_w, block_expert, n_used, wg, bg, wu, bu, wd, bd):
    n_rows, d = x_sorted.shape
    depth, ne, _, dff = wg.shape
    rowspec = lambda cols: pl.BlockSpec((MOE_TILE, cols), lambda i, be, nu: (i, 0))
    wspec = lambda a, b: pl.BlockSpec((1, 1, a, b), lambda i, be, nu: (layer, be[i], 0, 0))
    grid_spec = pltpu.PrefetchScalarGridSpec(
        num_scalar_prefetch=2,
        grid=(n_rows // MOE_TILE,),
        in_specs=[rowspec(d), rowspec(1), wspec(d, dff), wspec(1, dff), wspec(d, dff), wspec(1, dff),
                  wspec(dff, d), wspec(1, d)],
        out_specs=rowspec(d),
        scratch_shapes=[pltpu.VMEM((d, dff), BF16), pltpu.VMEM((d, dff), BF16), pltpu.VMEM((dff, d), BF16)],
    )
    return pl.pallas_call(
        _expert_kernel,
        grid_spec=grid_spec,
        out_shape=jax.ShapeDtypeStruct((n_rows, d), BF16),
        compiler_params=_params("arbitrary"),
        name="moe_experts",
    )(block_expert, n_used, x_sorted, row_w, wg, bg.reshape(depth, ne, 1, dff), wu, bu.reshape(depth, ne, 1, dff),
      wd, bd.reshape(depth, ne, 1, d))


def _route(logits):
    t_rows = logits.shape[0]
    n_assign = t_rows * TOP_K
    top_logit, top_idx = lax.top_k(logits, TOP_K)
    top_w = jax.nn.softmax(top_logit, axis=-1)
    expert = top_idx.T.reshape(-1).astype(jnp.int32)
    weight = top_w.T.reshape(-1)
    order = jnp.argsort(expert, stable=True).astype(jnp.int32)
    rank = jnp.argsort(order).astype(jnp.int32)
    counts = jnp.sum((expert[:, None] == jnp.arange(N_EXPERTS, dtype=jnp.int32)[None, :]).astype(jnp.int32), axis=0)
    start = jnp.cumsum(counts) - counts
    padded = (counts + MOE_TILE - 1) // MOE_TILE * MOE_TILE
    pad_end = jnp.cumsum(padded)
    pad_start = pad_end - padded
    dest = rank + (pad_start - start)[expert]
    n_rows = n_assign + N_EXPERTS * MOE_TILE
    n_blocks = n_rows // MOE_TILE
    block_expert = jnp.minimum(
        jnp.sum((jnp.arange(n_blocks, dtype=jnp.int32)[:, None] * MOE_TILE >= pad_end[None, :]).astype(jnp.int32),
                axis=1), N_EXPERTS - 1).astype(jnp.int32)
    row = jnp.arange(n_rows, dtype=jnp.int32)
    row_expert = jnp.repeat(block_expert, MOE_TILE)
    within = row - pad_start[row_expert]
    valid = within < counts[row_expert]
    src = order[jnp.clip(start[row_expert] + within, 0, n_assign - 1)]
    row_token = jnp.where(valid, src % t_rows, 0)
    row_w = jnp.where(valid, weight[src], 0.0)
    n_used = (pad_end[-1:] // MOE_TILE).astype(jnp.int32)
    return row_token, row_w, dest, block_expert, n_used


def _rope_tables(n_lat, n_ctx):
    rows = n_lat // GRID_W
    row = jnp.repeat(jnp.arange(rows), GRID_W).astype(F32)
    col = jnp.tile(jnp.arange(GRID_W), rows).astype(F32)
    n_freq = QK_ROPE // 4
    inv_freq = ROPE_THETA ** (-jnp.arange(n_freq, dtype=F32) / n_freq)
    ang = jnp.stack([row[:, None] * inv_freq, col[:, None] * inv_freq], axis=1)
    cos, sin = jnp.cos(ang), jnp.sin(ang)
    zero = jnp.zeros_like(sin)
    ones = lambda w: jnp.ones((n_lat, w), F32)
    zeros = lambda w: jnp.zeros((n_lat, w), F32)
    per_axis = lambda a, b: jnp.stack([a, b], axis=2).reshape(n_lat, QK_ROPE)
    tail = HEAD_PAD - QK_DIM
    c = jnp.concatenate([ones(QK_NOPE), per_axis(cos, cos), ones(tail)], axis=1)
    sa = jnp.concatenate([zeros(QK_NOPE), per_axis(zero, sin), zeros(tail)], axis=1)
    sb = jnp.concatenate([zeros(QK_NOPE), per_axis(-sin, zero), zeros(tail)], axis=1)
    pad = lambda a, v: jnp.concatenate([a, jnp.full((n_ctx, HEAD_PAD), v, F32)], axis=0)
    return pad(c, 1.0), pad(sa, 0.0), pad(sb, 0.0)


def _pad_cols(a, width):
    return jnp.pad(a, ((0, 0), (0, width - a.shape[1])))


def _pack_layer(i, w_in, q_g, kv_g, w_uq, w_ukv, qn_g, kn_g, conv_w, conv_b, a_log, dt_bias, d_skip, ssd_g,
                pool_w, pool_scale, w_out, router_w, router_b):
    d = w_in.shape[1]
    pts = [0, Q_LORA, KV_LORA, QK_ROPE, SSD_WIDTH, SSD_CONV_DIM, 2 * SSD_HEADS, POOL_WIDTH]
    offs = [sum(pts[:j + 1]) for j in range(len(pts))]
    seg = lambda j: w_in[i][:, offs[j]:offs[j + 1]]
    zc = lambda w: jnp.zeros((d, w), F32)
    w_in_p = jnp.concatenate([
        seg(0), seg(1), zc(QK_NOPE), seg(2), zc(LANES - QK_DIM), seg(3), seg(4),
        seg(5), zc(LANES - 2 * SSD_HEADS), seg(6)], axis=1).astype(BF16)
    wuq = w_uq[i].reshape(Q_LORA, MLA_HEADS, QK_DIM)
    wuq_p = jnp.pad(wuq, ((0, 0), (0, 0), (0, HEAD_PAD - QK_DIM))).reshape(Q_LORA, MLA_HEADS * HEAD_PAD)
    wukv = w_ukv[i].reshape(KV_LORA, MLA_HEADS, QK_NOPE + V_HEAD)
    wuk_p = jnp.pad(wukv[:, :, :QK_NOPE], ((0, 0), (0, 0), (0, HEAD_PAD - QK_NOPE))).reshape(
        KV_LORA, MLA_HEADS * HEAD_PAD)
    wuv = jnp.pad(wukv[:, :, QK_NOPE:], ((0, 0), (0, 0), (0, HEAD_PAD - V_HEAD))).reshape(
        KV_LORA, MLA_HEADS * HEAD_PAD)
    flat12 = lambda a: a[i].reshape(1, 2 * SSD_HEADS)
    poolw_bd = jax.scipy.linalg.block_diag(*[pool_w[i][g] for g in range(len(POOL_WINDOWS))])
    return dict(
        w_in_p=w_in_p, qg=q_g[i][None], kvg=kv_g[i][None],
        wuq_p=wuq_p.astype(BF16), wuk_p=wuk_p.astype(BF16), wuv=wuv.astype(BF16),
        qng_p=_pad_cols(qn_g[i][None], HEAD_PAD), kng_p=_pad_cols(kn_g[i][None], HEAD_PAD),
        conv_w=conv_w[i], conv_b=conv_b[i][None],
        bias_row=_pad_cols(flat12(dt_bias), LANES), bias_col=_pad_cols(flat12(dt_bias), 2 * SUBLANES).T,
        alog_row=_pad_cols(flat12(a_log), LANES), alog_col=_pad_cols(flat12(a_log), 2 * SUBLANES).T,
        dskip_row=jnp.repeat(d_skip[i], SSD_HEAD_DIM)[None],
        ssdg=ssd_g[i][None], poolw_bd=poolw_bd.astype(BF16), pools=pool_scale[i][None],
        wout=w_out[i].astype(BF16), rw_p=_pad_cols(router_w[i], LANES), rb_p=_pad_cols(router_b[i][None], LANES))


def kernel(x, c, ctx, c_ctx, norm1_g, norm2_g, mod_w, mod_b, w_in, mla_q_norm_g, mla_kv_norm_g, mla_w_uq, mla_w_ukv, mla_qn_g, mla_kn_g, ssd_conv_w, ssd_conv_b, ssd_a_log, ssd_dt_bias, ssd_d, ssd_norm_g, pool_w, pool_scale, w_out, router_w, router_b, moe_w_gate, moe_b_gate, moe_w_up, moe_b_up, moe_w_down, moe_b_down):
    batch, n_lat, d = x.shape
    n_ctx = ctx.shape[1]
    depth = mod_w.shape[0]
    assert batch == 1 and d == D_MODEL and n_ctx == SSD_CHUNK == ROW_TILE
    assert n_lat % ATT_TQ == 0 and n_lat % ATT_TK == 0 and n_lat % GRID_W == 0
    n_lat_tiles = n_lat // ROW_TILE

    cvec = jnp.zeros((SUBLANES, d), F32).at[0].set(c[0]).at[1].set(c_ctx)
    mods = _modulation(cvec, mod_w, mod_b)
    rope_c, rope_sa, rope_sb = _rope_tables(n_lat, n_ctx)

    xs = jnp.concatenate([x[0], ctx[0]], axis=0)
    for i in range(depth):
        lp = _pack_layer(i, w_in, mla_q_norm_g, mla_kv_norm_g, mla_w_uq, mla_w_ukv, mla_qn_g, mla_kn_g,
                         ssd_conv_w, ssd_conv_b, ssd_a_log, ssd_dt_bias, ssd_d, ssd_norm_g, pool_w, pool_scale,
                         w_out, router_w, router_b)
        modtab = jnp.pad(mods[i, :2].reshape(2, 6, d), ((0, 0), (0, SUBLANES - 6), (0, 0)))
        q, k, v, z, xbc, dt, pool_in = _in_projection(
            xs, modtab, norm1_g[i][None], lp['w_in_p'], lp['qg'], lp['kvg'], lp['wuq_p'], lp['wuk_p'], lp['wuv'],
            lp['qng_p'], lp['kng_p'], rope_c, rope_sa, rope_sb, n_lat_tiles)
        att = _attention(q, k, v, n_lat, n_ctx)
        xc = _conv(xbc, lp['conv_w'], lp['conv_b'], n_lat_tiles)
        dtt = dt[:, :2 * SUBLANES].T
        yf, yb = _ssd(xc, dt, dtt, lp['bias_row'], lp['bias_col'], lp['alog_row'], lp['alog_col'],
                      lp['dskip_row'], n_lat // SSD_CHUNK)
        x1, h2, logits = _mixer_output(
            xs, modtab, att, yf, yb, z, pool_in, lp['ssdg'], lp['poolw_bd'], lp['pools'], lp['wout'],
            norm2_g[i][None], lp['rw_p'], lp['rb_p'], n_lat_tiles, n_lat, n_ctx)
        row_token, row_w, dest, block_expert, n_used = _route(logits[:, :N_EXPERTS])
        x_sorted = jnp.take(h2, row_token, axis=0)
        out = _experts(i, x_sorted, row_w[:, None], block_expert, n_used,
                       moe_w_gate, moe_b_gate, moe_w_up, moe_b_up, moe_w_down, moe_b_down)
        y = jnp.sum(jnp.take(out, dest, axis=0).reshape(TOP_K, xs.shape[0], d).astype(F32), axis=0)
        gate2 = jnp.where(jnp.arange(xs.shape[0])[:, None] < n_lat, modtab[0, 5][None], modtab[1, 5][None])
        xs = x1 + gate2 * y
    return xs[:n_lat][None]
```

```python
import functools

import jax
import jax.numpy as jnp
from jax import lax
from jax.experimental import pallas as pl
from jax.experimental.pallas import tpu as pltpu

F32 = jnp.float32
BF16 = jnp.bfloat16
HIGHEST = lax.Precision.HIGHEST

D_MODEL = 1024
GRID_W = 64
EPS = 1e-6
MLA_HEADS = 6
QK_NOPE = 64
QK_ROPE = 32
QK_DIM = QK_NOPE + QK_ROPE
V_HEAD = 64
Q_LORA = 256
KV_LORA = 128
MLA_WIDTH = MLA_HEADS * V_HEAD
ROPE_THETA = 10000.0
SSD_HEADS = 6
SSD_HEAD_DIM = 64
SSD_WIDTH = SSD_HEADS * SSD_HEAD_DIM
SSD_GROUPS = 2
SSD_STATE = 64
SSD_CONV = 4
SSD_CONV_DIM = SSD_WIDTH + 2 * SSD_GROUPS * SSD_STATE
POOL_WINDOWS = (2, 4, 8, 16)
POOL_GROUP = 64
POOL_WIDTH = len(POOL_WINDOWS) * POOL_GROUP
N_EXPERTS = 32
TOP_K = 4
D_FF = 1024
SWIGLU_LIMIT = 7.0
SWIGLU_ALPHA = 1.702

LANES = 128
SUBLANES = 8
HEAD_PAD = LANES
Q_SCALE = QK_DIM ** -0.5 * 1.4426950408889634

ROW_TILE = 256
SSD_CHUNK = 256
HALO = SUBLANES
ATT_TQ = 1024
ATT_TK = 512
MOE_TILE = 256
VMEM_LIMIT = 56 * 1024 * 1024

COL_QLAT = 0
COL_KVLAT = COL_QLAT + Q_LORA
COL_KROPE = COL_KVLAT + KV_LORA
COL_Z = COL_KROPE + LANES
COL_XBC = COL_Z + SSD_WIDTH
COL_DT = COL_XBC + SSD_CONV_DIM
COL_POOL = COL_DT + LANES
IN_COLS_PACKED = COL_POOL + POOL_WIDTH


def _rms(x):
    return x * lax.rsqrt(jnp.mean(x * x, axis=-1, keepdims=True) + EPS)


def _params(*sem):
    return pltpu.CompilerParams(dimension_semantics=sem, vmem_limit_bytes=VMEM_LIMIT)


def _mod_kernel(c_ref, w_ref, b_ref, o_ref):
    c = c_ref[...]
    s = c / (1.0 + jnp.exp(-c))
    o_ref[0] = jnp.dot(s, w_ref[0], precision=HIGHEST, preferred_element_type=F32) + b_ref[0]


def _modulation(cvec, mod_w, mod_b):
    depth, d, cols = mod_w.shape
    tn = 1536
    return pl.pallas_call(
        _mod_kernel,
        grid=(depth, cols // tn),
        in_specs=[pl.BlockSpec((SUBLANES, d), lambda l, j: (0, 0)),
                  pl.BlockSpec((1, d, tn), lambda l, j: (l, 0, j)),
                  pl.BlockSpec((1, 1, tn), lambda l, j: (l, 0, j))],
        out_specs=pl.BlockSpec((1, SUBLANES, tn), lambda l, j: (l, 0, j)),
        out_shape=jax.ShapeDtypeStruct((depth, SUBLANES, cols), F32),
        compiler_params=_params("arbitrary", "arbitrary"),
        name="modulation",
    )(cvec, mod_w, mod_b.reshape(depth, 1, cols))


def _inproj_kernel(x_ref, mod_ref, g1_ref, win_ref, qg_ref, kvg_ref, wuq_ref, wuk_ref, wuv_ref,
                   qng_ref, kng_ref, rc_ref, rsa_ref, rsb_ref,
                   q_ref, k_ref, v_ref, z_ref, xbc_ref, dt_ref, pool_ref):
    mod = mod_ref[0]
    shift, scale = mod[0:1], mod[1:2]
    h = _rms(x_ref[...]) * g1_ref[...] * (1.0 + scale) + shift
    proj = jnp.dot(h.astype(BF16), win_ref[...], preferred_element_type=F32)
    z_ref[...] = proj[:, COL_Z:COL_XBC]
    xbc_ref[...] = proj[:, COL_XBC:COL_DT]
    dt_ref[...] = proj[:, COL_DT:COL_POOL]
    pool_ref[...] = proj[:, COL_POOL:IN_COLS_PACKED]

    rc, rsa, rsb = rc_ref[...], rsa_ref[...], rsb_ref[...]

    def head_norm_rope(t, gain):
        ms = jnp.sum(t * t, axis=-1, keepdims=True) * (1.0 / QK_DIM)
        t = t * lax.rsqrt(ms + EPS) * gain
        return (t * rc + pltpu.roll(t, QK_ROPE // 4, axis=1) * rsa
                + pltpu.roll(t, HEAD_PAD - QK_ROPE // 4, axis=1) * rsb)

    qn = (_rms(proj[:, COL_QLAT:COL_KVLAT]) * qg_ref[...]).astype(BF16)
    q_all = jnp.dot(qn, wuq_ref[...], preferred_element_type=F32)
    kvn = (_rms(proj[:, COL_KVLAT:COL_KROPE]) * kvg_ref[...]).astype(BF16)
    k_all = jnp.dot(kvn, wuk_ref[...], preferred_element_type=F32)
    k_rope = proj[:, COL_KROPE:COL_Z]
    v_all = jnp.dot(kvn, wuv_ref[...], preferred_element_type=F32)
    vlane = lax.broadcasted_iota(jnp.int32, v_all.shape, 1) % HEAD_PAD
    v_ref[...] = jnp.where(vlane == V_HEAD, 1.0, v_all).astype(BF16)
    qng, kng = qng_ref[...], kng_ref[...]
    for hd in range(MLA_HEADS):
        sl = slice(hd * HEAD_PAD, (hd + 1) * HEAD_PAD)
        q_ref[:, sl] = (head_norm_rope(q_all[:, sl], qng) * Q_SCALE).astype(BF16)
        k_ref[:, sl] = head_norm_rope(k_all[:, sl] + k_rope, kng).astype(BF16)


def _in_projection(x, modtab, g1, w_in_p, qg, kvg, wuq_p, wuk_p, wuv, qng_p, kng_p, rope_c, rope_sa, rope_sb,
                   n_lat_tiles):
    t_rows, d = x.shape
    nt = t_rows // ROW_TILE
    row = lambda cols: pl.BlockSpec((ROW_TILE, cols), lambda i: (i, 0))
    full = lambda a: pl.BlockSpec(a.shape, lambda i: (0,) * a.ndim)
    out_cols = (MLA_HEADS * HEAD_PAD,) * 3 + (SSD_WIDTH, SSD_CONV_DIM, LANES, POOL_WIDTH)
    out_dt = (BF16, BF16, BF16, F32, F32, F32, F32)
    return pl.pallas_call(
        _inproj_kernel,
        grid=(nt,),
        in_specs=[row(d),
                  pl.BlockSpec((1, SUBLANES, d), lambda i: (i // n_lat_tiles, 0, 0)),
                  full(g1), full(w_in_p), full(qg), full(kvg), full(wuq_p), full(wuk_p), full(wuv),
                  full(qng_p), full(kng_p), row(HEAD_PAD), row(HEAD_PAD), row(HEAD_PAD)],
        out_specs=[row(c) for c in out_cols],
        out_shape=[jax.ShapeDtypeStruct((t_rows, c), dt) for c, dt in zip(out_cols, out_dt)],
        compiler_params=_params("arbitrary"),
        name="in_projection",
    )(x, modtab, g1, w_in_p, qg, kvg, wuq_p, wuk_p, wuv, qng_p, kng_p, rope_c, rope_sa, rope_sb)


def _attention_kernel(q_ref, k_ref, v_ref, o_ref, sa_ref, sb_ref, st_ref, m_ref, acc_ref, *, n_main, tk, tail):
    tq = q_ref.shape[0]
    m_ref[...] = jnp.full(m_ref.shape, -jnp.inf, F32)
    acc_ref[...] = jnp.zeros(acc_ref.shape, F32)

    def scores(s_ref, start, size):
        for hh in range(2):
            q = q_ref[:, hh * HEAD_PAD:(hh + 1) * HEAD_PAD]
            k = k_ref[pl.ds(start, size), hh * HEAD_PAD:(hh + 1) * HEAD_PAD]
            s_ref[hh] = lax.dot_general(q, k, (((1,), (1,)), ((), ())), preferred_element_type=F32)

    def consume(s_ref, start, size):
        for hh in range(2):
            s = s_ref[hh]
            m_old = m_ref[hh]
            m_new = jnp.maximum(m_old, jnp.max(s, axis=-1, keepdims=True))
            p = jnp.exp2(s - m_new).astype(BF16)
            v = v_ref[pl.ds(start, size), hh * HEAD_PAD:(hh + 1) * HEAD_PAD]
            acc_ref[hh] = jnp.exp2(m_old - m_new) * acc_ref[hh] + jnp.dot(p, v, preferred_element_type=F32)
            m_ref[hh] = m_new

    chunk = lambda c: pl.multiple_of(c * tk, tk)
    t_start, t_size = tail
    scores(st_ref, t_start, t_size)
    if n_main > 0:
        scores(sa_ref, 0, tk)
    consume(st_ref, t_start, t_size)
    if n_main > 0:
        def body(j, carry):
            scores(sb_ref, chunk(2 * j + 1), tk)
            consume(sa_ref, chunk(2 * j), tk)
            scores(sa_ref, chunk(2 * j + 2), tk)
            consume(sb_ref, chunk(2 * j + 1), tk)
            return carry
        lax.fori_loop(0, n_main // 2 - 1, body, 0)
        scores(sb_ref, (n_main - 1) * tk, tk)
        consume(sa_ref, (n_main - 2) * tk, tk)
        consume(sb_ref, (n_main - 1) * tk, tk)
    outs = []
    for hh in range(2):
        acc = acc_ref[hh]
        outs.append(acc / acc[:, V_HEAD:V_HEAD + 1])
    lane = lax.broadcasted_iota(jnp.int32, (tq, HEAD_PAD), 1)
    o_ref[...] = jnp.where(lane < V_HEAD, outs[0], pltpu.roll(outs[1], V_HEAD, axis=1)).astype(o_ref.dtype)


def _attention(q, k, v, n_lat, n_ctx):
    t_rows = q.shape[0]
    pairs = MLA_HEADS // 2
    n_main = n_lat // ATT_TK
    assert n_main % 2 == 0 and n_main >= 2

    def scratch(tq, tk, t_size):
        return [pltpu.VMEM((2, tq, tk), F32), pltpu.VMEM((2, tq, tk), F32), pltpu.VMEM((2, tq, t_size), F32),
                pltpu.VMEM((2, tq, 1), F32), pltpu.VMEM((2, tq, HEAD_PAD), F32)]

    resident = lambda: pl.BlockSpec((t_rows, 2 * HEAD_PAD), lambda p, i: (0, p), pipeline_mode=pl.Buffered(1))
    lat = pl.pallas_call(
        functools.partial(_attention_kernel, n_main=n_main, tk=ATT_TK, tail=(n_lat, n_ctx)),
        grid=(pairs, n_lat // ATT_TQ),
        in_specs=[pl.BlockSpec((ATT_TQ, 2 * HEAD_PAD), lambda p, i: (i, p)), resident(), resident()],
        out_specs=pl.BlockSpec((ATT_TQ, 2 * V_HEAD), lambda p, i: (i, p)),
        out_shape=jax.ShapeDtypeStruct((n_lat, MLA_WIDTH), BF16),
        scratch_shapes=scratch(ATT_TQ, ATT_TK, n_ctx),
        compiler_params=_params("arbitrary", "arbitrary"),
        name="attention_latent",
    )(q, k, v)
    cblk = n_lat // n_ctx
    cspec = lambda: pl.BlockSpec((n_ctx, 2 * HEAD_PAD), lambda p: (cblk, p))
    ctx = pl.pallas_call(
        functools.partial(_attention_kernel, n_main=0, tk=SUBLANES, tail=(0, n_ctx)),
        grid=(pairs,),
        in_specs=[cspec(), cspec(), cspec()],
        out_specs=pl.BlockSpec((n_ctx, 2 * V_HEAD), lambda p: (0, p)),
        out_shape=jax.ShapeDtypeStruct((n_ctx, MLA_WIDTH), BF16),
        scratch_shapes=scratch(n_ctx, SUBLANES, n_ctx),
        compiler_params=_params("arbitrary"),
        name="attention_context",
    )(q, k, v)
    return jnp.concatenate([lat, ctx], axis=0)


def _halo_specs(cols, tile_rows, n_lat_tiles, n_tiles):
    per = tile_rows // HALO
    last = n_tiles * per - 1
    return [pl.BlockSpec((HALO, cols), lambda i: (jnp.maximum(i * per - 1, 0), 0)),
            pl.BlockSpec((tile_rows, cols), lambda i: (i, 0)),
            pl.BlockSpec((HALO, cols), lambda i: (jnp.minimum((i + 1) * per, last), 0))]


def _with_halo(prev_ref, cur_ref, next_ref, i, n_lat_tiles):
    has_prev = jnp.logical_and(i != 0, i != n_lat_tiles)
    has_next = jnp.logical_and(i != n_lat_tiles - 1, i != pl.num_programs(0) - 1)
    prev = jnp.where(has_prev, prev_ref[...], 0.0)
    nxt = jnp.where(has_next, next_ref[...], 0.0)
    return jnp.concatenate([prev, cur_ref[...], nxt], axis=0)


def _conv_kernel(prev_ref, cur_ref, next_ref, w_ref, b_ref, o_ref, *, n_lat_tiles):
    i = pl.program_id(0)
    ext = _with_halo(prev_ref, cur_ref, next_ref, i, n_lat_tiles)
    rows = ext.shape[0]
    w = w_ref[...]
    y = (pltpu.roll(ext, 2, axis=0) * w[0:1] + pltpu.roll(ext, 1, axis=0) * w[1:2]
         + ext * w[2:3] + pltpu.roll(ext, rows - 1, axis=0) * w[3:4])
    y = y[HALO:rows - HALO] + b_ref[...]
    o_ref[...] = y / (1.0 + jnp.exp(-y))


def _conv(xbc, conv_w, conv_b, n_lat_tiles):
    t_rows, cols = xbc.shape
    nt = t_rows // ROW_TILE
    return pl.pallas_call(
        functools.partial(_conv_kernel, n_lat_tiles=n_lat_tiles),
        grid=(nt,),
        in_specs=_halo_specs(cols, ROW_TILE, n_lat_tiles, nt) + [
            pl.BlockSpec(conv_w.shape, lambda i: (0, 0)), pl.BlockSpec(conv_b.shape, lambda i: (0, 0))],
        out_specs=pl.BlockSpec((ROW_TILE, cols), lambda i: (i, 0)),
        out_shape=jax.ShapeDtypeStruct((t_rows, cols), F32),
        compiler_params=_params("arbitrary"),
        name="ssd_conv",
    )(xbc, xbc, xbc, conv_w, conv_b)


def _softplus(v):
    return jnp.maximum(v, 0.0) + jnp.log(1.0 + jnp.exp(-jnp.abs(v)))


def _ssd_kernel(xf_ref, xb_ref, dtf_ref, dtb_ref, dttf_ref, dttb_ref, bias_ref, biast_ref, alog_ref, alogt_ref,
                dskip_ref, yf_ref, yb_ref, state_ref):
    L = xf_ref.shape[0]
    P, N, H = SSD_HEAD_DIM, SSD_STATE, SSD_HEADS

    @pl.when(pl.program_id(0) == 0)
    def _():
        state_ref[...] = jnp.zeros(state_ref.shape, F32)

    r = lax.broadcasted_iota(jnp.int32, (L, L), 0)
    c = lax.broadcasted_iota(jnp.int32, (L, L), 1)
    lower = r >= c
    upper = r <= c
    lower_f = lower.astype(F32)
    upper_f = upper.astype(F32)
    a_row = -jnp.exp(alog_ref[...])
    a_col = -jnp.exp(alogt_ref[...])

    def one_direction(x_ref, dt_ref, dtt_ref, y_ref, base, forward):
        x = x_ref[...]
        dt = _softplus(dt_ref[...] + bias_ref[...])
        dtt = _softplus(dtt_ref[...] + biast_ref[...])
        tri_col = lower_f if forward else upper_f
        tri_row = upper_f if forward else lower_f
        cs = jnp.dot(tri_col, dt * a_row, precision=HIGHEST, preferred_element_type=F32)
        cst = jnp.dot(dtt * a_col, tri_row, precision=HIGHEST, preferred_element_type=F32)
        mask = lower if forward else upper
        end = L - 1 if forward else 0
        outs = []
        cb = []
        for g in range(SSD_GROUPS):
            bg = x[:, SSD_WIDTH + g * N:SSD_WIDTH + (g + 1) * N].astype(BF16)
            cg = x[:, SSD_WIDTH + (SSD_GROUPS + g) * N:SSD_WIDTH + (SSD_GROUPS + g + 1) * N].astype(BF16)
            cb.append((bg, cg, lax.dot_general(cg, bg, (((1,), (1,)), ((), ())), preferred_element_type=F32)))
        for hd in range(H):
            j = base + hd
            bg, cg, cbg = cb[hd // (H // SSD_GROUPS)]
            col = cs[:, j:j + 1]
            rowv = cst[j:j + 1, :]
            total = cs[end:end + 1, j:j + 1]
            decay = jnp.where(mask, jnp.exp(jnp.minimum(col - rowv, 0.0)), 0.0)
            xh = (x[:, hd * P:(hd + 1) * P] * dt[:, j:j + 1]).astype(BF16)
            y = jnp.dot((cbg * decay).astype(BF16), xh, preferred_element_type=F32)
            st = state_ref[j]
            y = y + jnp.dot(cg, st.astype(BF16), preferred_element_type=F32) * jnp.exp(col)
            bw = (bg.astype(F32) * jnp.exp(total - col)).astype(BF16)
            state_ref[j] = st * jnp.exp(total) + lax.dot_general(
                bw, xh, (((0,), (0,)), ((), ())), preferred_element_type=F32)
            outs.append(y)
        y_all = jnp.concatenate(outs, axis=1)
        if forward:
            y_all = y_all + x[:, :SSD_WIDTH] * dskip_ref[...]
        y_ref[...] = y_all

    one_direction(xf_ref, dtf_ref, dttf_ref, yf_ref, 0, True)
    one_direction(xb_ref, dtb_ref, dttb_ref, yb_ref, H, False)


def _ssd(xc, dt, dtt, bias_row, bias_col, alog_row, alog_col, dskip_row, n_lat_chunks):
    t_rows, cols = xc.shape
    L = SSD_CHUNK
    nc = t_rows // L
    fwd = lambda j: jnp.where(j == 0, n_lat_chunks, j - 1)
    bwd = lambda j: jnp.where(j == 0, n_lat_chunks, n_lat_chunks - j)
    small = lambda a: pl.BlockSpec(a.shape, lambda j: (0,) * a.ndim)
    return pl.pallas_call(
        _ssd_kernel,
        grid=(nc,),
        in_specs=[pl.BlockSpec((L, cols), lambda j: (fwd(j), 0)),
                  pl.BlockSpec((L, cols), lambda j: (bwd(j), 0)),
                  pl.BlockSpec((L, LANES), lambda j: (fwd(j), 0)),
                  pl.BlockSpec((L, LANES), lambda j: (bwd(j), 0)),
                  pl.BlockSpec((2 * SUBLANES, L), lambda j: (0, fwd(j))),
                  pl.BlockSpec((2 * SUBLANES, L), lambda j: (0, bwd(j))),
                  small(bias_row), small(bias_col), small(alog_row), small(alog_col), small(dskip_row)],
        out_specs=[pl.BlockSpec((L, SSD_WIDTH), lambda j: (fwd(j), 0)),
                   pl.BlockSpec((L, SSD_WIDTH), lambda j: (bwd(j), 0))],
        out_shape=[jax.ShapeDtypeStruct((t_rows, SSD_WIDTH), F32)] * 2,
        scratch_shapes=[pltpu.VMEM((2 * SSD_HEADS, SSD_STATE, SSD_HEAD_DIM), F32)],
        compiler_params=_params("arbitrary"),
        name="ssd_scan",
    )(xc, xc, dt, dt, dtt, dtt, bias_row, bias_col, alog_row, alog_col, dskip_row)


def _mixout_kernel(x_ref, mod_ref, att_ref, yf_ref, yb_ref, z_ref, pprev_ref, pcur_ref, pnext_ref,
                   ssdg_ref, poolw_ref, pools_ref, wout_ref, g2_ref, rw_ref, rb_ref,
                   x1_ref, h2_ref, route_ref, cnt_ref, base_ref, *, n_lat_tiles, n_lat, n_ctx):
    i = pl.program_id(0)
    mod = mod_ref[0]
    z = z_ref[...]
    ssd = _rms((yf_ref[...] + yb_ref[...]) * (z / (1.0 + jnp.exp(-z)))) * ssdg_ref[...]
    ext = _with_halo(pprev_ref, pcur_ref, pnext_ref, i, n_lat_tiles)
    rows = ext.shape[0]
    tm = rows - 2 * HALO
    w2 = ext + pltpu.roll(ext, 1, axis=0)
    w4 = pltpu.roll(w2, 1, axis=0) + pltpu.roll(w2, rows - 1, axis=0)
    w8 = pltpu.roll(w4, 2, axis=0) + pltpu.roll(w4, rows - 2, axis=0)
    w16 = pltpu.roll(w8, 4, axis=0) + pltpu.roll(w8, rows - 4, axis=0)
    lane = lax.broadcasted_iota(jnp.int32, (tm, POOL_WIDTH), 1)
    grp = lane // POOL_GROUP
    sl = slice(HALO, rows - HALO)
    wsum = jnp.where(grp == 0, w2[sl], jnp.where(grp == 1, w4[sl], jnp.where(grp == 2, w8[sl], w16[sl])))
    is_ctx = i >= n_lat_tiles
    seg_len = jnp.where(is_ctx, n_ctx, n_lat)
    t = lax.broadcasted_iota(jnp.int32, (tm, POOL_WIDTH), 0) + jnp.where(is_ctx, i - n_lat_tiles, i) * tm
    half = jnp.left_shift(1, grp)
    lo = jnp.clip(t - half, 0, seg_len)
    hi = jnp.clip(t + half, 0, seg_len)
    p = wsum / (hi - lo).astype(F32) - pcur_ref[...]
    pool = jnp.dot(p.astype(BF16), poolw_ref[...], preferred_element_type=F32) * pools_ref[...]
    mix = (jnp.dot(att_ref[...], wout_ref[0:MLA_WIDTH], preferred_element_type=F32)
           + jnp.dot(ssd.astype(BF16), wout_ref[MLA_WIDTH:MLA_WIDTH + SSD_WIDTH], preferred_element_type=F32)
           + jnp.dot(pool.astype(BF16), wout_ref[MLA_WIDTH + SSD_WIDTH:], preferred_element_type=F32))
    x1 = x_ref[...] + mod[2:3] * mix
    x1_ref[...] = x1
    h2 = _rms(x1) * g2_ref[...] * (1.0 + mod[4:5]) + mod[3:4]
    h2b = h2.astype(BF16)
    for kk in range(TOP_K):
        h2_ref[kk] = h2b
    logits = jnp.dot(h2, rw_ref[...], precision=HIGHEST, preferred_element_type=F32) + rb_ref[...]

    @pl.when(i == 0)
    def _():
        base_ref[...] = jnp.zeros(base_ref.shape, F32)

    elane = lax.broadcasted_iota(jnp.int32, logits.shape, 1)
    lg = jnp.where(elane < N_EXPERTS, logits, -jnp.inf)
    rr = lax.broadcasted_iota(jnp.int32, (tm, tm), 0)
    cc = lax.broadcasted_iota(jnp.int32, (tm, tm), 1)
    earlier = (rr > cc).astype(BF16)
    offset = base_ref[...]
    tops, ids, ranks = [], [], []
    for kk in range(TOP_K):
        top = jnp.max(lg, axis=-1, keepdims=True)
        idx = jnp.min(jnp.where(lg == top, elane, LANES), axis=-1, keepdims=True)
        sel = elane == idx
        lg = jnp.where(sel, -jnp.inf, lg)
        onehot = sel.astype(BF16)
        before = jnp.dot(earlier, onehot, preferred_element_type=F32) + offset
        ranks.append(jnp.sum(jnp.where(sel, before, 0.0), axis=-1, keepdims=True))
        offset = offset + jnp.sum(sel.astype(F32), axis=0, keepdims=True)
        tops.append(top)
        ids.append(idx.astype(F32))
    base_ref[...] = offset
    cnt_ref[...] = jnp.broadcast_to(offset, cnt_ref.shape)
    exps = [jnp.exp(tp - tops[0]) for tp in tops]
    denom = exps[0] + exps[1] + exps[2] + exps[3]
    route = jnp.zeros(logits.shape, F32)
    for kk in range(TOP_K):
        route = jnp.where(elane == kk, ids[kk], route)
        route = jnp.where(elane == TOP_K + kk, exps[kk] / denom, route)
        route = jnp.where(elane == 2 * TOP_K + kk, ranks[kk], route)
    route_ref[...] = route


def _mixer_output(x, modtab, att, yf, yb, z, pool_in, ssdg, poolw_bd, pools, wout, g2, rw_p, rb_p,
                  n_lat_tiles, n_lat, n_ctx):
    t_rows, d = x.shape
    nt = t_rows // ROW_TILE
    row = lambda cols: pl.BlockSpec((ROW_TILE, cols), lambda i: (i, 0))
    full = lambda a: pl.BlockSpec(a.shape, lambda i: (0,) * a.ndim)
    return pl.pallas_call(
        functools.partial(_mixout_kernel, n_lat_tiles=n_lat_tiles, n_lat=n_lat, n_ctx=n_ctx),
        grid=(nt,),
        in_specs=[row(d), pl.BlockSpec((1, SUBLANES, d), lambda i: (i // n_lat_tiles, 0, 0)),
                  row(MLA_WIDTH), row(SSD_WIDTH), row(SSD_WIDTH), row(SSD_WIDTH)]
                 + _halo_specs(POOL_WIDTH, ROW_TILE, n_lat_tiles, nt)
                 + [full(ssdg), full(poolw_bd), full(pools), full(wout), full(g2), full(rw_p), full(rb_p)],
        out_specs=[row(d), pl.BlockSpec((TOP_K, ROW_TILE, d), lambda i: (0, i, 0)), row(LANES),
                   pl.BlockSpec((SUBLANES, LANES), lambda i: (0, 0))],
        out_shape=[jax.ShapeDtypeStruct((t_rows, d), F32), jax.ShapeDtypeStruct((TOP_K, t_rows, d), BF16),
                   jax.ShapeDtypeStruct((t_rows, LANES), F32), jax.ShapeDtypeStruct((SUBLANES, LANES), F32)],
        scratch_shapes=[pltpu.VMEM((1, LANES), F32)],
        compiler_params=_params("arbitrary"),
        name="mixer_output",
    )(x, modtab, att, yf, yb, z, pool_in, pool_in, pool_in, ssdg, poolw_bd, pools, wout, g2, rw_p, rb_p)


def _expert_kernel(be_ref, nu_ref, x_ref, wg_ref, bg_ref, wu_ref, bu_ref, wd_ref, bd_ref, o_ref,
                   wg_s, wu_s, wd_s):
    i = pl.program_id(0)

    @pl.when(jnp.logical_or(i == 0, be_ref[i] != be_ref[jnp.maximum(i - 1, 0)]))
    def _():
        wg_s[...] = wg_ref[0, 0].astype(BF16)
        wu_s[...] = wu_ref[0, 0].astype(BF16)
        wd_s[...] = wd_ref[0, 0].astype(BF16)

    @pl.when(i < nu_ref[0])
    def _():
        x = x_ref[...]
        g = jnp.minimum(jnp.dot(x, wg_s[...], preferred_element_type=F32) + bg_ref[0, 0], SWIGLU_LIMIT)
        u = jnp.clip(jnp.dot(x, wu_s[...], preferred_element_type=F32) + bu_ref[0, 0],
                     -SWIGLU_LIMIT, SWIGLU_LIMIT)
        a = g / (1.0 + jnp.exp(-SWIGLU_ALPHA * g)) * (u + 1.0)
        y = jnp.dot(a.astype(BF16), wd_s[...], preferred_element_type=F32) + bd_ref[0, 0]
        o_ref[...] = y.astype(o_ref.dtype)

    @pl.when(i >= nu_ref[0])
    def _():
        o_ref[...] = jnp.zeros(o_ref.shape, o_ref.dtype)


def _experts(layer, x_sorted, block_expert, n_used, wg, bg, wu, bu, wd, bd):
    n_rows, d = x_sorted.shape
    depth, ne, _, dff = wg.shape
    rowspec = lambda cols: pl.BlockSpec((MOE_TILE, cols), lambda i, be, nu: (i, 0))
    wspec = lambda a, b: pl.BlockSpec((1, 1, a, b), lambda i, be, nu: (layer, be[i], 0, 0))
    grid_spec = pltpu.PrefetchScalarGridSpec(
        num_scalar_prefetch=2,
        grid=(n_rows // MOE_TILE,),
        in_specs=[rowspec(d), wspec(d, dff), wspec(1, dff), wspec(d, dff), wspec(1, dff),
                  wspec(dff, d), wspec(1, d)],
        out_specs=rowspec(d),
        scratch_shapes=[pltpu.VMEM((d, dff), BF16), pltpu.VMEM((d, dff), BF16), pltpu.VMEM((dff, d), BF16)],
    )
    return pl.pallas_call(
        _expert_kernel,
        grid_spec=grid_spec,
        out_shape=jax.ShapeDtypeStruct((n_rows, d), BF16),
        compiler_params=_params("arbitrary"),
        name="moe_experts",
    )(block_expert, n_used, x_sorted, wg, bg.reshape(depth, ne, 1, dff), wu, bu.reshape(depth, ne, 1, dff),
      wd, bd.reshape(depth, ne, 1, d))


def _dispatch_plan(route, counts_f):
    t_rows = route.shape[0]
    n_assign = t_rows * TOP_K
    ids = route[:, 0:TOP_K].astype(jnp.int32)
    weights = route[:, TOP_K:2 * TOP_K]
    rank = route[:, 2 * TOP_K:3 * TOP_K].astype(jnp.int32)
    counts = counts_f[0, :N_EXPERTS].astype(jnp.int32)
    padded = (counts + MOE_TILE - 1) // MOE_TILE * MOE_TILE
    pad_end = jnp.cumsum(padded)
    pad_start = pad_end - padded
    experts = jnp.arange(N_EXPERTS, dtype=jnp.int32)
    first_row = jnp.sum(jnp.where(ids[:, :, None] == experts[None, None, :], pad_start[None, None, :], 0), axis=-1)
    dest = (first_row + rank).T.reshape(-1)
    n_rows = n_assign + N_EXPERTS * MOE_TILE
    n_blocks = n_rows // MOE_TILE
    block_expert = jnp.minimum(
        jnp.sum((jnp.arange(n_blocks, dtype=jnp.int32)[:, None] * MOE_TILE >= pad_end[None, :]).astype(jnp.int32),
                axis=1), N_EXPERTS - 1).astype(jnp.int32)
    row_src = jnp.zeros((n_rows,), jnp.int32).at[dest].set(
        jnp.arange(n_assign, dtype=jnp.int32), unique_indices=True, mode='promise_in_bounds')
    n_used = (pad_end[-1:] // MOE_TILE).astype(jnp.int32)
    return weights, dest, row_src, block_expert, n_used


def _rope_tables(n_lat, n_ctx):
    rows = n_lat // GRID_W
    row = jnp.repeat(jnp.arange(rows), GRID_W).astype(F32)
    col = jnp.tile(jnp.arange(GRID_W), rows).astype(F32)
    n_freq = QK_ROPE // 4
    inv_freq = ROPE_THETA ** (-jnp.arange(n_freq, dtype=F32) / n_freq)
    ang = jnp.stack([row[:, None] * inv_freq, col[:, None] * inv_freq], axis=1)
    cos, sin = jnp.cos(ang), jnp.sin(ang)
    zero = jnp.zeros_like(sin)
    ones = lambda w: jnp.ones((n_lat, w), F32)
    zeros = lambda w: jnp.zeros((n_lat, w), F32)
    per_axis = lambda a, b: jnp.stack([a, b], axis=2).reshape(n_lat, QK_ROPE)
    tail = HEAD_PAD - QK_DIM
    c = jnp.concatenate([ones(QK_NOPE), per_axis(cos, cos), ones(tail)], axis=1)
    sa = jnp.concatenate([zeros(QK_NOPE), per_axis(zero, sin), zeros(tail)], axis=1)
    sb = jnp.concatenate([zeros(QK_NOPE), per_axis(-sin, zero), zeros(tail)], axis=1)
    pad = lambda a, v: jnp.concatenate([a, jnp.full((n_ctx, HEAD_PAD), v, F32)], axis=0)
    return pad(c, 1.0), pad(sa, 0.0), pad(sb, 0.0)


def _pad_cols(a, width):
    return jnp.pad(a, ((0, 0), (0, width - a.shape[1])))


def _pack_layer(i, w_in, q_g, kv_g, w_uq, w_ukv, qn_g, kn_g, conv_w, conv_b, a_log, dt_bias, d_skip, ssd_g,
                pool_w, pool_scale, w_out, router_w, router_b):
    d = w_in.shape[1]
    pts = [0, Q_LORA, KV_LORA, QK_ROPE, SSD_WIDTH, SSD_CONV_DIM, 2 * SSD_HEADS, POOL_WIDTH]
    offs = [sum(pts[:j + 1]) for j in range(len(pts))]
    seg = lambda j: w_in[i][:, offs[j]:offs[j + 1]]
    zc = lambda w: jnp.zeros((d, w), F32)
    w_in_p = jnp.concatenate([
        seg(0), seg(1), zc(QK_NOPE), seg(2), zc(LANES - QK_DIM), seg(3), seg(4),
        seg(5), zc(LANES - 2 * SSD_HEADS), seg(6)], axis=1).astype(BF16)
    wuq = w_uq[i].reshape(Q_LORA, MLA_HEADS, QK_DIM)
    wuq_p = jnp.pad(wuq, ((0, 0), (0, 0), (0, HEAD_PAD - QK_DIM))).reshape(Q_LORA, MLA_HEADS * HEAD_PAD)
    wukv = w_ukv[i].reshape(KV_LORA, MLA_HEADS, QK_NOPE + V_HEAD)
    wuk_p = jnp.pad(wukv[:, :, :QK_NOPE], ((0, 0), (0, 0), (0, HEAD_PAD - QK_NOPE))).reshape(
        KV_LORA, MLA_HEADS * HEAD_PAD)
    wuv = jnp.pad(wukv[:, :, QK_NOPE:], ((0, 0), (0, 0), (0, HEAD_PAD - V_HEAD))).reshape(
        KV_LORA, MLA_HEADS * HEAD_PAD)
    flat12 = lambda a: a[i].reshape(1, 2 * SSD_HEADS)
    poolw_bd = jax.scipy.linalg.block_diag(*[pool_w[i][g] for g in range(len(POOL_WINDOWS))])
    return dict(
        w_in_p=w_in_p, qg=q_g[i][None], kvg=kv_g[i][None],
        wuq_p=wuq_p.astype(BF16), wuk_p=wuk_p.astype(BF16), wuv=wuv.astype(BF16),
        qng_p=_pad_cols(qn_g[i][None], HEAD_PAD), kng_p=_pad_cols(kn_g[i][None], HEAD_PAD),
        conv_w=conv_w[i], conv_b=conv_b[i][None],
        bias_row=_pad_cols(flat12(dt_bias), LANES), bias_col=_pad_cols(flat12(dt_bias), 2 * SUBLANES).T,
        alog_row=_pad_cols(flat12(a_log), LANES), alog_col=_pad_cols(flat12(a_log), 2 * SUBLANES).T,
        dskip_row=jnp.repeat(d_skip[i], SSD_HEAD_DIM)[None],
        ssdg=ssd_g[i][None], poolw_bd=poolw_bd.astype(BF16), pools=pool_scale[i][None],
        wout=w_out[i].astype(BF16), rw_p=_pad_cols(router_w[i], LANES), rb_p=_pad_cols(router_b[i][None], LANES))


def kernel(x, c, ctx, c_ctx, norm1_g, norm2_g, mod_w, mod_b, w_in, mla_q_norm_g, mla_kv_norm_g, mla_w_uq, mla_w_ukv, mla_qn_g, mla_kn_g, ssd_conv_w, ssd_conv_b, ssd_a_log, ssd_dt_bias, ssd_d, ssd_norm_g, pool_w, pool_scale, w_out, router_w, router_b, moe_w_gate, moe_b_gate, moe_w_up, moe_b_up, moe_w_down, moe_b_down):
    batch, n_lat, d = x.shape
    n_ctx = ctx.shape[1]
    depth = mod_w.shape[0]
    assert batch == 1 and d == D_MODEL and n_ctx == SSD_CHUNK == ROW_TILE
    assert n_lat % ATT_TQ == 0 and n_lat % ATT_TK == 0 and n_lat % GRID_W == 0
    n_lat_tiles = n_lat // ROW_TILE

    cvec = jnp.zeros((SUBLANES, d), F32).at[0].set(c[0]).at[1].set(c_ctx)
    mods = _modulation(cvec, mod_w, mod_b)
    rope_c, rope_sa, rope_sb = _rope_tables(n_lat, n_ctx)

    xs = jnp.concatenate([x[0], ctx[0]], axis=0)
    for i in range(depth):
        lp = _pack_layer(i, w_in, mla_q_norm_g, mla_kv_norm_g, mla_w_uq, mla_w_ukv, mla_qn_g, mla_kn_g,
                         ssd_conv_w, ssd_conv_b, ssd_a_log, ssd_dt_bias, ssd_d, ssd_norm_g, pool_w, pool_scale,
                         w_out, router_w, router_b)
        modtab = jnp.pad(mods[i, :2].reshape(2, 6, d), ((0, 0), (0, SUBLANES - 6), (0, 0)))
        q, k, v, z, xbc, dt, pool_in = _in_projection(
            xs, modtab, norm1_g[i][None], lp['w_in_p'], lp['qg'], lp['kvg'], lp['wuq_p'], lp['wuk_p'], lp['wuv'],
            lp['qng_p'], lp['kng_p'], rope_c, rope_sa, rope_sb, n_lat_tiles)
        att = _attention(q, k, v, n_lat, n_ctx)
        xc = _conv(xbc, lp['conv_w'], lp['conv_b'], n_lat_tiles)
        dtt = dt[:, :2 * SUBLANES].T
        yf, yb = _ssd(xc, dt, dtt, lp['bias_row'], lp['bias_col'], lp['alog_row'], lp['alog_col'],
                      lp['dskip_row'], n_lat // SSD_CHUNK)
        x1, h2_slots, route, counts = _mixer_output(
            xs, modtab, att, yf, yb, z, pool_in, lp['ssdg'], lp['poolw_bd'], lp['pools'], lp['wout'],
            norm2_g[i][None], lp['rw_p'], lp['rb_p'], n_lat_tiles, n_lat, n_ctx)
        top_w, dest, row_src, block_expert, n_used = _dispatch_plan(route, counts)
        x_sorted = h2_slots.reshape(-1, d).at[row_src].get(mode='promise_in_bounds')
        out = _experts(i, x_sorted, block_expert, n_used,
                       moe_w_gate, moe_b_gate, moe_w_up, moe_b_up, moe_w_down, moe_b_down)
        picked = out.at[dest].get(mode='promise_in_bounds', unique_indices=True)
        y = jnp.sum(picked.reshape(TOP_K, xs.shape[0], d).astype(F32) * top_w.T[:, :, None], axis=0)
        gate2 = jnp.where(jnp.arange(xs.shape[0])[:, None] < n_lat, modtab[0, 5][None], modtab[1, 5][None])
        xs = x1 + gate2 * y
    return xs[:n_lat][None]
```

```python
import functools

import jax
import jax.numpy as jnp
from jax import lax
from jax.experimental import pallas as pl
from jax.experimental.pallas import tpu as pltpu

F32 = jnp.float32
BF16 = jnp.bfloat16
HIGHEST = lax.Precision.HIGHEST

D_MODEL = 1024
GRID_W = 64
EPS = 1e-6
MLA_HEADS = 6
QK_NOPE = 64
QK_ROPE = 32
QK_DIM = QK_NOPE + QK_ROPE
V_HEAD = 64
Q_LORA = 256
KV_LORA = 128
MLA_WIDTH = MLA_HEADS * V_HEAD
ROPE_THETA = 10000.0
SSD_HEADS = 6
SSD_HEAD_DIM = 64
SSD_WIDTH = SSD_HEADS * SSD_HEAD_DIM
SSD_GROUPS = 2
SSD_STATE = 64
SSD_CONV = 4
SSD_CONV_DIM = SSD_WIDTH + 2 * SSD_GROUPS * SSD_STATE
POOL_WINDOWS = (2, 4, 8, 16)
POOL_GROUP = 64
POOL_WIDTH = len(POOL_WINDOWS) * POOL_GROUP
N_EXPERTS = 32
TOP_K = 4
D_FF = 1024
SWIGLU_LIMIT = 7.0
SWIGLU_ALPHA = 1.702

LANES = 128
SUBLANES = 8
HEAD_PAD = LANES
Q_SCALE = QK_DIM ** -0.5 * 1.4426950408889634

ROW_TILE = 256
SSD_CHUNK = 256
HALO = SUBLANES
ATT_TQ = 1024
ATT_TK = 512
MOE_TILE = 256
VMEM_LIMIT = 56 * 1024 * 1024

COL_QLAT = 0
COL_KVLAT = COL_QLAT + Q_LORA
COL_KROPE = COL_KVLAT + KV_LORA
COL_Z = COL_KROPE + LANES
COL_XBC = COL_Z + SSD_WIDTH
COL_DT = COL_XBC + SSD_CONV_DIM
COL_POOL = COL_DT + LANES
IN_COLS_PACKED = COL_POOL + POOL_WIDTH


def _rms(x):
    return x * lax.rsqrt(jnp.mean(x * x, axis=-1, keepdims=True) + EPS)


def _params(*sem):
    return pltpu.CompilerParams(dimension_semantics=sem, vmem_limit_bytes=VMEM_LIMIT)


def _mod_kernel(c_ref, w_ref, b_ref, o_ref):
    c = c_ref[...]
    s = c / (1.0 + jnp.exp(-c))
    o_ref[0] = jnp.dot(s, w_ref[0], precision=HIGHEST, preferred_element_type=F32) + b_ref[0]


def _modulation(cvec, mod_w, mod_b):
    depth, d, cols = mod_w.shape
    tn = 1536
    return pl.pallas_call(
        _mod_kernel,
        grid=(depth, cols // tn),
        in_specs=[pl.BlockSpec((SUBLANES, d), lambda l, j: (0, 0)),
                  pl.BlockSpec((1, d, tn), lambda l, j: (l, 0, j)),
                  pl.BlockSpec((1, 1, tn), lambda l, j: (l, 0, j))],
        out_specs=pl.BlockSpec((1, SUBLANES, tn), lambda l, j: (l, 0, j)),
        out_shape=jax.ShapeDtypeStruct((depth, SUBLANES, cols), F32),
        compiler_params=_params("arbitrary", "arbitrary"),
        name="modulation",
    )(cvec, mod_w, mod_b.reshape(depth, 1, cols))


def _moe_combine(x1_ref, picked_ref, route_ref, pmod_ref):
    route = route_ref[...]
    y = picked_ref[0].astype(F32) * route[:, TOP_K:TOP_K + 1]
    for kk in range(1, TOP_K):
        y = y + picked_ref[kk].astype(F32) * route[:, TOP_K + kk:TOP_K + kk + 1]
    return x1_ref[...] + pmod_ref[0][5:6] * y


def _combine_kernel(x1_ref, picked_ref, route_ref, pmod_ref, o_ref):
    o_ref[...] = _moe_combine(x1_ref, picked_ref, route_ref, pmod_ref)


def _final_combine(x1, picked, route, pmodtab, n_lat):
    d = x1.shape[1]
    row = lambda cols: pl.BlockSpec((ROW_TILE, cols), lambda i: (i, 0))
    return pl.pallas_call(
        _combine_kernel,
        grid=(n_lat // ROW_TILE,),
        in_specs=[row(d), pl.BlockSpec((TOP_K, ROW_TILE, d), lambda i: (0, i, 0)), row(LANES),
                  pl.BlockSpec((1, SUBLANES, d), lambda i: (0, 0, 0))],
        out_specs=row(d),
        out_shape=jax.ShapeDtypeStruct((n_lat, d), F32),
        compiler_params=_params("arbitrary"),
        name="moe_combine",
    )(x1, picked, route, pmodtab)


def _inproj_kernel(*refs, has_moe):
    if has_moe:
        x1_ref, picked_ref, route_ref, pmod_ref = refs[:4]
        refs = refs[4:]
    else:
        x_ref = refs[0]
        refs = refs[1:]
    (mod_ref, g1_ref, win_ref, qg_ref, kvg_ref, wuq_ref, wuk_ref, wuv_ref, qng_ref, kng_ref,
     rc_ref, rsa_ref, rsb_ref) = refs[:13]
    outs = refs[13:]
    if has_moe:
        xs_ref, outs = outs[0], outs[1:]
        x = _moe_combine(x1_ref, picked_ref, route_ref, pmod_ref)
        xs_ref[...] = x
    else:
        x = x_ref[...]
    q_ref, k_ref, v_ref, z_ref, xbc_ref, dt_ref, pool_ref = outs
    mod = mod_ref[0]
    shift, scale = mod[0:1], mod[1:2]
    h = _rms(x) * g1_ref[...] * (1.0 + scale) + shift
    proj = jnp.dot(h.astype(BF16), win_ref[...], preferred_element_type=F32)
    z_ref[...] = proj[:, COL_Z:COL_XBC]
    xbc_ref[...] = proj[:, COL_XBC:COL_DT]
    dt_ref[...] = proj[:, COL_DT:COL_POOL]
    pool_ref[...] = proj[:, COL_POOL:IN_COLS_PACKED]

    rc, rsa, rsb = rc_ref[...], rsa_ref[...], rsb_ref[...]

    def head_norm_rope(t, gain):
        ms = jnp.sum(t * t, axis=-1, keepdims=True) * (1.0 / QK_DIM)
        t = t * lax.rsqrt(ms + EPS) * gain
        return (t * rc + pltpu.roll(t, QK_ROPE // 4, axis=1) * rsa
                + pltpu.roll(t, HEAD_PAD - QK_ROPE // 4, axis=1) * rsb)

    qn = (_rms(proj[:, COL_QLAT:COL_KVLAT]) * qg_ref[...]).astype(BF16)
    q_all = jnp.dot(qn, wuq_ref[...], preferred_element_type=F32)
    kvn = (_rms(proj[:, COL_KVLAT:COL_KROPE]) * kvg_ref[...]).astype(BF16)
    k_all = jnp.dot(kvn, wuk_ref[...], preferred_element_type=F32)
    k_rope = proj[:, COL_KROPE:COL_Z]
    v_all = jnp.dot(kvn, wuv_ref[...], preferred_element_type=F32)
    vlane = lax.broadcasted_iota(jnp.int32, v_all.shape, 1) % HEAD_PAD
    v_ref[...] = jnp.where(vlane == V_HEAD, 1.0, v_all).astype(BF16)
    qng, kng = qng_ref[...], kng_ref[...]
    for hd in range(MLA_HEADS):
        sl = slice(hd * HEAD_PAD, (hd + 1) * HEAD_PAD)
        q_ref[:, sl] = (head_norm_rope(q_all[:, sl], qng) * Q_SCALE).astype(BF16)
        k_ref[:, sl] = head_norm_rope(k_all[:, sl] + k_rope, kng).astype(BF16)


def _in_projection(x, moe, modtab, g1, w_in_p, qg, kvg, wuq_p, wuk_p, wuv, qng_p, kng_p, rope_c, rope_sa, rope_sb,
                   n_lat_tiles):
    t_rows, d = x.shape
    nt = t_rows // ROW_TILE
    row = lambda cols: pl.BlockSpec((ROW_TILE, cols), lambda i: (i, 0))
    full = lambda a: pl.BlockSpec(a.shape, lambda i: (0,) * a.ndim)
    seg = lambda: pl.BlockSpec((1, SUBLANES, d), lambda i: (i // n_lat_tiles, 0, 0))
    out_cols = (MLA_HEADS * HEAD_PAD,) * 3 + (SSD_WIDTH, SSD_CONV_DIM, LANES, POOL_WIDTH)
    out_dt = (BF16, BF16, BF16, F32, F32, F32, F32)
    lead_in, lead_specs = [x], [row(d)]
    if moe is not None:
        picked, route, pmodtab = moe
        lead_in += [picked, route, pmodtab]
        lead_specs += [pl.BlockSpec((TOP_K, ROW_TILE, d), lambda i: (0, i, 0)), row(LANES), seg()]
        out_cols = (d,) + out_cols
        out_dt = (F32,) + out_dt
    return pl.pallas_call(
        functools.partial(_inproj_kernel, has_moe=moe is not None),
        grid=(nt,),
        in_specs=lead_specs + [seg(), full(g1), full(w_in_p), full(qg), full(kvg), full(wuq_p), full(wuk_p),
                               full(wuv), full(qng_p), full(kng_p), row(HEAD_PAD), row(HEAD_PAD), row(HEAD_PAD)],
        out_specs=[row(c) for c in out_cols],
        out_shape=[jax.ShapeDtypeStruct((t_rows, c), dt) for c, dt in zip(out_cols, out_dt)],
        compiler_params=_params("arbitrary"),
        name="in_projection",
    )(*lead_in, modtab, g1, w_in_p, qg, kvg, wuq_p, wuk_p, wuv, qng_p, kng_p, rope_c, rope_sa, rope_sb)


def _attention_kernel(q_ref, k_ref, v_ref, o_ref, sa_ref, sb_ref, st_ref, m_ref, acc_ref, *, n_main, tk, tail):
    tq = q_ref.shape[0]
    m_ref[...] = jnp.full(m_ref.shape, -jnp.inf, F32)
    acc_ref[...] = jnp.zeros(acc_ref.shape, F32)

    def scores(s_ref, start, size):
        for hh in range(2):
            q = q_ref[:, hh * HEAD_PAD:(hh + 1) * HEAD_PAD]
            k = k_ref[pl.ds(start, size), hh * HEAD_PAD:(hh + 1) * HEAD_PAD]
            s_ref[hh] = lax.dot_general(q, k, (((1,), (1,)), ((), ())), preferred_element_type=F32)

    def consume(s_ref, start, size):
        for hh in range(2):
            s = s_ref[hh]
            m_old = m_ref[hh]
            m_new = jnp.maximum(m_old, jnp.max(s, axis=-1, keepdims=True))
            p = jnp.exp2((s - m_new).astype(BF16))
            v = v_ref[pl.ds(start, size), hh * HEAD_PAD:(hh + 1) * HEAD_PAD]
            acc_ref[hh] = jnp.exp2(m_old - m_new) * acc_ref[hh] + jnp.dot(p, v, preferred_element_type=F32)
            m_ref[hh] = m_new

    chunk = lambda c: pl.multiple_of(c * tk, tk)
    t_start, t_size = tail
    scores(st_ref, t_start, t_size)
    if n_main > 0:
        scores(sa_ref, 0, tk)
    consume(st_ref, t_start, t_size)
    if n_main > 0:
        def body(j, carry):
            scores(sb_ref, chunk(2 * j + 1), tk)
            consume(sa_ref, chunk(2 * j), tk)
            scores(sa_ref, chunk(2 * j + 2), tk)
            consume(sb_ref, chunk(2 * j + 1), tk)
            return carry
        lax.fori_loop(0, n_main // 2 - 1, body, 0)
        scores(sb_ref, (n_main - 1) * tk, tk)
        consume(sa_ref, (n_main - 2) * tk, tk)
        consume(sb_ref, (n_main - 1) * tk, tk)
    outs = []
    for hh in range(2):
        acc = acc_ref[hh]
        outs.append(acc / acc[:, V_HEAD:V_HEAD + 1])
    lane = lax.broadcasted_iota(jnp.int32, (tq, HEAD_PAD), 1)
    o_ref[...] = jnp.where(lane < V_HEAD, outs[0], pltpu.roll(outs[1], V_HEAD, axis=1)).astype(o_ref.dtype)


def _attention(q, k, v, n_lat, n_ctx):
    t_rows = q.shape[0]
    pairs = MLA_HEADS // 2
    n_main = n_lat // ATT_TK
    assert n_main % 2 == 0 and n_main >= 2

    def scratch(tq, tk, t_size):
        return [pltpu.VMEM((2, tq, tk), F32), pltpu.VMEM((2, tq, tk), F32), pltpu.VMEM((2, tq, t_size), F32),
                pltpu.VMEM((2, tq, 1), F32), pltpu.VMEM((2, tq, HEAD_PAD), F32)]

    resident = lambda: pl.BlockSpec((t_rows, 2 * HEAD_PAD), lambda p, i: (0, p), pipeline_mode=pl.Buffered(1))
    lat = pl.pallas_call(
        functools.partial(_attention_kernel, n_main=n_main, tk=ATT_TK, tail=(n_lat, n_ctx)),
        grid=(pairs, n_lat // ATT_TQ),
        in_specs=[pl.BlockSpec((ATT_TQ, 2 * HEAD_PAD), lambda p, i: (i, p)), resident(), resident()],
        out_specs=pl.BlockSpec((ATT_TQ, 2 * V_HEAD), lambda p, i: (i, p)),
        out_shape=jax.ShapeDtypeStruct((n_lat, MLA_WIDTH), BF16),
        scratch_shapes=scratch(ATT_TQ, ATT_TK, n_ctx),
        compiler_params=_params("arbitrary", "arbitrary"),
        name="attention_latent",
    )(q, k, v)
    cblk = n_lat // n_ctx
    cspec = lambda: pl.BlockSpec((n_ctx, 2 * HEAD_PAD), lambda p: (cblk, p))
    ctx = pl.pallas_call(
        functools.partial(_attention_kernel, n_main=0, tk=SUBLANES, tail=(0, n_ctx)),
        grid=(pairs,),
        in_specs=[cspec(), cspec(), cspec()],
        out_specs=pl.BlockSpec((n_ctx, 2 * V_HEAD), lambda p: (0, p)),
        out_shape=jax.ShapeDtypeStruct((n_ctx, MLA_WIDTH), BF16),
        scratch_shapes=scratch(n_ctx, SUBLANES, n_ctx),
        compiler_params=_params("arbitrary"),
        name="attention_context",
    )(q, k, v)
    return jnp.concatenate([lat, ctx], axis=0)


def _halo_specs(cols, tile_rows, n_lat_tiles, n_tiles):
    per = tile_rows // HALO
    last = n_tiles * per - 1
    return [pl.BlockSpec((HALO, cols), lambda i: (jnp.maximum(i * per - 1, 0), 0)),
            pl.BlockSpec((tile_rows, cols), lambda i: (i, 0)),
            pl.BlockSpec((HALO, cols), lambda i: (jnp.minimum((i + 1) * per, last), 0))]


def _with_halo(prev_ref, cur_ref, next_ref, i, n_lat_tiles):
    has_prev = jnp.logical_and(i != 0, i != n_lat_tiles)
    has_next = jnp.logical_and(i != n_lat_tiles - 1, i != pl.num_programs(0) - 1)
    prev = jnp.where(has_prev, prev_ref[...], 0.0)
    nxt = jnp.where(has_next, next_ref[...], 0.0)
    return jnp.concatenate([prev, cur_ref[...], nxt], axis=0)


def _conv_kernel(prev_ref, cur_ref, next_ref, w_ref, b_ref, o_ref, *, n_lat_tiles):
    i = pl.program_id(0)
    ext = _with_halo(prev_ref, cur_ref, next_ref, i, n_lat_tiles)
    rows = ext.shape[0]
    w = w_ref[...]
    y = (pltpu.roll(ext, 2, axis=0) * w[0:1] + pltpu.roll(ext, 1, axis=0) * w[1:2]
         + ext * w[2:3] + pltpu.roll(ext, rows - 1, axis=0) * w[3:4])
    y = y[HALO:rows - HALO] + b_ref[...]
    o_ref[...] = y / (1.0 + jnp.exp(-y))


def _conv(xbc, conv_w, conv_b, n_lat_tiles):
    t_rows, cols = xbc.shape
    nt = t_rows // ROW_TILE
    return pl.pallas_call(
        functools.partial(_conv_kernel, n_lat_tiles=n_lat_tiles),
        grid=(nt,),
        in_specs=_halo_specs(cols, ROW_TILE, n_lat_tiles, nt) + [
            pl.BlockSpec(conv_w.shape, lambda i: (0, 0)), pl.BlockSpec(conv_b.shape, lambda i: (0, 0))],
        out_specs=pl.BlockSpec((ROW_TILE, cols), lambda i: (i, 0)),
        out_shape=jax.ShapeDtypeStruct((t_rows, cols), F32),
        compiler_params=_params("arbitrary"),
        name="ssd_conv",
    )(xbc, xbc, xbc, conv_w, conv_b)


def _softplus(v):
    return jnp.maximum(v, 0.0) + jnp.log(1.0 + jnp.exp(-jnp.abs(v)))


def _ssd_kernel(xf_ref, xb_ref, dtf_ref, dtb_ref, dttf_ref, dttb_ref, bias_ref, biast_ref, alog_ref, alogt_ref,
                dskip_ref, yf_ref, yb_ref, state_ref):
    L = xf_ref.shape[0]
    P, N, H = SSD_HEAD_DIM, SSD_STATE, SSD_HEADS

    @pl.when(pl.program_id(0) == 0)
    def _():
        state_ref[...] = jnp.zeros(state_ref.shape, F32)

    r = lax.broadcasted_iota(jnp.int32, (L, L), 0)
    c = lax.broadcasted_iota(jnp.int32, (L, L), 1)
    lower = r >= c
    upper = r <= c
    lower_f = lower.astype(F32)
    upper_f = upper.astype(F32)
    a_row = -jnp.exp(alog_ref[...])
    a_col = -jnp.exp(alogt_ref[...])

    def one_direction(x_ref, dt_ref, dtt_ref, y_ref, base, forward):
        x = x_ref[...]
        dt = _softplus(dt_ref[...] + bias_ref[...])
        dtt = _softplus(dtt_ref[...] + biast_ref[...])
        tri_col = lower_f if forward else upper_f
        tri_row = upper_f if forward else lower_f
        cs = jnp.dot(tri_col, dt * a_row, precision=HIGHEST, preferred_element_type=F32)
        cst = jnp.dot(dtt * a_col, tri_row, precision=HIGHEST, preferred_element_type=F32)
        mask = lower if forward else upper
        end = L - 1 if forward else 0
        outs = []
        cb = []
        for g in range(SSD_GROUPS):
            bg = x[:, SSD_WIDTH + g * N:SSD_WIDTH + (g + 1) * N].astype(BF16)
            cg = x[:, SSD_WIDTH + (SSD_GROUPS + g) * N:SSD_WIDTH + (SSD_GROUPS + g + 1) * N].astype(BF16)
            cb.append((bg, cg, lax.dot_general(cg, bg, (((1,), (1,)), ((), ())), preferred_element_type=F32)))
        for hd in range(H):
            j = base + hd
            bg, cg, cbg = cb[hd // (H // SSD_GROUPS)]
            col = cs[:, j:j + 1]
            rowv = cst[j:j + 1, :]
            total = cs[end:end + 1, j:j + 1]
            decay = jnp.where(mask, jnp.exp(jnp.minimum(col - rowv, 0.0)), 0.0)
            xh = (x[:, hd * P:(hd + 1) * P] * dt[:, j:j + 1]).astype(BF16)
            y = jnp.dot((cbg * decay).astype(BF16), xh, preferred_element_type=F32)
            st = state_ref[j]
            y = y + jnp.dot(cg, st.astype(BF16), preferred_element_type=F32) * jnp.exp(col)
            bw = (bg.astype(F32) * jnp.exp(total - col)).astype(BF16)
            state_ref[j] = st * jnp.exp(total) + lax.dot_general(
                bw, xh, (((0,), (0,)), ((), ())), preferred_element_type=F32)
            outs.append(y)
        y_all = jnp.concatenate(outs, axis=1)
        if forward:
            y_all = y_all + x[:, :SSD_WIDTH] * dskip_ref[...]
        y_ref[...] = y_all

    one_direction(xf_ref, dtf_ref, dttf_ref, yf_ref, 0, True)
    one_direction(xb_ref, dtb_ref, dttb_ref, yb_ref, H, False)


def _ssd(xc, dt, dtt, bias_row, bias_col, alog_row, alog_col, dskip_row, n_lat_chunks):
    t_rows, cols = xc.shape
    L = SSD_CHUNK
    nc = t_rows // L
    fwd = lambda j: jnp.where(j == 0, n_lat_chunks, j - 1)
    bwd = lambda j: jnp.where(j == 0, n_lat_chunks, n_lat_chunks - j)
    small = lambda a: pl.BlockSpec(a.shape, lambda j: (0,) * a.ndim)
    return pl.pallas_call(
        _ssd_kernel,
        grid=(nc,),
        in_specs=[pl.BlockSpec((L, cols), lambda j: (fwd(j), 0)),
                  pl.BlockSpec((L, cols), lambda j: (bwd(j), 0)),
                  pl.BlockSpec((L, LANES), lambda j: (fwd(j), 0)),
                  pl.BlockSpec((L, LANES), lambda j: (bwd(j), 0)),
                  pl.BlockSpec((2 * SUBLANES, L), lambda j: (0, fwd(j))),
                  pl.BlockSpec((2 * SUBLANES, L), lambda j: (0, bwd(j))),
                  small(bias_row), small(bias_col), small(alog_row), small(alog_col), small(dskip_row)],
        out_specs=[pl.BlockSpec((L, SSD_WIDTH), lambda j: (fwd(j), 0)),
                   pl.BlockSpec((L, SSD_WIDTH), lambda j: (bwd(j), 0))],
        out_shape=[jax.ShapeDtypeStruct((t_rows, SSD_WIDTH), F32)] * 2,
        scratch_shapes=[pltpu.VMEM((2 * SSD_HEADS, SSD_STATE, SSD_HEAD_DIM), F32)],
        compiler_params=_params("arbitrary"),
        name="ssd_scan",
    )(xc, xc, dt, dt, dtt, dtt, bias_row, bias_col, alog_row, alog_col, dskip_row)


def _mixout_kernel(x_ref, mod_ref, att_ref, yf_ref, yb_ref, z_ref, pprev_ref, pcur_ref, pnext_ref,
                   ssdg_ref, poolw_ref, pools_ref, wout_ref, g2_ref, rw_ref, rb_ref,
                   x1_ref, h2_ref, route_ref, cnt_ref, base_ref, *, n_lat_tiles, n_lat, n_ctx):
    i = pl.program_id(0)
    mod = mod_ref[0]
    z = z_ref[...]
    ssd = _rms((yf_ref[...] + yb_ref[...]) * (z / (1.0 + jnp.exp(-z)))) * ssdg_ref[...]
    ext = _with_halo(pprev_ref, pcur_ref, pnext_ref, i, n_lat_tiles)
    rows = ext.shape[0]
    tm = rows - 2 * HALO
    w2 = ext + pltpu.roll(ext, 1, axis=0)
    w4 = pltpu.roll(w2, 1, axis=0) + pltpu.roll(w2, rows - 1, axis=0)
    w8 = pltpu.roll(w4, 2, axis=0) + pltpu.roll(w4, rows - 2, axis=0)
    w16 = pltpu.roll(w8, 4, axis=0) + pltpu.roll(w8, rows - 4, axis=0)
    lane = lax.broadcasted_iota(jnp.int32, (tm, POOL_WIDTH), 1)
    grp = lane // POOL_GROUP
    sl = slice(HALO, rows - HALO)
    wsum = jnp.where(grp == 0, w2[sl], jnp.where(grp == 1, w4[sl], jnp.where(grp == 2, w8[sl], w16[sl])))
    is_ctx = i >= n_lat_tiles
    seg_len = jnp.where(is_ctx, n_ctx, n_lat)
    t = lax.broadcasted_iota(jnp.int32, (tm, POOL_WIDTH), 0) + jnp.where(is_ctx, i - n_lat_tiles, i) * tm
    half = jnp.left_shift(1, grp)
    lo = jnp.clip(t - half, 0, seg_len)
    hi = jnp.clip(t + half, 0, seg_len)
    p = wsum / (hi - lo).astype(F32) - pcur_ref[...]
    pool = jnp.dot(p.astype(BF16), poolw_ref[...], preferred_element_type=F32) * pools_ref[...]
    mix = (jnp.dot(att_ref[...], wout_ref[0:MLA_WIDTH], preferred_element_type=F32)
           + jnp.dot(ssd.astype(BF16), wout_ref[MLA_WIDTH:MLA_WIDTH + SSD_WIDTH], preferred_element_type=F32)
           + jnp.dot(pool.astype(BF16), wout_ref[MLA_WIDTH + SSD_WIDTH:], preferred_element_type=F32))
    x1 = x_ref[...] + mod[2:3] * mix
    x1_ref[...] = x1
    h2 = _rms(x1) * g2_ref[...] * (1.0 + mod[4:5]) + mod[3:4]
    h2b = h2.astype(BF16)
    for kk in range(TOP_K):
        h2_ref[kk] = h2b
    logits = jnp.dot(h2, rw_ref[...], precision=HIGHEST, preferred_element_type=F32) + rb_ref[...]

    @pl.when(i == 0)
    def _():
        base_ref[...] = jnp.zeros(base_ref.shape, F32)

    elane = lax.broadcasted_iota(jnp.int32, logits.shape, 1)
    lg = jnp.where(elane < N_EXPERTS, logits, -jnp.inf)
    rr = lax.broadcasted_iota(jnp.int32, (tm, tm), 0)
    cc = lax.broadcasted_iota(jnp.int32, (tm, tm), 1)
    earlier = (rr > cc).astype(BF16)
    offset = base_ref[...]
    tops, ids, ranks = [], [], []
    for kk in range(TOP_K):
        top = jnp.max(lg, axis=-1, keepdims=True)
        idx = jnp.min(jnp.where(lg == top, elane, LANES), axis=-1, keepdims=True)
        sel = elane == idx
        lg = jnp.where(sel, -jnp.inf, lg)
        onehot = sel.astype(BF16)
        before = jnp.dot(earlier, onehot, preferred_element_type=F32) + offset
        ranks.append(jnp.sum(jnp.where(sel, before, 0.0), axis=-1, keepdims=True))
        offset = offset + jnp.sum(sel.astype(F32), axis=0, keepdims=True)
        tops.append(top)
        ids.append(idx.astype(F32))
    base_ref[...] = offset
    cnt_ref[...] = jnp.broadcast_to(offset, cnt_ref.shape)
    exps = [jnp.exp(tp - tops[0]) for tp in tops]
    denom = exps[0] + exps[1] + exps[2] + exps[3]
    route = jnp.zeros(logits.shape, F32)
    for kk in range(TOP_K):
        route = jnp.where(elane == kk, ids[kk], route)
        route = jnp.where(elane == TOP_K + kk, exps[kk] / denom, route)
        route = jnp.where(elane == 2 * TOP_K + kk, ranks[kk], route)
    route_ref[...] = route


def _mixer_output(x, modtab, att, yf, yb, z, pool_in, ssdg, poolw_bd, pools, wout, g2, rw_p, rb_p,
                  n_lat_tiles, n_lat, n_ctx):
    t_rows, d = x.shape
    nt = t_rows // ROW_TILE
    row = lambda cols: pl.BlockSpec((ROW_TILE, cols), lambda i: (i, 0))
    full = lambda a: pl.BlockSpec(a.shape, lambda i: (0,) * a.ndim)
    return pl.pallas_call(
        functools.partial(_mixout_kernel, n_lat_tiles=n_lat_tiles, n_lat=n_lat, n_ctx=n_ctx),
        grid=(nt,),
        in_specs=[row(d), pl.BlockSpec((1, SUBLANES, d), lambda i: (i // n_lat_tiles, 0, 0)),
                  row(MLA_WIDTH), row(SSD_WIDTH), row(SSD_WIDTH), row(SSD_WIDTH)]
                 + _halo_specs(POOL_WIDTH, ROW_TILE, n_lat_tiles, nt)
                 + [full(ssdg), full(poolw_bd), full(pools), full(wout), full(g2), full(rw_p), full(rb_p)],
        out_specs=[row(d), pl.BlockSpec((TOP_K, ROW_TILE, d), lambda i: (0, i, 0)), row(LANES),
                   pl.BlockSpec((SUBLANES, LANES), lambda i: (0, 0))],
        out_shape=[jax.ShapeDtypeStruct((t_rows, d), F32), jax.ShapeDtypeStruct((TOP_K, t_rows, d), BF16),
                   jax.ShapeDtypeStruct((t_rows, LANES), F32), jax.ShapeDtypeStruct((SUBLANES, LANES), F32)],
        scratch_shapes=[pltpu.VMEM((1, LANES), F32)],
        compiler_params=_params("arbitrary"),
        name="mixer_output",
    )(x, modtab, att, yf, yb, z, pool_in, pool_in, pool_in, ssdg, poolw_bd, pools, wout, g2, rw_p, rb_p)


def _expert_kernel(be_ref, nu_ref, x_ref, wg_ref, bg_ref, wu_ref, bu_ref, wd_ref, bd_ref, o_ref,
                   wg_s, wu_s, wd_s):
    i = pl.program_id(0)

    @pl.when(jnp.logical_or(i == 0, be_ref[i] != be_ref[jnp.maximum(i - 1, 0)]))
    def _():
        wg_s[...] = wg_ref[0, 0].astype(BF16)
        wu_s[...] = wu_ref[0, 0].astype(BF16)
        wd_s[...] = wd_ref[0, 0].astype(BF16)

    @pl.when(i < nu_ref[0])
    def _():
        x = x_ref[...]
        g = jnp.minimum(jnp.dot(x, wg_s[...], preferred_element_type=F32) + bg_ref[0, 0], SWIGLU_LIMIT)
        u = jnp.clip(jnp.dot(x, wu_s[...], preferred_element_type=F32) + bu_ref[0, 0],
                     -SWIGLU_LIMIT, SWIGLU_LIMIT)
        a = g / (1.0 + jnp.exp(-SWIGLU_ALPHA * g)) * (u + 1.0)
        y = jnp.dot(a.astype(BF16), wd_s[...], preferred_element_type=F32) + bd_ref[0, 0]
        o_ref[...] = y.astype(o_ref.dtype)

    @pl.when(i >= nu_ref[0])
    def _():
        o_ref[...] = jnp.zeros(o_ref.shape, o_ref.dtype)


def _experts(layer, x_sorted, block_expert, n_used, wg, bg, wu, bu, wd, bd):
    n_rows, d = x_sorted.shape
    depth, ne, _, dff = wg.shape
    rowspec = lambda cols: pl.BlockSpec((MOE_TILE, cols), lambda i, be, nu: (i, 0))
    wspec = lambda a, b: pl.BlockSpec((1, 1, a, b), lambda i, be, nu: (layer, be[i], 0, 0))
    grid_spec = pltpu.PrefetchScalarGridSpec(
        num_scalar_prefetch=2,
        grid=(n_rows // MOE_TILE,),
        in_specs=[rowspec(d), wspec(d, dff), wspec(1, dff), wspec(d, dff), wspec(1, dff),
                  wspec(dff, d), wspec(1, d)],
        out_specs=rowspec(d),
        scratch_shapes=[pltpu.VMEM((d, dff), BF16), pltpu.VMEM((d, dff), BF16), pltpu.VMEM((dff, d), BF16)],
    )
    return pl.pallas_call(
        _expert_kernel,
        grid_spec=grid_spec,
        out_shape=jax.ShapeDtypeStruct((n_rows, d), BF16),
        compiler_params=_params("arbitrary"),
        name="moe_experts",
    )(block_expert, n_used, x_sorted, wg, bg.reshape(depth, ne, 1, dff), wu, bu.reshape(depth, ne, 1, dff),
      wd, bd.reshape(depth, ne, 1, d))


def _dispatch_plan(route, counts_f):
    t_rows = route.shape[0]
    n_assign = t_rows * TOP_K
    ids = route[:, 0:TOP_K].astype(jnp.int32)
    rank =route[:, 2 * TOP_K:3 * TOP_K].astype(jnp.int32)
    counts = counts_f[0, :N_EXPERTS].astype(jnp.int32)
    padded = (counts + MOE_TILE - 1) // MOE_TILE * MOE_TILE
    pad_end = jnp.cumsum(padded)
    pad_start = pad_end - padded
    experts = jnp.arange(N_EXPERTS, dtype=jnp.int32)
    first_row = jnp.sum(jnp.where(ids[:, :, None] == experts[None, None, :], pad_start[None, None, :], 0), axis=-1)
    dest = (first_row + rank).T.reshape(-1)
    n_rows = n_assign + N_EXPERTS * MOE_TILE
    n_blocks = n_rows // MOE_TILE
    block_expert = jnp.minimum(
        jnp.sum((jnp.arange(n_blocks, dtype=jnp.int32)[:, None] * MOE_TILE >= pad_end[None, :]).astype(jnp.int32),
                axis=1), N_EXPERTS - 1).astype(jnp.int32)
    row_src = (jnp.arange(n_rows, dtype=jnp.int32) % n_assign).at[dest].set(
        jnp.arange(n_assign, dtype=jnp.int32), unique_indices=True, mode='promise_in_bounds')
    n_used = (pad_end[-1:] // MOE_TILE).astype(jnp.int32)
    return dest, row_src, block_expert, n_used


def _rope_tables(n_lat, n_ctx):
    rows = n_lat // GRID_W
    row = jnp.repeat(jnp.arange(rows), GRID_W).astype(F32)
    col = jnp.tile(jnp.arange(GRID_W), rows).astype(F32)
    n_freq = QK_ROPE // 4
    inv_freq = ROPE_THETA ** (-jnp.arange(n_freq, dtype=F32) / n_freq)
    ang = jnp.stack([row[:, None] * inv_freq, col[:, None] * inv_freq], axis=1)
    cos, sin = jnp.cos(ang), jnp.sin(ang)
    zero = jnp.zeros_like(sin)
    ones = lambda w: jnp.ones((n_lat, w), F32)
    zeros = lambda w: jnp.zeros((n_lat, w), F32)
    per_axis = lambda a, b: jnp.stack([a, b], axis=2).reshape(n_lat, QK_ROPE)
    tail = HEAD_PAD - QK_DIM
    c = jnp.concatenate([ones(QK_NOPE), per_axis(cos, cos), ones(tail)], axis=1)
    sa = jnp.concatenate([zeros(QK_NOPE), per_axis(zero, sin), zeros(tail)], axis=1)
    sb = jnp.concatenate([zeros(QK_NOPE), per_axis(-sin, zero), zeros(tail)], axis=1)
    pad = lambda a, v: jnp.concatenate([a, jnp.full((n_ctx, HEAD_PAD), v, F32)], axis=0)
    return pad(c, 1.0), pad(sa, 0.0), pad(sb, 0.0)


def _pad_cols(a, width):
    return jnp.pad(a, ((0, 0), (0, width - a.shape[1])))


def _pack_layer(i, w_in, q_g, kv_g, w_uq, w_ukv, qn_g, kn_g, conv_w, conv_b, a_log, dt_bias, d_skip, ssd_g,
                pool_w, pool_scale, w_out, router_w, router_b):
    d = w_in.shape[1]
    pts = [0, Q_LORA, KV_LORA, QK_ROPE, SSD_WIDTH, SSD_CONV_DIM, 2 * SSD_HEADS, POOL_WIDTH]
    offs = [sum(pts[:j + 1]) for j in range(len(pts))]
    seg = lambda j: w_in[i][:, offs[j]:offs[j + 1]]
    zc = lambda w: jnp.zeros((d, w), F32)
    w_in_p = jnp.concatenate([
        seg(0), seg(1), zc(QK_NOPE), seg(2), zc(LANES - QK_DIM), seg(3), seg(4),
        seg(5), zc(LANES - 2 * SSD_HEADS), seg(6)], axis=1).astype(BF16)
    wuq = w_uq[i].reshape(Q_LORA, MLA_HEADS, QK_DIM)
    wuq_p = jnp.pad(wuq, ((0, 0), (0, 0), (0, HEAD_PAD - QK_DIM))).reshape(Q_LORA, MLA_HEADS * HEAD_PAD)
    wukv = w_ukv[i].reshape(KV_LORA, MLA_HEADS, QK_NOPE + V_HEAD)
    wuk_p = jnp.pad(wukv[:, :, :QK_NOPE], ((0, 0), (0, 0), (0, HEAD_PAD - QK_NOPE))).reshape(
        KV_LORA, MLA_HEADS * HEAD_PAD)
    wuv = jnp.pad(wukv[:, :, QK_NOPE:], ((0, 0), (0, 0), (0, HEAD_PAD - V_HEAD))).reshape(
        KV_LORA, MLA_HEADS * HEAD_PAD)
    flat12 = lambda a: a[i].reshape(1, 2 * SSD_HEADS)
    poolw_bd = jax.scipy.linalg.block_diag(*[pool_w[i][g] for g in range(len(POOL_WINDOWS))])
    return dict(
        w_in_p=w_in_p, qg=q_g[i][None], kvg=kv_g[i][None],
        wuq_p=wuq_p.astype(BF16), wuk_p=wuk_p.astype(BF16), wuv=wuv.astype(BF16),
        qng_p=_pad_cols(qn_g[i][None], HEAD_PAD), kng_p=_pad_cols(kn_g[i][None], HEAD_PAD),
        conv_w=conv_w[i], conv_b=conv_b[i][None],
        bias_row=_pad_cols(flat12(dt_bias), LANES), bias_col=_pad_cols(flat12(dt_bias), 2 * SUBLANES).T,
        alog_row=_pad_cols(flat12(a_log), LANES), alog_col=_pad_cols(flat12(a_log), 2 * SUBLANES).T,
        dskip_row=jnp.repeat(d_skip[i], SSD_HEAD_DIM)[None],
        ssdg=ssd_g[i][None], poolw_bd=poolw_bd.astype(BF16), pools=pool_scale[i][None],
        wout=w_out[i].astype(BF16), rw_p=_pad_cols(router_w[i], LANES), rb_p=_pad_cols(router_b[i][None], LANES))


def kernel(x, c, ctx, c_ctx, norm1_g, norm2_g, mod_w, mod_b, w_in, mla_q_norm_g, mla_kv_norm_g, mla_w_uq, mla_w_ukv, mla_qn_g, mla_kn_g, ssd_conv_w, ssd_conv_b, ssd_a_log, ssd_dt_bias, ssd_d, ssd_norm_g, pool_w, pool_scale, w_out, router_w, router_b, moe_w_gate, moe_b_gate, moe_w_up, moe_b_up, moe_w_down, moe_b_down):
    batch, n_lat, d = x.shape
    n_ctx = ctx.shape[1]
    depth = mod_w.shape[0]
    assert batch == 1 and d == D_MODEL and n_ctx == SSD_CHUNK == ROW_TILE
    assert n_lat % ATT_TQ == 0 and n_lat % ATT_TK == 0 and n_lat % GRID_W == 0
    n_lat_tiles = n_lat // ROW_TILE

    cvec = jnp.zeros((SUBLANES, d), F32).at[0].set(c[0]).at[1].set(c_ctx)
    mods = _modulation(cvec, mod_w, mod_b)
    rope_c, rope_sa, rope_sb = _rope_tables(n_lat, n_ctx)

    xs = jnp.concatenate([x[0], ctx[0]], axis=0)
    moe = None
    for i in range(depth):
        lp = _pack_layer(i, w_in, mla_q_norm_g, mla_kv_norm_g, mla_w_uq, mla_w_ukv, mla_qn_g, mla_kn_g,
                         ssd_conv_w, ssd_conv_b, ssd_a_log, ssd_dt_bias, ssd_d, ssd_norm_g, pool_w, pool_scale,
                         w_out, router_w, router_b)
        modtab = jnp.pad(mods[i, :2].reshape(2, 6, d), ((0, 0), (0, SUBLANES - 6), (0, 0)))
        outs = _in_projection(
            xs, moe, modtab, norm1_g[i][None], lp['w_in_p'], lp['qg'], lp['kvg'], lp['wuq_p'], lp['wuk_p'], lp['wuv'],
            lp['qng_p'], lp['kng_p'], rope_c, rope_sa, rope_sb, n_lat_tiles)
        if moe is not None:
            xs, outs = outs[0], outs[1:]
        q, k, v, z, xbc, dt, pool_in = outs
        att = _attention(q, k, v, n_lat, n_ctx)
        xc = _conv(xbc, lp['conv_w'], lp['conv_b'], n_lat_tiles)
        dtt = dt[:, :2 * SUBLANES].T
        yf, yb = _ssd(xc, dt, dtt, lp['bias_row'], lp['bias_col'], lp['alog_row'], lp['alog_col'],
                      lp['dskip_row'], n_lat // SSD_CHUNK)
        x1, h2_slots, route, counts = _mixer_output(
            xs, modtab, att, yf, yb, z, pool_in, lp['ssdg'], lp['poolw_bd'], lp['pools'], lp['wout'],
            norm2_g[i][None], lp['rw_p'], lp['rb_p'], n_lat_tiles, n_lat, n_ctx)
        dest, row_src, block_expert, n_used = _dispatch_plan(route, counts)
        x_sorted = h2_slots.reshape(-1, d).at[row_src].get(mode='promise_in_bounds')
        out = _experts(i, x_sorted, block_expert, n_used,
                       moe_w_gate, moe_b_gate, moe_w_up, moe_b_up, moe_w_down, moe_b_down)
        picked = out.at[dest].get(mode='promise_in_bounds', unique_indices=True)
        xs, moe = x1, (picked.reshape(TOP_K, xs.shape[0], d), route, modtab)
    return _final_combine(xs, *moe, n_lat)[None]
```

```python
import functools

import jax
import jax.numpy as jnp
from jax import lax
from jax.experimental import pallas as pl
from jax.experimental.pallas import tpu as pltpu

F32 = jnp.float32
BF16 = jnp.bfloat16
HIGHEST = lax.Precision.HIGHEST

D_MODEL = 1024
GRID_W = 64
EPS = 1e-6
MLA_HEADS = 6
QK_NOPE = 64
QK_ROPE = 32
QK_DIM = QK_NOPE + QK_ROPE
V_HEAD = 64
Q_LORA = 256
KV_LORA = 128
MLA_WIDTH = MLA_HEADS * V_HEAD
ROPE_THETA = 10000.0
SSD_HEADS = 6
SSD_HEAD_DIM = 64
SSD_WIDTH = SSD_HEADS * SSD_HEAD_DIM
SSD_GROUPS = 2
SSD_STATE = 64
SSD_CONV = 4
SSD_CONV_DIM = SSD_WIDTH + 2 * SSD_GROUPS * SSD_STATE
POOL_WINDOWS = (2, 4, 8, 16)
POOL_GROUP = 64
POOL_WIDTH = len(POOL_WINDOWS) * POOL_GROUP
N_EXPERTS = 32
TOP_K = 4
D_FF = 1024
SWIGLU_LIMIT = 7.0
SWIGLU_ALPHA = 1.702

LANES = 128
SUBLANES = 8
HEAD_PAD = LANES
Q_SCALE = QK_DIM ** -0.5 * 1.4426950408889634

ROW_TILE = 256
SSD_CHUNK = 256
HALO = SUBLANES
ATT_TQ = 1024
ATT_TK = 512
MOE_TILE = 512
VMEM_LIMIT = 56 * 1024 * 1024

COL_QLAT = 0
COL_KVLAT = COL_QLAT + Q_LORA
COL_KROPE = COL_KVLAT + KV_LORA
COL_Z = COL_KROPE + LANES
COL_XBC = COL_Z + SSD_WIDTH
COL_DT = COL_XBC + SSD_CONV_DIM
COL_POOL = COL_DT + LANES
IN_COLS_PACKED = COL_POOL + POOL_WIDTH


def _rms(x):
    return x * lax.rsqrt(jnp.mean(x * x, axis=-1, keepdims=True) + EPS)


def _params(*sem):
    return pltpu.CompilerParams(dimension_semantics=sem, vmem_limit_bytes=VMEM_LIMIT)


def _mod_kernel(c_ref, w_ref, b_ref, o_ref):
    c = c_ref[...]
    s = c / (1.0 + jnp.exp(-c))
    o_ref[0] = jnp.dot(s, w_ref[0], precision=HIGHEST, preferred_element_type=F32) + b_ref[0]


def _modulation(cvec, mod_w, mod_b):
    depth, d, cols = mod_w.shape
    tn = 1536
    return pl.pallas_call(
        _mod_kernel,
        grid=(depth, cols // tn),
        in_specs=[pl.BlockSpec((SUBLANES, d), lambda l, j: (0, 0)),
                  pl.BlockSpec((1, d, tn), lambda l, j: (l, 0, j)),
                  pl.BlockSpec((1, 1, tn), lambda l, j: (l, 0, j))],
        out_specs=pl.BlockSpec((1, SUBLANES, tn), lambda l, j: (l, 0, j)),
        out_shape=jax.ShapeDtypeStruct((depth, SUBLANES, cols), F32),
        compiler_params=_params("arbitrary", "arbitrary"),
        name="modulation",
    )(cvec, mod_w, mod_b.reshape(depth, 1, cols))


def _moe_combine(x1_ref, picked_ref, route_ref, pmod_ref):
    route = route_ref[...]
    y = picked_ref[0].astype(F32) * route[:, TOP_K:TOP_K + 1]
    for kk in range(1, TOP_K):
        y = y + picked_ref[kk].astype(F32) * route[:, TOP_K + kk:TOP_K + kk + 1]
    return x1_ref[...] + pmod_ref[0][5:6] * y


def _combine_kernel(x1_ref, picked_ref, route_ref, pmod_ref, o_ref):
    o_ref[...] = _moe_combine(x1_ref, picked_ref, route_ref, pmod_ref)


def _final_combine(x1, picked, route, pmodtab, n_lat):
    d = x1.shape[1]
    row = lambda cols: pl.BlockSpec((ROW_TILE, cols), lambda i: (i, 0))
    return pl.pallas_call(
        _combine_kernel,
        grid=(n_lat // ROW_TILE,),
        in_specs=[row(d), pl.BlockSpec((TOP_K, ROW_TILE, d), lambda i: (0, i, 0)), row(LANES),
                  pl.BlockSpec((1, SUBLANES, d), lambda i: (0, 0, 0))],
        out_specs=row(d),
        out_shape=jax.ShapeDtypeStruct((n_lat, d), F32),
        compiler_params=_params("arbitrary"),
        name="moe_combine",
    )(x1, picked, route, pmodtab)


def _inproj_kernel(*refs, has_moe):
    if has_moe:
        x1_ref, picked_ref, route_ref, pmod_ref = refs[:4]
        refs = refs[4:]
    else:
        x_ref = refs[0]
        refs = refs[1:]
    (mod_ref, g1_ref, win_ref, qg_ref, kvg_ref, wuq_ref, wuk_ref, wuv_ref, qng_ref, kng_ref,
     rc_ref, rsa_ref, rsb_ref) = refs[:13]
    outs = refs[13:]
    if has_moe:
        xs_ref, outs = outs[0], outs[1:]
        x = _moe_combine(x1_ref, picked_ref, route_ref, pmod_ref)
        xs_ref[...] = x
    else:
        x = x_ref[...]
    q_ref, k_ref, v_ref, z_ref, xbc_ref, dt_ref, pool_ref = outs
    mod = mod_ref[0]
    shift, scale = mod[0:1], mod[1:2]
    h = _rms(x) * g1_ref[...] * (1.0 + scale) + shift
    proj = jnp.dot(h.astype(BF16), win_ref[...], preferred_element_type=F32)
    z_ref[...] = proj[:, COL_Z:COL_XBC]
    xbc_ref[...] = proj[:, COL_XBC:COL_DT]
    dt_ref[...] = proj[:, COL_DT:COL_POOL]
    pool_ref[...] = proj[:, COL_POOL:IN_COLS_PACKED]

    rc, rsa, rsb = rc_ref[...], rsa_ref[...], rsb_ref[...]

    def head_norm_rope(t, gain):
        ms = jnp.sum(t * t, axis=-1, keepdims=True) * (1.0 / QK_DIM)
        t = t * lax.rsqrt(ms + EPS) * gain
        return (t * rc + pltpu.roll(t, QK_ROPE // 4, axis=1) * rsa
                + pltpu.roll(t, HEAD_PAD - QK_ROPE // 4, axis=1) * rsb)

    qn = (_rms(proj[:, COL_QLAT:COL_KVLAT]) * qg_ref[...]).astype(BF16)
    q_all = jnp.dot(qn, wuq_ref[...], preferred_element_type=F32)
    kvn = (_rms(proj[:, COL_KVLAT:COL_KROPE]) * kvg_ref[...]).astype(BF16)
    k_all = jnp.dot(kvn, wuk_ref[...], preferred_element_type=F32)
    k_rope = proj[:, COL_KROPE:COL_Z]
    vt_all = lax.dot_general(wuv_ref[...], kvn, (((1,), (1,)), ((), ())), preferred_element_type=F32)
    vrow = lax.broadcasted_iota(jnp.int32, vt_all.shape, 0) % HEAD_PAD
    v_ref[0] = jnp.where(vrow == V_HEAD, 1.0, vt_all).astype(BF16)
    qng, kng = qng_ref[...], kng_ref[...]
    for hd in range(MLA_HEADS):
        sl = slice(hd * HEAD_PAD, (hd + 1) * HEAD_PAD)
        q_ref[:, sl] = (head_norm_rope(q_all[:, sl], qng) * Q_SCALE).astype(BF16)
        k_ref[:, sl] = head_norm_rope(k_all[:, sl] + k_rope, kng).astype(BF16)


def _in_projection(x, moe, modtab, g1, w_in_p, qg, kvg, wuq_p, wuk_p, wuv, qng_p, kng_p, rope_c, rope_sa, rope_sb,
                   n_lat_tiles):
    t_rows, d = x.shape
    nt = t_rows // ROW_TILE
    row = lambda cols: pl.BlockSpec((ROW_TILE, cols), lambda i: (i, 0))
    full = lambda a: pl.BlockSpec(a.shape, lambda i: (0,) * a.ndim)
    seg = lambda: pl.BlockSpec((1, SUBLANES, d), lambda i: (i // n_lat_tiles, 0, 0))
    hw = MLA_HEADS * HEAD_PAD
    per = ATT_TK // ROW_TILE
    out_specs = [row(hw), row(hw), pl.BlockSpec((1, hw, ROW_TILE), lambda i: (i // per, 0, i % per)),
                 row(SSD_WIDTH), row(SSD_CONV_DIM), row(LANES), row(POOL_WIDTH)]
    sds = lambda cols, dt: jax.ShapeDtypeStruct((t_rows, cols), dt)
    out_shape = [sds(hw, BF16), sds(hw, BF16), jax.ShapeDtypeStruct((pl.cdiv(nt, per), hw, ATT_TK), BF16),
                 sds(SSD_WIDTH, F32), sds(SSD_CONV_DIM, F32), sds(LANES, F32), sds(POOL_WIDTH, F32)]
    lead_in, lead_specs = [x], [row(d)]
    if moe is not None:
        picked, route, pmodtab = moe
        lead_in += [picked, route, pmodtab]
        lead_specs += [pl.BlockSpec((TOP_K, ROW_TILE, d), lambda i: (0, i, 0)), row(LANES), seg()]
        out_specs = [row(d)] + out_specs
        out_shape = [sds(d, F32)] + out_shape
    return pl.pallas_call(
        functools.partial(_inproj_kernel, has_moe=moe is not None),
        grid=(nt,),
        in_specs=lead_specs + [seg(), full(g1), full(w_in_p), full(qg), full(kvg), full(wuq_p), full(wuk_p),
                               full(wuv), full(qng_p), full(kng_p), row(HEAD_PAD), row(HEAD_PAD), row(HEAD_PAD)],
        out_specs=out_specs,
        out_shape=out_shape,
        compiler_params=_params("arbitrary"),
        name="in_projection",
    )(*lead_in, modtab, g1, w_in_p, qg, kvg, wuq_p, wuk_p, wuv, qng_p, kng_p, rope_c, rope_sa, rope_sb)


def _attention_kernel(q_ref, k_ref, vt_ref, o_ref, sa_ref, sb_ref, st_ref, m_ref, acc_ref, *,
                      n_main, tk, tail):
    m_ref[...] = jnp.full(m_ref.shape, -jnp.inf, F32)
    acc_ref[...] = jnp.zeros(acc_ref.shape, F32)

    def scores(s_ref, start, size):
        for hh in range(2):
            q = q_ref[:, hh * HEAD_PAD:(hh + 1) * HEAD_PAD]
            k = k_ref[pl.ds(start, size), hh * HEAD_PAD:(hh + 1) * HEAD_PAD]
            s_ref[hh] = lax.dot_general(k, q, (((1,), (1,)), ((), ())), preferred_element_type=F32)

    def consume(s_ref, c, size):
        for hh in range(2):
            s = s_ref[hh]
            m_old = m_ref[hh]
            m_new = jnp.maximum(m_old, jnp.max(s, axis=0, keepdims=True))
            p = jnp.exp2((s - m_new).astype(BF16))
            vt = vt_ref[c, hh * HEAD_PAD:(hh + 1) * HEAD_PAD, 0:size]
            acc_ref[hh] = jnp.exp2(m_old - m_new) * acc_ref[hh] + jnp.dot(vt, p, preferred_element_type=F32)
            m_ref[hh] = m_new

    row0 = lambda c: pl.multiple_of(c * tk, tk)
    t_start, t_size, t_chunk = tail
    scores(st_ref, t_start, t_size)
    if n_main > 0:
        scores(sa_ref, 0, tk)
    consume(st_ref, t_chunk, t_size)
    if n_main > 0:
        def body(j, carry):
            scores(sb_ref, row0(2 * j + 1), tk)
            consume(sa_ref, 2 * j, tk)
            scores(sa_ref, row0(2 * j + 2), tk)
            consume(sb_ref, 2 * j + 1, tk)
            return carry
        lax.fori_loop(0, n_main // 2 - 1, body, 0)
        scores(sb_ref, (n_main - 1) * tk, tk)
        consume(sa_ref, n_main - 2, tk)
        consume(sb_ref, n_main - 1, tk)
    outs = []
    for hh in range(2):
        acc = acc_ref[hh]
        outs.append(acc[0:V_HEAD] / acc[V_HEAD:V_HEAD + 1])
    o_ref[...] = jnp.concatenate(outs, axis=0).T.astype(o_ref.dtype)


def _attention(q, k, vt, n_lat, n_ctx):
    t_rows = q.shape[0]
    pairs = MLA_HEADS // 2
    n_main = n_lat // ATT_TK
    n_chunks = vt.shape[0]
    assert n_main % 2 == 0 and n_main >= 2 and n_chunks == n_main + 1

    def scratch(tq, tk, t_size):
        return [pltpu.VMEM((2, tk, tq), F32), pltpu.VMEM((2, tk, tq), F32), pltpu.VMEM((2, t_size, tq), F32),
                pltpu.VMEM((2, 1, tq), F32), pltpu.VMEM((2, HEAD_PAD, tq), F32)]

    lat = pl.pallas_call(
        functools.partial(_attention_kernel, n_main=n_main, tk=ATT_TK, tail=(n_lat, n_ctx, n_main)),
        grid=(pairs, n_lat // ATT_TQ),
        in_specs=[pl.BlockSpec((ATT_TQ, 2 * HEAD_PAD), lambda p, i: (i, p)),
                  pl.BlockSpec((t_rows, 2 * HEAD_PAD), lambda p, i: (0, p), pipeline_mode=pl.Buffered(1)),
                  pl.BlockSpec((n_chunks, 2 * HEAD_PAD, ATT_TK), lambda p, i: (0, p, 0),
                               pipeline_mode=pl.Buffered(1))],
        out_specs=pl.BlockSpec((ATT_TQ, 2 * V_HEAD), lambda p, i: (i, p)),
        out_shape=jax.ShapeDtypeStruct((n_lat, MLA_WIDTH), BF16),
        scratch_shapes=scratch(ATT_TQ, ATT_TK, n_ctx),
        compiler_params=_params("arbitrary", "arbitrary"),
        name="attention_latent",
    )(q, k, vt)
    cblk = n_lat // n_ctx
    cspec = lambda: pl.BlockSpec((n_ctx, 2 * HEAD_PAD), lambda p: (cblk, p))
    ctx = pl.pallas_call(
        functools.partial(_attention_kernel, n_main=0, tk=SUBLANES, tail=(0, n_ctx, 0)),
        grid=(pairs,),
        in_specs=[cspec(), cspec(), pl.BlockSpec((1, 2 * HEAD_PAD, ATT_TK), lambda p: (n_main, p, 0))],
        out_specs=pl.BlockSpec((n_ctx, 2 * V_HEAD), lambda p: (0, p)),
        out_shape=jax.ShapeDtypeStruct((n_ctx, MLA_WIDTH), BF16),
        scratch_shapes=scratch(n_ctx, SUBLANES, n_ctx),
        compiler_params=_params("arbitrary"),
        name="attention_context",
    )(q, k, vt)
    return jnp.concatenate([lat, ctx], axis=0)


def _halo_specs(cols, tile_rows, n_lat_tiles, n_tiles):
    per = tile_rows // HALO
    last = n_tiles * per - 1
    return [pl.BlockSpec((HALO, cols), lambda i: (jnp.maximum(i * per - 1, 0), 0)),
            pl.BlockSpec((tile_rows, cols), lambda i: (i, 0)),
            pl.BlockSpec((HALO, cols), lambda i: (jnp.minimum((i + 1) * per, last), 0))]


def _with_halo(prev_ref, cur_ref, next_ref, i, n_lat_tiles):
    has_prev = jnp.logical_and(i != 0, i != n_lat_tiles)
    has_next = jnp.logical_and(i != n_lat_tiles - 1, i != pl.num_programs(0) - 1)
    prev = jnp.where(has_prev, prev_ref[...], 0.0)
    nxt = jnp.where(has_next, next_ref[...], 0.0)
    return jnp.concatenate([prev, cur_ref[...], nxt], axis=0)


def _conv_kernel(prev_ref, cur_ref, next_ref, w_ref, b_ref, o_ref, *, n_lat_tiles):
    i = pl.program_id(0)
    ext = _with_halo(prev_ref, cur_ref, next_ref, i, n_lat_tiles)
    rows = ext.shape[0]
    w = w_ref[...]
    y = (pltpu.roll(ext, 2, axis=0) * w[0:1] + pltpu.roll(ext, 1, axis=0) * w[1:2]
         + ext * w[2:3] + pltpu.roll(ext, rows - 1, axis=0) * w[3:4])
    y = y[HALO:rows - HALO] + b_ref[...]
    o_ref[...] = y / (1.0 + jnp.exp(-y))


def _conv(xbc, conv_w, conv_b, n_lat_tiles):
    t_rows, cols = xbc.shape
    nt = t_rows // ROW_TILE
    return pl.pallas_call(
        functools.partial(_conv_kernel, n_lat_tiles=n_lat_tiles),
        grid=(nt,),
        in_specs=_halo_specs(cols, ROW_TILE, n_lat_tiles, nt) + [
            pl.BlockSpec(conv_w.shape, lambda i: (0, 0)), pl.BlockSpec(conv_b.shape, lambda i: (0, 0))],
        out_specs=pl.BlockSpec((ROW_TILE, cols), lambda i: (i, 0)),
        out_shape=jax.ShapeDtypeStruct((t_rows, cols), F32),
        compiler_params=_params("arbitrary"),
        name="ssd_conv",
    )(xbc, xbc, xbc, conv_w, conv_b)


def _softplus(v):
    return jnp.maximum(v, 0.0) + jnp.log(1.0 + jnp.exp(-jnp.abs(v)))


def _ssd_kernel(xf_ref, xb_ref, dtf_ref, dtb_ref, dttf_ref, dttb_ref, bias_ref, biast_ref, alog_ref, alogt_ref,
                dskip_ref, yf_ref, yb_ref, state_ref):
    L = xf_ref.shape[0]
    P, N, H = SSD_HEAD_DIM, SSD_STATE, SSD_HEADS

    @pl.when(pl.program_id(0) == 0)
    def _():
        state_ref[...] = jnp.zeros(state_ref.shape, F32)

    r = lax.broadcasted_iota(jnp.int32, (L, L), 0)
    c = lax.broadcasted_iota(jnp.int32, (L, L), 1)
    lower = r >= c
    upper = r <= c
    lower_f = lower.astype(F32)
    upper_f = upper.astype(F32)
    a_row = -jnp.exp(alog_ref[...])
    a_col = -jnp.exp(alogt_ref[...])

    def one_direction(x_ref, dt_ref, dtt_ref, y_ref, base, forward):
        x = x_ref[...]
        dt = _softplus(dt_ref[...] + bias_ref[...])
        dtt = _softplus(dtt_ref[...] + biast_ref[...])
        tri_col = lower_f if forward else upper_f
        tri_row = upper_f if forward else lower_f
        cs = jnp.dot(tri_col, dt * a_row, precision=HIGHEST, preferred_element_type=F32)
        cst = jnp.dot(dtt * a_col, tri_row, precision=HIGHEST, preferred_element_type=F32)
        mask = lower if forward else upper
        end = L - 1 if forward else 0
        outs = []
        cb = []
        for g in range(SSD_GROUPS):
            bg = x[:, SSD_WIDTH + g * N:SSD_WIDTH + (g + 1) * N].astype(BF16)
            cg = x[:, SSD_WIDTH + (SSD_GROUPS + g) * N:SSD_WIDTH + (SSD_GROUPS + g + 1) * N].astype(BF16)
            cb.append((bg, cg, lax.dot_general(cg, bg, (((1,), (1,)), ((), ())), preferred_element_type=F32)))
        for hd in range(H):
            j = base + hd
            bg, cg, cbg = cb[hd // (H // SSD_GROUPS)]
            col = cs[:, j:j + 1]
            rowv = cst[j:j + 1, :]
            total = cs[end:end + 1, j:j + 1]
            decay = jnp.where(mask, jnp.exp(jnp.minimum(col - rowv, 0.0)), 0.0)
            xh = (x[:, hd * P:(hd + 1) * P] * dt[:, j:j + 1]).astype(BF16)
            y = jnp.dot((cbg * decay).astype(BF16), xh, preferred_element_type=F32)
            st = state_ref[j]
            y = y + jnp.dot(cg, st.astype(BF16), preferred_element_type=F32) * jnp.exp(col)
            bw = (bg.astype(F32) * jnp.exp(total - col)).astype(BF16)
            state_ref[j] = st * jnp.exp(total) + lax.dot_general(
                bw, xh, (((0,), (0,)), ((), ())), preferred_element_type=F32)
            outs.append(y)
        y_all = jnp.concatenate(outs, axis=1)
        if forward:
            y_all = y_all + x[:, :SSD_WIDTH] * dskip_ref[...]
        y_ref[...] = y_all

    one_direction(xf_ref, dtf_ref, dttf_ref, yf_ref, 0, True)
    one_direction(xb_ref, dtb_ref, dttb_ref, yb_ref, H, False)


def _ssd(xc, dt, dtt, bias_row, bias_col, alog_row, alog_col, dskip_row, n_lat_chunks):
    t_rows, cols = xc.shape
    L = SSD_CHUNK
    nc = t_rows // L
    fwd = lambda j: jnp.where(j == 0, n_lat_chunks, j - 1)
    bwd = lambda j: jnp.where(j == 0, n_lat_chunks, n_lat_chunks - j)
    small = lambda a: pl.BlockSpec(a.shape, lambda j: (0,) * a.ndim)
    return pl.pallas_call(
        _ssd_kernel,
        grid=(nc,),
        in_specs=[pl.BlockSpec((L, cols), lambda j: (fwd(j), 0)),
                  pl.BlockSpec((L, cols), lambda j: (bwd(j), 0)),
                  pl.BlockSpec((L, LANES), lambda j: (fwd(j), 0)),
                  pl.BlockSpec((L, LANES), lambda j: (bwd(j), 0)),
                  pl.BlockSpec((2 * SUBLANES, L), lambda j: (0, fwd(j))),
                  pl.BlockSpec((2 * SUBLANES, L), lambda j: (0, bwd(j))),
                  small(bias_row), small(bias_col), small(alog_row), small(alog_col), small(dskip_row)],
        out_specs=[pl.BlockSpec((L, SSD_WIDTH), lambda j: (fwd(j), 0)),
                   pl.BlockSpec((L, SSD_WIDTH), lambda j: (bwd(j), 0))],
        out_shape=[jax.ShapeDtypeStruct((t_rows, SSD_WIDTH), F32)] * 2,
        scratch_shapes=[pltpu.VMEM((2 * SSD_HEADS, SSD_STATE, SSD_HEAD_DIM), F32)],
        compiler_params=_params("arbitrary"),
        name="ssd_scan",
    )(xc, xc, dt, dt, dtt, dtt, bias_row, bias_col, alog_row, alog_col, dskip_row)


def _mixout_kernel(x_ref, mod_ref, att_ref, yf_ref, yb_ref, z_ref, pprev_ref, pcur_ref, pnext_ref,
                   ssdg_ref, poolw_ref, pools_ref, wout_ref, g2_ref, rw_ref, rb_ref,
                   x1_ref, h2_ref, route_ref, cnt_ref, base_ref, *, n_lat_tiles, n_lat, n_ctx):
    i = pl.program_id(0)
    mod = mod_ref[0]
    z = z_ref[...]
    ssd = _rms((yf_ref[...] + yb_ref[...]) * (z / (1.0 + jnp.exp(-z)))) * ssdg_ref[...]
    ext = _with_halo(pprev_ref, pcur_ref, pnext_ref, i, n_lat_tiles)
    rows = ext.shape[0]
    tm = rows - 2 * HALO
    w2 = ext + pltpu.roll(ext, 1, axis=0)
    w4 = pltpu.roll(w2, 1, axis=0) + pltpu.roll(w2, rows - 1, axis=0)
    w8 = pltpu.roll(w4, 2, axis=0) + pltpu.roll(w4, rows - 2, axis=0)
    w16 = pltpu.roll(w8, 4, axis=0) + pltpu.roll(w8, rows - 4, axis=0)
    lane = lax.broadcasted_iota(jnp.int32, (tm, POOL_WIDTH), 1)
    grp = lane // POOL_GROUP
    sl = slice(HALO, rows - HALO)
    wsum = jnp.where(grp == 0, w2[sl], jnp.where(grp == 1, w4[sl], jnp.where(grp == 2, w8[sl], w16[sl])))
    is_ctx = i >= n_lat_tiles
    seg_len = jnp.where(is_ctx, n_ctx, n_lat)
    t = lax.broadcasted_iota(jnp.int32, (tm, POOL_WIDTH), 0) + jnp.where(is_ctx, i - n_lat_tiles, i) * tm
    half = jnp.left_shift(1, grp)
    lo = jnp.clip(t - half, 0, seg_len)
    hi = jnp.clip(t + half, 0, seg_len)
    p = wsum / (hi - lo).astype(F32) - pcur_ref[...]
    pool = jnp.dot(p.astype(BF16), poolw_ref[...], preferred_element_type=F32) * pools_ref[...]
    mix = (jnp.dot(att_ref[...], wout_ref[0:MLA_WIDTH], preferred_element_type=F32)
           + jnp.dot(ssd.astype(BF16), wout_ref[MLA_WIDTH:MLA_WIDTH + SSD_WIDTH], preferred_element_type=F32)
           + jnp.dot(pool.astype(BF16), wout_ref[MLA_WIDTH + SSD_WIDTH:], preferred_element_type=F32))
    x1 = x_ref[...] + mod[2:3] * mix
    x1_ref[...] = x1
    h2 = _rms(x1) * g2_ref[...] * (1.0 + mod[4:5]) + mod[3:4]
    h2b = h2.astype(BF16)
    for kk in range(TOP_K):
        h2_ref[kk] = h2b
    logits = jnp.dot(h2, rw_ref[...], precision=HIGHEST, preferred_element_type=F32) + rb_ref[...]

    @pl.when(i == 0)
    def _():
        base_ref[...] = jnp.zeros(base_ref.shape, F32)

    elane = lax.broadcasted_iota(jnp.int32, logits.shape, 1)
    lg = jnp.where(elane < N_EXPERTS, logits, -jnp.inf)
    rr = lax.broadcasted_iota(jnp.int32, (tm, tm), 0)
    cc = lax.broadcasted_iota(jnp.int32, (tm, tm), 1)
    earlier = (rr > cc).astype(BF16)
    offset = base_ref[...]
    tops, ids, ranks = [], [], []
    for kk in range(TOP_K):
        top = jnp.max(lg, axis=-1, keepdims=True)
        idx = jnp.min(jnp.where(lg == top, elane, LANES), axis=-1, keepdims=True)
        sel = elane == idx
        lg = jnp.where(sel, -jnp.inf, lg)
        onehot = sel.astype(BF16)
        before = jnp.dot(earlier, onehot, preferred_element_type=F32) + offset
        ranks.append(jnp.sum(jnp.where(sel, before, 0.0), axis=-1, keepdims=True))
        offset = offset + jnp.sum(sel.astype(F32), axis=0, keepdims=True)
        tops.append(top)
        ids.append(idx.astype(F32))
    base_ref[...] = offset
    cnt_ref[...] = jnp.broadcast_to(offset, cnt_ref.shape)
    exps = [jnp.exp(tp - tops[0]) for tp in tops]
    denom = exps[0] + exps[1] + exps[2] + exps[3]
    route = jnp.zeros(logits.shape, F32)
    for kk in range(TOP_K):
        route = jnp.where(elane == kk, ids[kk], route)
        route = jnp.where(elane == TOP_K + kk, exps[kk] / denom, route)
        route = jnp.where(elane == 2 * TOP_K + kk, ranks[kk], route)
    route_ref[...] = route


def _mixer_output(x, modtab, att, yf, yb, z, pool_in, ssdg, poolw_bd, pools, wout, g2, rw_p, rb_p,
                  n_lat_tiles, n_lat, n_ctx):
    t_rows, d = x.shape
    nt = t_rows // ROW_TILE
    row = lambda cols: pl.BlockSpec((ROW_TILE, cols), lambda i: (i, 0))
    full = lambda a: pl.BlockSpec(a.shape, lambda i: (0,) * a.ndim)
    return pl.pallas_call(
        functools.partial(_mixout_kernel, n_lat_tiles=n_lat_tiles, n_lat=n_lat, n_ctx=n_ctx),
        grid=(nt,),
        in_specs=[row(d), pl.BlockSpec((1, SUBLANES, d), lambda i: (i // n_lat_tiles, 0, 0)),
                  row(MLA_WIDTH), row(SSD_WIDTH), row(SSD_WIDTH), row(SSD_WIDTH)]
                 + _halo_specs(POOL_WIDTH, ROW_TILE, n_lat_tiles, nt)
                 + [full(ssdg), full(poolw_bd), full(pools), full(wout), full(g2), full(rw_p), full(rb_p)],
        out_specs=[row(d), pl.BlockSpec((TOP_K, ROW_TILE, d), lambda i: (0, i, 0)), row(LANES),
                   pl.BlockSpec((SUBLANES, LANES), lambda i: (0, 0))],
        out_shape=[jax.ShapeDtypeStruct((t_rows, d), F32), jax.ShapeDtypeStruct((TOP_K, t_rows, d), BF16),
                   jax.ShapeDtypeStruct((t_rows, LANES), F32), jax.ShapeDtypeStruct((SUBLANES, LANES), F32)],
        scratch_shapes=[pltpu.VMEM((1, LANES), F32)],
        compiler_params=_params("arbitrary"),
        name="mixer_output",
    )(x, modtab, att, yf, yb, z, pool_in, pool_in, pool_in, ssdg, poolw_bd, pools, wout, g2, rw_p, rb_p)


def _expert_kernel(be_ref, nu_ref, x_ref, wg_ref, bg_ref, wu_ref, bu_ref, wd_ref, bd_ref, o_ref,
                   wg_s, wu_s, wd_s):
    i = pl.program_id(0)

    @pl.when(jnp.logical_or(i == 0, be_ref[i] != be_ref[jnp.maximum(i - 1, 0)]))
    def _():
        wg_s[...] = wg_ref[0, 0].astype(BF16)
        wu_s[...] = wu_ref[0, 0].astype(BF16)
        wd_s[...] = wd_ref[0, 0].astype(BF16)

    @pl.when(i < nu_ref[0])
    def _():
        x = x_ref[...]
        g = jnp.minimum(jnp.dot(x, wg_s[...], preferred_element_type=F32) + bg_ref[0, 0], SWIGLU_LIMIT)
        u = jnp.clip(jnp.dot(x, wu_s[...], preferred_element_type=F32) + bu_ref[0, 0],
                     -SWIGLU_LIMIT, SWIGLU_LIMIT)
        a = g / (1.0 + jnp.exp(-SWIGLU_ALPHA * g)) * (u + 1.0)
        y = jnp.dot(a.astype(BF16), wd_s[...], preferred_element_type=F32) + bd_ref[0, 0]
        o_ref[...] = y.astype(o_ref.dtype)

    @pl.when(i >= nu_ref[0])
    def _():
        o_ref[...] = jnp.zeros(o_ref.shape, o_ref.dtype)


def _experts(layer, x_sorted, block_expert, n_used, wg, bg, wu, bu, wd, bd):
    n_rows, d = x_sorted.shape
    depth, ne, _, dff = wg.shape
    rowspec = lambda cols: pl.BlockSpec((MOE_TILE, cols), lambda i, be, nu: (i, 0))
    wspec = lambda a, b: pl.BlockSpec((1, 1, a, b), lambda i, be, nu: (layer, be[i], 0, 0))
    grid_spec = pltpu.PrefetchScalarGridSpec(
        num_scalar_prefetch=2,
        grid=(n_rows // MOE_TILE,),
        in_specs=[rowspec(d), wspec(d, dff), wspec(1, dff), wspec(d, dff), wspec(1, dff),
                  wspec(dff, d), wspec(1, d)],
        out_specs=rowspec(d),
        scratch_shapes=[pltpu.VMEM((d, dff), BF16), pltpu.VMEM((d, dff), BF16), pltpu.VMEM((dff, d), BF16)],
    )
    return pl.pallas_call(
        _expert_kernel,
        grid_spec=grid_spec,
        out_shape=jax.ShapeDtypeStruct((n_rows, d), BF16),
        compiler_params=_params("arbitrary"),
        name="moe_experts",
    )(block_expert, n_used, x_sorted, wg, bg.reshape(depth, ne, 1, dff), wu, bu.reshape(depth, ne, 1, dff),
      wd, bd.reshape(depth, ne, 1, d))


def _dispatch_plan(route, counts_f):
    t_rows = route.shape[0]
    n_assign = t_rows * TOP_K
    ids = route[:, 0:TOP_K].astype(jnp.int32)
    rank =route[:, 2 * TOP_K:3 * TOP_K].astype(jnp.int32)
    counts = counts_f[0, :N_EXPERTS].astype(jnp.int32)
    padded = (counts + MOE_TILE - 1) // MOE_TILE * MOE_TILE
    pad_end = jnp.cumsum(padded)
    pad_start = pad_end - padded
    experts = jnp.arange(N_EXPERTS, dtype=jnp.int32)
    first_row = jnp.sum(jnp.where(ids[:, :, None] == experts[None, None, :], pad_start[None, None, :], 0), axis=-1)
    dest = (first_row + rank).T.reshape(-1)
    n_rows = n_assign + N_EXPERTS * MOE_TILE
    n_blocks = n_rows // MOE_TILE
    block_expert = jnp.minimum(
        jnp.sum((jnp.arange(n_blocks, dtype=jnp.int32)[:, None] * MOE_TILE >= pad_end[None, :]).astype(jnp.int32),
                axis=1), N_EXPERTS - 1).astype(jnp.int32)
    n_pad = n_rows - n_assign
    pads = jnp.concatenate([padded - counts, n_rows - pad_end[-1:]])
    pads_end = jnp.cumsum(pads)
    pad_first = jnp.concatenate([pad_start + counts, pad_end[-1:]])
    j = jnp.arange(n_pad, dtype=jnp.int32)
    grp = jnp.sum((j[:, None] >= pads_end[None, :]).astype(jnp.int32), axis=1)
    hit = grp[:, None] == jnp.arange(N_EXPERTS + 1, dtype=jnp.int32)[None, :]
    pad_rows = j + jnp.sum(jnp.where(hit, (pad_first - (pads_end - pads))[None, :], 0), axis=1)
    keys = jnp.concatenate([dest, pad_rows]).astype(jnp.int32)
    vals = jnp.concatenate([jnp.arange(n_assign, dtype=jnp.int32), j])
    row_src = lax.sort((keys, vals), num_keys=1)[1]
    n_used = (pad_end[-1:] // MOE_TILE).astype(jnp.int32)
    return dest, row_src, block_expert, n_used


def _rope_tables(n_lat, n_ctx):
    rows = n_lat // GRID_W
    row = jnp.repeat(jnp.arange(rows), GRID_W).astype(F32)
    col = jnp.tile(jnp.arange(GRID_W), rows).astype(F32)
    n_freq = QK_ROPE // 4
    inv_freq = ROPE_THETA ** (-jnp.arange(n_freq, dtype=F32) / n_freq)
    ang = jnp.stack([row[:, None] * inv_freq, col[:, None] * inv_freq], axis=1)
    cos, sin = jnp.cos(ang), jnp.sin(ang)
    zero = jnp.zeros_like(sin)
    ones = lambda w: jnp.ones((n_lat, w), F32)
    zeros = lambda w: jnp.zeros((n_lat, w), F32)
    per_axis = lambda a, b: jnp.stack([a, b], axis=2).reshape(n_lat, QK_ROPE)
    tail = HEAD_PAD - QK_DIM
    c = jnp.concatenate([ones(QK_NOPE), per_axis(cos, cos), ones(tail)], axis=1)
    sa = jnp.concatenate([zeros(QK_NOPE), per_axis(zero, sin), zeros(tail)], axis=1)
    sb = jnp.concatenate([zeros(QK_NOPE), per_axis(-sin, zero), zeros(tail)], axis=1)
    pad = lambda a, v: jnp.concatenate([a, jnp.full((n_ctx, HEAD_PAD), v, F32)], axis=0)
    return pad(c, 1.0), pad(sa, 0.0), pad(sb, 0.0)


def _pad_cols(a, width):
    return jnp.pad(a, ((0, 0), (0, width - a.shape[1])))


def _pack_layer(i, w_in, q_g, kv_g, w_uq, w_ukv, qn_g, kn_g, conv_w, conv_b, a_log, dt_bias, d_skip, ssd_g,
                pool_w, pool_scale, w_out, router_w, router_b):
    d = w_in.shape[1]
    pts = [0, Q_LORA, KV_LORA, QK_ROPE, SSD_WIDTH, SSD_CONV_DIM, 2 * SSD_HEADS, POOL_WIDTH]
    offs = [sum(pts[:j + 1]) for j in range(len(pts))]
    seg = lambda j: w_in[i][:, offs[j]:offs[j + 1]]
    zc = lambda w: jnp.zeros((d, w), F32)
    w_in_p = jnp.concatenate([
        seg(0), seg(1), zc(QK_NOPE), seg(2), zc(LANES - QK_DIM), seg(3), seg(4),
        seg(5), zc(LANES - 2 * SSD_HEADS), seg(6)], axis=1).astype(BF16)
    wuq = w_uq[i].reshape(Q_LORA, MLA_HEADS, QK_DIM)
    wuq_p = jnp.pad(wuq, ((0, 0), (0, 0), (0, HEAD_PAD - QK_DIM))).reshape(Q_LORA, MLA_HEADS * HEAD_PAD)
    wukv = w_ukv[i].reshape(KV_LORA, MLA_HEADS, QK_NOPE + V_HEAD)
    wuk_p = jnp.pad(wukv[:, :, :QK_NOPE], ((0, 0), (0, 0), (0, HEAD_PAD - QK_NOPE))).reshape(
        KV_LORA, MLA_HEADS * HEAD_PAD)
    wuv = jnp.pad(wukv[:, :, QK_NOPE:], ((0, 0), (0, 0), (0, HEAD_PAD - V_HEAD))).reshape(
        KV_LORA, MLA_HEADS * HEAD_PAD).T
    flat12 = lambda a: a[i].reshape(1, 2 * SSD_HEADS)
    poolw_bd = jax.scipy.linalg.block_diag(*[pool_w[i][g] for g in range(len(POOL_WINDOWS))])
    return dict(
        w_in_p=w_in_p, qg=q_g[i][None], kvg=kv_g[i][None],
        wuq_p=wuq_p.astype(BF16), wuk_p=wuk_p.astype(BF16), wuv=wuv.astype(BF16),
        qng_p=_pad_cols(qn_g[i][None], HEAD_PAD), kng_p=_pad_cols(kn_g[i][None], HEAD_PAD),
        conv_w=conv_w[i], conv_b=conv_b[i][None],
        bias_row=_pad_cols(flat12(dt_bias), LANES), bias_col=_pad_cols(flat12(dt_bias), 2 * SUBLANES).T,
        alog_row=_pad_cols(flat12(a_log), LANES), alog_col=_pad_cols(flat12(a_log), 2 * SUBLANES).T,
        dskip_row=jnp.repeat(d_skip[i], SSD_HEAD_DIM)[None],
        ssdg=ssd_g[i][None], poolw_bd=poolw_bd.astype(BF16), pools=pool_scale[i][None],
        wout=w_out[i].astype(BF16), rw_p=_pad_cols(router_w[i], LANES), rb_p=_pad_cols(router_b[i][None], LANES))


def kernel(x, c, ctx, c_ctx, norm1_g, norm2_g, mod_w, mod_b, w_in, mla_q_norm_g, mla_kv_norm_g, mla_w_uq, mla_w_ukv, mla_qn_g, mla_kn_g, ssd_conv_w, ssd_conv_b, ssd_a_log, ssd_dt_bias, ssd_d, ssd_norm_g, pool_w, pool_scale, w_out, router_w, router_b, moe_w_gate, moe_b_gate, moe_w_up, moe_b_up, moe_w_down, moe_b_down):
    batch, n_lat, d = x.shape
    n_ctx = ctx.shape[1]
    depth = mod_w.shape[0]
    assert batch == 1 and d == D_MODEL and n_ctx == SSD_CHUNK == ROW_TILE
    assert n_lat % ATT_TQ == 0 and n_lat % ATT_TK == 0 and n_lat % GRID_W == 0
    n_lat_tiles = n_lat // ROW_TILE

    cvec = jnp.zeros((SUBLANES, d), F32).at[0].set(c[0]).at[1].set(c_ctx)
    mods = _modulation(cvec, mod_w, mod_b)
    rope_c, rope_sa, rope_sb = _rope_tables(n_lat, n_ctx)

    xs = jnp.concatenate([x[0], ctx[0]], axis=0)
    moe = None
    for i in range(depth):
        lp = _pack_layer(i, w_in, mla_q_norm_g, mla_kv_norm_g, mla_w_uq, mla_w_ukv, mla_qn_g, mla_kn_g,
                         ssd_conv_w, ssd_conv_b, ssd_a_log, ssd_dt_bias, ssd_d, ssd_norm_g, pool_w, pool_scale,
                         w_out, router_w, router_b)
        modtab = jnp.pad(mods[i, :2].reshape(2, 6, d), ((0, 0), (0, SUBLANES - 6), (0, 0)))
        outs = _in_projection(
            xs, moe, modtab, norm1_g[i][None], lp['w_in_p'], lp['qg'], lp['kvg'], lp['wuq_p'], lp['wuk_p'], lp['wuv'],
            lp['qng_p'], lp['kng_p'], rope_c, rope_sa, rope_sb, n_lat_tiles)
        if moe is not None:
            xs, outs = outs[0], outs[1:]
        q, k, v, z, xbc, dt, pool_in = outs
        att = _attention(q, k, v, n_lat, n_ctx)
        xc = _conv(xbc, lp['conv_w'], lp['conv_b'], n_lat_tiles)
        dtt = dt[:, :2 * SUBLANES].T
        yf, yb = _ssd(xc, dt, dtt, lp['bias_row'], lp['bias_col'], lp['alog_row'], lp['alog_col'],
                      lp['dskip_row'], n_lat // SSD_CHUNK)
        x1, h2_slots, route, counts = _mixer_output(
            xs, modtab, att, yf, yb, z, pool_in, lp['ssdg'], lp['poolw_bd'], lp['pools'], lp['wout'],
            norm2_g[i][None], lp['rw_p'], lp['rb_p'], n_lat_tiles, n_lat, n_ctx)
        dest, row_src, block_expert, n_used = _dispatch_plan(route, counts)
        x_sorted = h2_slots.reshape(-1, d).at[row_src].get(mode='promise_in_bounds')
        out = _experts(i, x_sorted, block_expert, n_used,
                       moe_w_gate, moe_b_gate, moe_w_up, moe_b_up, moe_w_down, moe_b_down)
        picked = out.at[dest].get(mode='promise_in_bounds', unique_indices=True)
        xs, moe = x1, (picked.reshape(TOP_K, xs.shape[0], d), route, modtab)
    return _final_combine(xs, *moe, n_lat)[None]
```

```python
import functools

import jax
import jax.numpy as jnp
from jax import lax
from jax.experimental import pallas as pl
from jax.experimental.pallas import tpu as pltpu

F32 = jnp.float32
BF16 = jnp.bfloat16
HIGHEST = lax.Precision.HIGHEST

D_MODEL = 1024
GRID_W = 64
EPS = 1e-6
MLA_HEADS = 6
QK_NOPE = 64
QK_ROPE = 32
QK_DIM = QK_NOPE + QK_ROPE
V_HEAD = 64
Q_LORA = 256
KV_LORA = 128
MLA_WIDTH = MLA_HEADS * V_HEAD
ROPE_THETA = 10000.0
SSD_HEADS = 6
SSD_HEAD_DIM = 64
SSD_WIDTH = SSD_HEADS * SSD_HEAD_DIM
SSD_GROUPS = 2
SSD_STATE = 64
SSD_CONV = 4
SSD_CONV_DIM = SSD_WIDTH + 2 * SSD_GROUPS * SSD_STATE
POOL_WINDOWS = (2, 4, 8, 16)
POOL_GROUP = 64
POOL_WIDTH = len(POOL_WINDOWS) * POOL_GROUP
N_EXPERTS = 32
TOP_K = 4
D_FF = 1024
SWIGLU_LIMIT = 7.0
SWIGLU_ALPHA = 1.702

LANES = 128
SUBLANES = 8
HEAD_PAD = LANES
Q_SCALE = QK_DIM ** -0.5 * 1.4426950408889634

ROW_TILE = 256
SSD_CHUNK = 256
HALO = SUBLANES
ATT_TQ = 1024
ATT_TK = 512
MOE_TILE = 512
VMEM_LIMIT = 56 * 1024 * 1024

COL_QLAT = 0
COL_KVLAT = COL_QLAT + Q_LORA
COL_KROPE = COL_KVLAT + KV_LORA
COL_Z = COL_KROPE + LANES
COL_XBC = COL_Z + SSD_WIDTH
COL_DT = COL_XBC + SSD_CONV_DIM
COL_POOL = COL_DT + LANES
IN_COLS_PACKED = COL_POOL + POOL_WIDTH


def _rms(x):
    return x * lax.rsqrt(jnp.mean(x * x, axis=-1, keepdims=True) + EPS)


def _params(*sem):
    return pltpu.CompilerParams(dimension_semantics=sem, vmem_limit_bytes=VMEM_LIMIT)


def _mod_kernel(c_ref, w_ref, b_ref, o_ref):
    c = c_ref[...]
    s = c / (1.0 + jnp.exp(-c))
    o_ref[0] = jnp.dot(s, w_ref[0], precision=HIGHEST, preferred_element_type=F32) + b_ref[0]


def _modulation(cvec, mod_w, mod_b):
    depth, d, cols = mod_w.shape
    tn = 1536
    return pl.pallas_call(
        _mod_kernel,
        grid=(depth, cols // tn),
        in_specs=[pl.BlockSpec((SUBLANES, d), lambda l, j: (0, 0)),
                  pl.BlockSpec((1, d, tn), lambda l, j: (l, 0, j)),
                  pl.BlockSpec((1, 1, tn), lambda l, j: (l, 0, j))],
        out_specs=pl.BlockSpec((1, SUBLANES, tn), lambda l, j: (l, 0, j)),
        out_shape=jax.ShapeDtypeStruct((depth, SUBLANES, cols), F32),
        compiler_params=_params("arbitrary", "arbitrary"),
        name="modulation",
    )(cvec, mod_w, mod_b.reshape(depth, 1, cols))


def _moe_combine(x1_ref, picked_ref, route_ref, pmod_ref):
    route = route_ref[...]
    y = picked_ref[0].astype(F32) * route[:, TOP_K:TOP_K + 1]
    for kk in range(1, TOP_K):
        y = y + picked_ref[kk].astype(F32) * route[:, TOP_K + kk:TOP_K + kk + 1]
    return x1_ref[...] + pmod_ref[0][5:6] * y


def _combine_kernel(x1_ref, picked_ref, route_ref, pmod_ref, o_ref):
    o_ref[...] = _moe_combine(x1_ref, picked_ref, route_ref, pmod_ref)


def _final_combine(x1, picked, route, pmodtab, n_lat):
    d = x1.shape[1]
    row = lambda cols: pl.BlockSpec((ROW_TILE, cols), lambda i: (i, 0))
    return pl.pallas_call(
        _combine_kernel,
        grid=(n_lat // ROW_TILE,),
        in_specs=[row(d), pl.BlockSpec((TOP_K, ROW_TILE, d), lambda i: (0, i, 0)), row(LANES),
                  pl.BlockSpec((1, SUBLANES, d), lambda i: (0, 0, 0))],
        out_specs=row(d),
        out_shape=jax.ShapeDtypeStruct((n_lat, d), F32),
        compiler_params=_params("arbitrary"),
        name="moe_combine",
    )(x1, picked, route, pmodtab)


def _inproj_kernel(*refs, has_moe):
    if has_moe:
        x1_ref, picked_ref, route_ref, pmod_ref = refs[:4]
        refs = refs[4:]
    else:
        x_ref = refs[0]
        refs = refs[1:]
    (mod_ref, g1_ref, win_ref, qg_ref, kvg_ref, wuq_ref, wuk_ref, wuv_ref, qng_ref, kng_ref,
     rc_ref, rsa_ref, rsb_ref) = refs[:13]
    outs = refs[13:]
    if has_moe:
        xs_ref, outs = outs[0], outs[1:]
        x = _moe_combine(x1_ref, picked_ref, route_ref, pmod_ref)
        xs_ref[...] = x
    else:
        x = x_ref[...]
    q_ref, k_ref, v_ref, z_ref, xbc_ref, dt_ref, pool_ref = outs
    mod = mod_ref[0]
    shift, scale = mod[0:1], mod[1:2]
    h = _rms(x) * g1_ref[...] * (1.0 + scale) + shift
    proj = jnp.dot(h.astype(BF16), win_ref[...], preferred_element_type=F32)
    z_ref[...] = proj[:, COL_Z:COL_XBC]
    xbc_ref[...] = proj[:, COL_XBC:COL_DT]
    dt_ref[...] = proj[:, COL_DT:COL_POOL]
    pool_ref[...] = proj[:, COL_POOL:IN_COLS_PACKED]

    rc, rsa, rsb = rc_ref[...], rsa_ref[...], rsb_ref[...]

    def head_norm_rope(t, gain):
        ms = jnp.sum(t * t, axis=-1, keepdims=True) * (1.0 / QK_DIM)
        t = t * lax.rsqrt(ms + EPS) * gain
        return (t * rc + pltpu.roll(t, QK_ROPE // 4, axis=1) * rsa
                + pltpu.roll(t, HEAD_PAD - QK_ROPE // 4, axis=1) * rsb)

    qn = (_rms(proj[:, COL_QLAT:COL_KVLAT]) * qg_ref[...]).astype(BF16)
    q_all = jnp.dot(qn, wuq_ref[...], preferred_element_type=F32)
    kvn = (_rms(proj[:, COL_KVLAT:COL_KROPE]) * kvg_ref[...]).astype(BF16)
    k_all = jnp.dot(kvn, wuk_ref[...], preferred_element_type=F32)
    k_rope = proj[:, COL_KROPE:COL_Z]
    vt_all = lax.dot_general(wuv_ref[...], kvn, (((1,), (1,)), ((), ())), preferred_element_type=F32)
    vrow = lax.broadcasted_iota(jnp.int32, vt_all.shape, 0) % HEAD_PAD
    v_ref[0] = jnp.where(vrow == V_HEAD, 1.0, vt_all).astype(BF16)
    qng, kng = qng_ref[...], kng_ref[...]
    for hd in range(MLA_HEADS):
        sl = slice(hd * HEAD_PAD, (hd + 1) * HEAD_PAD)
        q_ref[:, sl] = (head_norm_rope(q_all[:, sl], qng) * Q_SCALE).astype(BF16)
        k_ref[:, sl] = head_norm_rope(k_all[:, sl] + k_rope, kng).astype(BF16)


def _in_projection(x, moe, modtab, g1, w_in_p, qg, kvg, wuq_p, wuk_p, wuv, qng_p, kng_p, rope_c, rope_sa, rope_sb,
                   n_lat_tiles):
    t_rows, d = x.shape
    nt = t_rows // ROW_TILE
    row = lambda cols: pl.BlockSpec((ROW_TILE, cols), lambda i: (i, 0))
    full = lambda a: pl.BlockSpec(a.shape, lambda i: (0,) * a.ndim)
    seg = lambda: pl.BlockSpec((1, SUBLANES, d), lambda i: (i // n_lat_tiles, 0, 0))
    hw = MLA_HEADS * HEAD_PAD
    per = ATT_TK // ROW_TILE
    out_specs = [row(hw), row(hw), pl.BlockSpec((1, hw, ROW_TILE), lambda i: (i // per, 0, i % per)),
                 row(SSD_WIDTH), row(SSD_CONV_DIM), row(LANES), row(POOL_WIDTH)]
    sds = lambda cols, dt: jax.ShapeDtypeStruct((t_rows, cols), dt)
    out_shape = [sds(hw, BF16), sds(hw, BF16), jax.ShapeDtypeStruct((pl.cdiv(nt, per), hw, ATT_TK), BF16),
                 sds(SSD_WIDTH, F32), sds(SSD_CONV_DIM, F32), sds(LANES, F32), sds(POOL_WIDTH, F32)]
    lead_in, lead_specs = [x], [row(d)]
    if moe is not None:
        picked, route, pmodtab = moe
        lead_in += [picked, route, pmodtab]
        lead_specs += [pl.BlockSpec((TOP_K, ROW_TILE, d), lambda i: (0, i, 0)), row(LANES), seg()]
        out_specs = [row(d)] + out_specs
        out_shape = [sds(d, F32)] + out_shape
    return pl.pallas_call(
        functools.partial(_inproj_kernel, has_moe=moe is not None),
        grid=(nt,),
        in_specs=lead_specs + [seg(), full(g1), full(w_in_p), full(qg), full(kvg), full(wuq_p), full(wuk_p),
                               full(wuv), full(qng_p), full(kng_p), row(HEAD_PAD), row(HEAD_PAD), row(HEAD_PAD)],
        out_specs=out_specs,
        out_shape=out_shape,
        compiler_params=_params("arbitrary"),
        name="in_projection",
    )(*lead_in, modtab, g1, w_in_p, qg, kvg, wuq_p, wuk_p, wuv, qng_p, kng_p, rope_c, rope_sa, rope_sb)


def _attention_kernel(q_ref, k_ref, vt_ref, o_ref, sa_ref, sb_ref, st_ref, ca_ref, cb_ref, ct_ref, m_ref, acc_ref,
                      *, n_main, tk, tail):
    m_ref[...] = jnp.full(m_ref.shape, -jnp.inf, F32)
    acc_ref[...] = jnp.zeros(acc_ref.shape, F32)

    def scores(buf, start, size):
        s_ref, cmax_ref = buf
        for hh in range(2):
            q = q_ref[:, hh * HEAD_PAD:(hh + 1) * HEAD_PAD]
            k = k_ref[pl.ds(start, size), hh * HEAD_PAD:(hh + 1) * HEAD_PAD]
            s = lax.dot_general(k, q, (((1,), (1,)), ((), ())), preferred_element_type=F32)
            s_ref[hh] = s
            cmax_ref[hh] = jnp.max(s, axis=0, keepdims=True)

    def consume(buf, c, size):
        s_ref, cmax_ref = buf
        for hh in range(2):
            s = s_ref[hh]
            m_old = m_ref[hh]
            m_new = jnp.maximum(m_old, cmax_ref[hh])
            p = jnp.exp2((s - m_new).astype(BF16))
            vt = vt_ref[c, hh * HEAD_PAD:(hh + 1) * HEAD_PAD, 0:size]
            acc_ref[hh] = jnp.exp2(m_old - m_new) * acc_ref[hh] + jnp.dot(vt, p, preferred_element_type=F32)
            m_ref[hh] = m_new

    row0 = lambda c: pl.multiple_of(c * tk, tk)
    t_start, t_size, t_chunk = tail
    sa_ref, sb_ref, st_ref = (sa_ref, ca_ref), (sb_ref, cb_ref), (st_ref, ct_ref)
    scores(st_ref, t_start, t_size)
    if n_main > 0:
        scores(sa_ref, 0, tk)
    consume(st_ref, t_chunk, t_size)
    if n_main > 0:
        def body(j, carry):
            scores(sb_ref, row0(2 * j + 1), tk)
            consume(sa_ref, 2 * j, tk)
            scores(sa_ref, row0(2 * j + 2), tk)
            consume(sb_ref, 2 * j + 1, tk)
            return carry
        lax.fori_loop(0, n_main // 2 - 1, body, 0)
        scores(sb_ref, (n_main - 1) * tk, tk)
        consume(sa_ref, n_main - 2, tk)
        consume(sb_ref, n_main - 1, tk)
    outs = []
    for hh in range(2):
        acc = acc_ref[hh]
        outs.append(acc[0:V_HEAD] / acc[V_HEAD:V_HEAD + 1])
    o_ref[...] = jnp.concatenate(outs, axis=0).T.astype(o_ref.dtype)


def _attention(q, k, vt, n_lat, n_ctx):
    t_rows = q.shape[0]
    pairs = MLA_HEADS // 2
    n_main = n_lat // ATT_TK
    n_chunks = vt.shape[0]
    assert n_main % 2 == 0 and n_main >= 2 and n_chunks == n_main + 1

    def scratch(tq, tk, t_size):
        return ([pltpu.VMEM((2, tk, tq), F32), pltpu.VMEM((2, tk, tq), F32), pltpu.VMEM((2, t_size, tq), F32)]
                + [pltpu.VMEM((2, 1, tq), F32)] * 4 + [pltpu.VMEM((2, HEAD_PAD, tq), F32)])

    lat = pl.pallas_call(
        functools.partial(_attention_kernel, n_main=n_main, tk=ATT_TK, tail=(n_lat, n_ctx, n_main)),
        grid=(pairs, n_lat // ATT_TQ),
        in_specs=[pl.BlockSpec((ATT_TQ, 2 * HEAD_PAD), lambda p, i: (i, p)),
                  pl.BlockSpec((t_rows, 2 * HEAD_PAD), lambda p, i: (0, p), pipeline_mode=pl.Buffered(1)),
                  pl.BlockSpec((n_chunks, 2 * HEAD_PAD, ATT_TK), lambda p, i: (0, p, 0),
                               pipeline_mode=pl.Buffered(1))],
        out_specs=pl.BlockSpec((ATT_TQ, 2 * V_HEAD), lambda p, i: (i, p)),
        out_shape=jax.ShapeDtypeStruct((n_lat, MLA_WIDTH), BF16),
        scratch_shapes=scratch(ATT_TQ, ATT_TK, n_ctx),
        compiler_params=_params("arbitrary", "arbitrary"),
        name="attention_latent",
    )(q, k, vt)
    cblk = n_lat // n_ctx
    cspec = lambda: pl.BlockSpec((n_ctx, 2 * HEAD_PAD), lambda p: (cblk, p))
    ctx = pl.pallas_call(
        functools.partial(_attention_kernel, n_main=0, tk=SUBLANES, tail=(0, n_ctx, 0)),
        grid=(pairs,),
        in_specs=[cspec(), cspec(), pl.BlockSpec((1, 2 * HEAD_PAD, ATT_TK), lambda p: (n_main, p, 0))],
        out_specs=pl.BlockSpec((n_ctx, 2 * V_HEAD), lambda p: (0, p)),
        out_shape=jax.ShapeDtypeStruct((n_ctx, MLA_WIDTH), BF16),
        scratch_shapes=scratch(n_ctx, SUBLANES, n_ctx),
        compiler_params=_params("arbitrary"),
        name="attention_context",
    )(q, k, vt)
    return jnp.concatenate([lat, ctx], axis=0)


def _halo_specs(cols, tile_rows, n_lat_tiles, n_tiles):
    per = tile_rows // HALO
    last = n_tiles * per - 1
    return [pl.BlockSpec((HALO, cols), lambda i: (jnp.maximum(i * per - 1, 0), 0)),
            pl.BlockSpec((tile_rows, cols), lambda i: (i, 0)),
            pl.BlockSpec((HALO, cols), lambda i: (jnp.minimum((i + 1) * per, last), 0))]


def _with_halo(prev_ref, cur_ref, next_ref, i, n_lat_tiles):
    has_prev = jnp.logical_and(i != 0, i != n_lat_tiles)
    has_next = jnp.logical_and(i != n_lat_tiles - 1, i != pl.num_programs(0) - 1)
    prev = jnp.where(has_prev, prev_ref[...], 0.0)
    nxt = jnp.where(has_next, next_ref[...], 0.0)
    return jnp.concatenate([prev, cur_ref[...], nxt], axis=0)


def _conv_kernel(prev_ref, cur_ref, next_ref, w_ref, b_ref, o_ref, *, n_lat_tiles):
    i = pl.program_id(0)
    ext = _with_halo(prev_ref, cur_ref, next_ref, i, n_lat_tiles)
    rows = ext.shape[0]
    w = w_ref[...]
    y = (pltpu.roll(ext, 2, axis=0) * w[0:1] + pltpu.roll(ext, 1, axis=0) * w[1:2]
         + ext * w[2:3] + pltpu.roll(ext, rows - 1, axis=0) * w[3:4])
    y = y[HALO:rows - HALO] + b_ref[...]
    o_ref[...] = y / (1.0 + jnp.exp(-y))


def _conv(xbc, conv_w, conv_b, n_lat_tiles):
    t_rows, cols = xbc.shape
    nt = t_rows // ROW_TILE
    return pl.pallas_call(
        functools.partial(_conv_kernel, n_lat_tiles=n_lat_tiles),
        grid=(nt,),
        in_specs=_halo_specs(cols, ROW_TILE, n_lat_tiles, nt) + [
            pl.BlockSpec(conv_w.shape, lambda i: (0, 0)), pl.BlockSpec(conv_b.shape, lambda i: (0, 0))],
        out_specs=pl.BlockSpec((ROW_TILE, cols), lambda i: (i, 0)),
        out_shape=jax.ShapeDtypeStruct((t_rows, cols), F32),
        compiler_params=_params("arbitrary"),
        name="ssd_conv",
    )(xbc, xbc, xbc, conv_w, conv_b)


def _softplus(v):
    return jnp.maximum(v, 0.0) + jnp.log(1.0 + jnp.exp(-jnp.abs(v)))


def _ssd_kernel(xf_ref, xb_ref, dtf_ref, dtb_ref, dttf_ref, dttb_ref, bias_ref, biast_ref, alog_ref, alogt_ref,
                dskip_ref, yf_ref, yb_ref, state_ref):
    L = xf_ref.shape[0]
    P, N, H = SSD_HEAD_DIM, SSD_STATE, SSD_HEADS

    @pl.when(pl.program_id(0) == 0)
    def _():
        state_ref[...] = jnp.zeros(state_ref.shape, F32)

    r = lax.broadcasted_iota(jnp.int32, (L, L), 0)
    c = lax.broadcasted_iota(jnp.int32, (L, L), 1)
    lower = r >= c
    upper = r <= c
    lower_f = lower.astype(BF16)
    upper_f = upper.astype(BF16)

    def split3(a):
        a1 = a.astype(BF16)
        r1 = a - a1.astype(F32)
        a2 = r1.astype(BF16)
        return a1, a2, (r1 - a2.astype(F32)).astype(BF16)
    a_row = -jnp.exp(alog_ref[...])
    a_col = -jnp.exp(alogt_ref[...])

    def one_direction(x_ref, dt_ref, dtt_ref, y_ref, base, forward):
        x = x_ref[...]
        dt = _softplus(dt_ref[...] + bias_ref[...])
        dtt = _softplus(dtt_ref[...] + biast_ref[...])
        tri_col = lower_f if forward else upper_f
        tri_row = upper_f if forward else lower_f
        cs = sum(jnp.dot(tri_col, part, preferred_element_type=F32) for part in split3(dt * a_row))
        cst = sum(jnp.dot(part, tri_row, preferred_element_type=F32) for part in split3(dtt * a_col))
        mask = lower if forward else upper
        end = L - 1 if forward else 0
        outs = []
        cb = []
        for g in range(SSD_GROUPS):
            bg = x[:, SSD_WIDTH + g * N:SSD_WIDTH + (g + 1) * N].astype(BF16)
            cg = x[:, SSD_WIDTH + (SSD_GROUPS + g) * N:SSD_WIDTH + (SSD_GROUPS + g + 1) * N].astype(BF16)
            cb.append((bg, cg, lax.dot_general(cg, bg, (((1,), (1,)), ((), ())),
                                               preferred_element_type=F32).astype(BF16)))
        for hd in range(H):
            j = base + hd
            bg, cg, cbg = cb[hd // (H // SSD_GROUPS)]
            col = cs[:, j:j + 1]
            rowv = cst[j:j + 1, :]
            total = cs[end:end + 1, j:j + 1]
            decay = jnp.where(mask, jnp.exp(jnp.minimum(col - rowv, 0.0).astype(BF16)), 0.0)
            xh = (x[:, hd * P:(hd + 1) * P] * dt[:, j:j + 1]).astype(BF16)
            y = jnp.dot(cbg * decay, xh, preferred_element_type=F32)
            st = state_ref[j]
            y = y + jnp.dot(cg, st.astype(BF16), preferred_element_type=F32) * jnp.exp(col)
            bw = (bg.astype(F32) * jnp.exp(total - col)).astype(BF16)
            state_ref[j] = st * jnp.exp(total) + lax.dot_general(
                bw, xh, (((0,), (0,)), ((), ())), preferred_element_type=F32)
            outs.append(y)
        y_all = jnp.concatenate(outs, axis=1)
        if forward:
            y_all = y_all + x[:, :SSD_WIDTH] * dskip_ref[...]
        y_ref[...] = y_all

    one_direction(xf_ref, dtf_ref, dttf_ref, yf_ref, 0, True)
    one_direction(xb_ref, dtb_ref, dttb_ref, yb_ref, H, False)


def _ssd(xc, dt, dtt, bias_row, bias_col, alog_row, alog_col, dskip_row, n_lat_chunks):
    t_rows, cols = xc.shape
    L = SSD_CHUNK
    nc = t_rows // L
    fwd = lambda j: jnp.where(j == 0, n_lat_chunks, j - 1)
    bwd = lambda j: jnp.where(j == 0, n_lat_chunks, n_lat_chunks - j)
    small = lambda a: pl.BlockSpec(a.shape, lambda j: (0,) * a.ndim)
    return pl.pallas_call(
        _ssd_kernel,
        grid=(nc,),
        in_specs=[pl.BlockSpec((L, cols), lambda j: (fwd(j), 0)),
                  pl.BlockSpec((L, cols), lambda j: (bwd(j), 0)),
                  pl.BlockSpec((L, LANES), lambda j: (fwd(j), 0)),
                  pl.BlockSpec((L, LANES), lambda j: (bwd(j), 0)),
                  pl.BlockSpec((2 * SUBLANES, L), lambda j: (0, fwd(j))),
                  pl.BlockSpec((2 * SUBLANES, L), lambda j: (0, bwd(j))),
                  small(bias_row), small(bias_col), small(alog_row), small(alog_col), small(dskip_row)],
        out_specs=[pl.BlockSpec((L, SSD_WIDTH), lambda j: (fwd(j), 0)),
                   pl.BlockSpec((L, SSD_WIDTH), lambda j: (bwd(j), 0))],
        out_shape=[jax.ShapeDtypeStruct((t_rows, SSD_WIDTH), F32)] * 2,
        scratch_shapes=[pltpu.VMEM((2 * SSD_HEADS, SSD_STATE, SSD_HEAD_DIM), F32)],
        compiler_params=_params("arbitrary"),
        name="ssd_scan",
    )(xc, xc, dt, dt, dtt, dtt, bias_row, bias_col, alog_row, alog_col, dskip_row)


def _mixout_kernel(x_ref, mod_ref, att_ref, yf_ref, yb_ref, z_ref, pprev_ref, pcur_ref, pnext_ref,
                   ssdg_ref, poolw_ref, pools_ref, wout_ref, g2_ref, rw_ref, rb_ref,
                   x1_ref, h2_ref, route_ref, cnt_ref, base_ref, *, n_lat_tiles, n_lat, n_ctx):
    i = pl.program_id(0)
    mod = mod_ref[0]
    z = z_ref[...]
    ssd = _rms((yf_ref[...] + yb_ref[...]) * (z / (1.0 + jnp.exp(-z)))) * ssdg_ref[...]
    ext = _with_halo(pprev_ref, pcur_ref, pnext_ref, i, n_lat_tiles)
    rows = ext.shape[0]
    tm = rows - 2 * HALO
    w2 = ext + pltpu.roll(ext, 1, axis=0)
    w4 = pltpu.roll(w2, 1, axis=0) + pltpu.roll(w2, rows - 1, axis=0)
    w8 = pltpu.roll(w4, 2, axis=0) + pltpu.roll(w4, rows - 2, axis=0)
    w16 = pltpu.roll(w8, 4, axis=0) + pltpu.roll(w8, rows - 4, axis=0)
    lane = lax.broadcasted_iota(jnp.int32, (tm, POOL_WIDTH), 1)
    grp = lane // POOL_GROUP
    sl = slice(HALO, rows - HALO)
    wsum = jnp.where(grp == 0, w2[sl], jnp.where(grp == 1, w4[sl], jnp.where(grp == 2, w8[sl], w16[sl])))
    is_ctx = i >= n_lat_tiles
    seg_len = jnp.where(is_ctx, n_ctx, n_lat)
    t = lax.broadcasted_iota(jnp.int32, (tm, POOL_WIDTH), 0) + jnp.where(is_ctx, i - n_lat_tiles, i) * tm
    half = jnp.left_shift(1, grp)
    lo = jnp.clip(t - half, 0, seg_len)
    hi = jnp.clip(t + half, 0, seg_len)
    p = wsum / (hi - lo).astype(F32) - pcur_ref[...]
    pool = jnp.dot(p.astype(BF16), poolw_ref[...], preferred_element_type=F32) * pools_ref[...]
    mix = (jnp.dot(att_ref[...], wout_ref[0:MLA_WIDTH], preferred_element_type=F32)
           + jnp.dot(ssd.astype(BF16), wout_ref[MLA_WIDTH:MLA_WIDTH + SSD_WIDTH], preferred_element_type=F32)
           + jnp.dot(pool.astype(BF16), wout_ref[MLA_WIDTH + SSD_WIDTH:], preferred_element_type=F32))
    x1 = x_ref[...] + mod[2:3] * mix
    x1_ref[...] = x1
    h2 = _rms(x1) * g2_ref[...] * (1.0 + mod[4:5]) + mod[3:4]
    h2b = h2.astype(BF16)
    for kk in range(TOP_K):
        h2_ref[kk] = h2b
    h2_lo = (h2 - h2b.astype(F32)).astype(BF16)
    logits = (jnp.dot(h2b, rw_ref[0], preferred_element_type=F32)
              + jnp.dot(h2_lo, rw_ref[0], preferred_element_type=F32)
              + jnp.dot(h2b, rw_ref[1], preferred_element_type=F32)) + rb_ref[...]

    @pl.when(i == 0)
    def _():
        base_ref[...] = jnp.zeros(base_ref.shape, F32)

    elane = lax.broadcasted_iota(jnp.int32, logits.shape, 1)
    lg = jnp.where(elane < N_EXPERTS, logits, -jnp.inf)
    rr = lax.broadcasted_iota(jnp.int32, (tm, tm), 0)
    cc = lax.broadcasted_iota(jnp.int32, (tm, tm), 1)
    earlier = (rr > cc).astype(BF16)
    offset = base_ref[...]
    tops, ids, ranks = [], [], []
    for kk in range(TOP_K):
        top = jnp.max(lg, axis=-1, keepdims=True)
        idx = jnp.min(jnp.where(lg == top, elane, LANES), axis=-1, keepdims=True)
        sel = elane == idx
        lg = jnp.where(sel, -jnp.inf, lg)
        onehot = sel.astype(BF16)
        before = jnp.dot(earlier, onehot, preferred_element_type=F32) + offset
        ranks.append(jnp.sum(jnp.where(sel, before, 0.0), axis=-1, keepdims=True))
        offset = offset + jnp.sum(sel.astype(F32), axis=0, keepdims=True)
        tops.append(top)
        ids.append(idx.astype(F32))
    base_ref[...] = offset
    cnt_ref[...] = jnp.broadcast_to(offset, cnt_ref.shape)
    exps = [jnp.exp(tp - tops[0]) for tp in tops]
    denom = exps[0] + exps[1] + exps[2] + exps[3]
    route = jnp.zeros(logits.shape, F32)
    for kk in range(TOP_K):
        route = jnp.where(elane == kk, ids[kk], route)
        route = jnp.where(elane == TOP_K + kk, exps[kk] / denom, route)
        route = jnp.where(elane == 2 * TOP_K + kk, ranks[kk], route)
    route_ref[...] = route


def _mixer_output(x, modtab, att, yf, yb, z, pool_in, ssdg, poolw_bd, pools, wout, g2, rw_p, rb_p,
                  n_lat_tiles, n_lat, n_ctx):
    t_rows, d = x.shape
    nt = t_rows // ROW_TILE
    row = lambda cols: pl.BlockSpec((ROW_TILE, cols), lambda i: (i, 0))
    full = lambda a: pl.BlockSpec(a.shape, lambda i: (0,) * a.ndim)
    return pl.pallas_call(
        functools.partial(_mixout_kernel, n_lat_tiles=n_lat_tiles, n_lat=n_lat, n_ctx=n_ctx),
        grid=(nt,),
        in_specs=[row(d), pl.BlockSpec((1, SUBLANES, d), lambda i: (i // n_lat_tiles, 0, 0)),
                  row(MLA_WIDTH), row(SSD_WIDTH), row(SSD_WIDTH), row(SSD_WIDTH)]
                 + _halo_specs(POOL_WIDTH, ROW_TILE, n_lat_tiles, nt)
                 + [full(ssdg), full(poolw_bd), full(pools), full(wout), full(g2), full(rw_p), full(rb_p)],
        out_specs=[row(d), pl.BlockSpec((TOP_K, ROW_TILE, d), lambda i: (0, i, 0)), row(LANES),
                   pl.BlockSpec((SUBLANES, LANES), lambda i: (0, 0))],
        out_shape=[jax.ShapeDtypeStruct((t_rows, d), F32), jax.ShapeDtypeStruct((TOP_K, t_rows, d), BF16),
                   jax.ShapeDtypeStruct((t_rows, LANES), F32), jax.ShapeDtypeStruct((SUBLANES, LANES), F32)],
        scratch_shapes=[pltpu.VMEM((1, LANES), F32)],
        compiler_params=_params("arbitrary"),
        name="mixer_output",
    )(x, modtab, att, yf, yb, z, pool_in, pool_in, pool_in, ssdg, poolw_bd, pools, wout, g2, rw_p, rb_p)


def _expert_kernel(be_ref, nu_ref, x_ref, wg_ref, bg_ref, wu_ref, bu_ref, wd_ref, bd_ref, o_ref,
                   wg_s, wu_s, wd_s):
    i = pl.program_id(0)

    @pl.when(jnp.logical_or(i == 0, be_ref[i] != be_ref[jnp.maximum(i - 1, 0)]))
    def _():
        wg_s[...] = wg_ref[0, 0].astype(BF16)
        wu_s[...] = wu_ref[0, 0].astype(BF16)
        wd_s[...] = wd_ref[0, 0].astype(BF16)

    @pl.when(i < nu_ref[0])
    def _():
        x = x_ref[...]
        g = jnp.minimum(jnp.dot(x, wg_s[...], preferred_element_type=F32) + bg_ref[0, 0], SWIGLU_LIMIT)
        u = jnp.clip(jnp.dot(x, wu_s[...], preferred_element_type=F32) + bu_ref[0, 0],
                     -SWIGLU_LIMIT, SWIGLU_LIMIT)
        a = g / (1.0 + jnp.exp(-SWIGLU_ALPHA * g)) * (u + 1.0)
        y = jnp.dot(a.astype(BF16), wd_s[...], preferred_element_type=F32) + bd_ref[0, 0]
        o_ref[...] = y.astype(o_ref.dtype)

    @pl.when(i >= nu_ref[0])
    def _():
        o_ref[...] = jnp.zeros(o_ref.shape, o_ref.dtype)


def _experts(layer, x_sorted, block_expert, n_used, wg, bg, wu, bu, wd, bd):
    n_rows, d = x_sorted.shape
    depth, ne, _, dff = wg.shape
    rowspec = lambda cols: pl.BlockSpec((MOE_TILE, cols), lambda i, be, nu: (i, 0))
    wspec = lambda a, b: pl.BlockSpec((1, 1, a, b), lambda i, be, nu: (layer, be[i], 0, 0))
    grid_spec = pltpu.PrefetchScalarGridSpec(
        num_scalar_prefetch=2,
        grid=(n_rows // MOE_TILE,),
        in_specs=[rowspec(d), wspec(d, dff), wspec(1, dff), wspec(d, dff), wspec(1, dff),
                  wspec(dff, d), wspec(1, d)],
        out_specs=rowspec(d),
        scratch_shapes=[pltpu.VMEM((d, dff), BF16), pltpu.VMEM((d, dff), BF16), pltpu.VMEM((dff, d), BF16)],
    )
    return pl.pallas_call(
        _expert_kernel,
        grid_spec=grid_spec,
        out_shape=jax.ShapeDtypeStruct((n_rows, d), BF16),
        compiler_params=_params("arbitrary"),
        name="moe_experts",
    )(block_expert, n_used, x_sorted, wg, bg.reshape(depth, ne, 1, dff), wu, bu.reshape(depth, ne, 1, dff),
      wd, bd.reshape(depth, ne, 1, d))


def _dispatch_plan(route, counts_f):
    t_rows = route.shape[0]
    n_assign = t_rows * TOP_K
    ids = route[:, 0:TOP_K].astype(jnp.int32)
    rank =route[:, 2 * TOP_K:3 * TOP_K].astype(jnp.int32)
    counts = counts_f[0, :N_EXPERTS].astype(jnp.int32)
    padded = (counts + MOE_TILE - 1) // MOE_TILE * MOE_TILE
    pad_end = jnp.cumsum(padded)
    pad_start = pad_end - padded
    experts = jnp.arange(N_EXPERTS, dtype=jnp.int32)
    first_row = jnp.sum(jnp.where(ids[:, :, None] == experts[None, None, :], pad_start[None, None, :], 0), axis=-1)
    dest = (first_row + rank).T.reshape(-1)
    n_rows = n_assign + N_EXPERTS * MOE_TILE
    n_blocks = n_rows // MOE_TILE
    block_expert = jnp.minimum(
        jnp.sum((jnp.arange(n_blocks, dtype=jnp.int32)[:, None] * MOE_TILE >= pad_end[None, :]).astype(jnp.int32),
                axis=1), N_EXPERTS - 1).astype(jnp.int32)
    n_pad = n_rows - n_assign
    pads = jnp.concatenate([padded - counts, n_rows - pad_end[-1:]])
    pads_end = jnp.cumsum(pads)
    pad_first = jnp.concatenate([pad_start + counts, pad_end[-1:]])
    j = jnp.arange(n_pad, dtype=jnp.int32)
    grp = jnp.sum((j[:, None] >= pads_end[None, :]).astype(jnp.int32), axis=1)
    hit = grp[:, None] == jnp.arange(N_EXPERTS + 1, dtype=jnp.int32)[None, :]
    pad_rows = j + jnp.sum(jnp.where(hit, (pad_first - (pads_end - pads))[None, :], 0), axis=1)
    keys = jnp.concatenate([dest, pad_rows]).astype(jnp.int32)
    vals = jnp.concatenate([jnp.arange(n_assign, dtype=jnp.int32), j])
    row_src = lax.sort((keys, vals), num_keys=1)[1]
    n_used = (pad_end[-1:] // MOE_TILE).astype(jnp.int32)
    return dest, row_src, block_expert, n_used


def _rope_tables(n_lat, n_ctx):
    rows = n_lat // GRID_W
    row = jnp.repeat(jnp.arange(rows), GRID_W).astype(F32)
    col = jnp.tile(jnp.arange(GRID_W), rows).astype(F32)
    n_freq = QK_ROPE // 4
    inv_freq = ROPE_THETA ** (-jnp.arange(n_freq, dtype=F32) / n_freq)
    ang = jnp.stack([row[:, None] * inv_freq, col[:, None] * inv_freq], axis=1)
    cos, sin = jnp.cos(ang), jnp.sin(ang)
    zero = jnp.zeros_like(sin)
    ones = lambda w: jnp.ones((n_lat, w), F32)
    zeros = lambda w: jnp.zeros((n_lat, w), F32)
    per_axis = lambda a, b: jnp.stack([a, b], axis=2).reshape(n_lat, QK_ROPE)
    tail = HEAD_PAD - QK_DIM
    c = jnp.concatenate([ones(QK_NOPE), per_axis(cos, cos), ones(tail)], axis=1)
    sa = jnp.concatenate([zeros(QK_NOPE), per_axis(zero, sin), zeros(tail)], axis=1)
    sb = jnp.concatenate([zeros(QK_NOPE), per_axis(-sin, zero), zeros(tail)], axis=1)
    pad = lambda a, v: jnp.concatenate([a, jnp.full((n_ctx, HEAD_PAD), v, F32)], axis=0)
    return pad(c, 1.0), pad(sa, 0.0), pad(sb, 0.0)


def _pad_cols(a, width):
    return jnp.pad(a, ((0, 0), (0, width - a.shape[1])))


def _split_bf16(a):
    hi = a.astype(BF16)
    return jnp.stack([hi, (a - hi.astype(F32)).astype(BF16)])


def _pack_layer(i, w_in, q_g, kv_g, w_uq, w_ukv, qn_g, kn_g, conv_w, conv_b, a_log, dt_bias, d_skip, ssd_g,
                pool_w, pool_scale, w_out, router_w, router_b):
    d = w_in.shape[1]
    pts = [0, Q_LORA, KV_LORA, QK_ROPE, SSD_WIDTH, SSD_CONV_DIM, 2 * SSD_HEADS, POOL_WIDTH]
    offs = [sum(pts[:j + 1]) for j in range(len(pts))]
    seg = lambda j: w_in[i][:, offs[j]:offs[j + 1]]
    zc = lambda w: jnp.zeros((d, w), F32)
    w_in_p = jnp.concatenate([
        seg(0), seg(1), zc(QK_NOPE), seg(2), zc(LANES - QK_DIM), seg(3), seg(4),
        seg(5), zc(LANES - 2 * SSD_HEADS), seg(6)], axis=1).astype(BF16)
    wuq = w_uq[i].reshape(Q_LORA, MLA_HEADS, QK_DIM)
    wuq_p = jnp.pad(wuq, ((0, 0), (0, 0), (0, HEAD_PAD - QK_DIM))).reshape(Q_LORA, MLA_HEADS * HEAD_PAD)
    wukv = w_ukv[i].reshape(KV_LORA, MLA_HEADS, QK_NOPE + V_HEAD)
    wuk_p = jnp.pad(wukv[:, :, :QK_NOPE], ((0, 0), (0, 0), (0, HEAD_PAD - QK_NOPE))).reshape(
        KV_LORA, MLA_HEADS * HEAD_PAD)
    wuv = jnp.pad(wukv[:, :, QK_NOPE:], ((0, 0), (0, 0), (0, HEAD_PAD - V_HEAD))).reshape(
        KV_LORA, MLA_HEADS * HEAD_PAD).T
    flat12 = lambda a: a[i].reshape(1, 2 * SSD_HEADS)
    poolw_bd = jax.scipy.linalg.block_diag(*[pool_w[i][g] for g in range(len(POOL_WINDOWS))])
    return dict(
        w_in_p=w_in_p, qg=q_g[i][None], kvg=kv_g[i][None],
        wuq_p=wuq_p.astype(BF16), wuk_p=wuk_p.astype(BF16), wuv=wuv.astype(BF16),
        qng_p=_pad_cols(qn_g[i][None], HEAD_PAD), kng_p=_pad_cols(kn_g[i][None], HEAD_PAD),
        conv_w=conv_w[i], conv_b=conv_b[i][None],
        bias_row=_pad_cols(flat12(dt_bias), LANES), bias_col=_pad_cols(flat12(dt_bias), 2 * SUBLANES).T,
        alog_row=_pad_cols(flat12(a_log), LANES), alog_col=_pad_cols(flat12(a_log), 2 * SUBLANES).T,
        dskip_row=jnp.repeat(d_skip[i], SSD_HEAD_DIM)[None],
        ssdg=ssd_g[i][None], poolw_bd=poolw_bd.astype(BF16), pools=pool_scale[i][None],
        wout=w_out[i].astype(BF16), rw_p=_split_bf16(_pad_cols(router_w[i], LANES)),
        rb_p=_pad_cols(router_b[i][None], LANES))


def kernel(x, c, ctx, c_ctx, norm1_g, norm2_g, mod_w, mod_b, w_in, mla_q_norm_g, mla_kv_norm_g, mla_w_uq, mla_w_ukv, mla_qn_g, mla_kn_g, ssd_conv_w, ssd_conv_b, ssd_a_log, ssd_dt_bias, ssd_d, ssd_norm_g, pool_w, pool_scale, w_out, router_w, router_b, moe_w_gate, moe_b_gate, moe_w_up, moe_b_up, moe_w_down, moe_b_down):
    batch, n_lat, d = x.shape
    n_ctx = ctx.shape[1]
    depth = mod_w.shape[0]
    assert batch == 1 and d == D_MODEL and n_ctx == SSD_CHUNK == ROW_TILE
    assert n_lat % ATT_TQ == 0 and n_lat % ATT_TK == 0 and n_lat % GRID_W == 0
    n_lat_tiles = n_lat // ROW_TILE

    cvec = jnp.zeros((SUBLANES, d), F32).at[0].set(c[0]).at[1].set(c_ctx)
    mods = _modulation(cvec, mod_w, mod_b)
    rope_c, rope_sa, rope_sb = _rope_tables(n_lat, n_ctx)

    xs = jnp.concatenate([x[0], ctx[0]], axis=0)
    moe = None
    for i in range(depth):
        lp = _pack_layer(i, w_in, mla_q_norm_g, mla_kv_norm_g, mla_w_uq, mla_w_ukv, mla_qn_g, mla_kn_g,
                         ssd_conv_w, ssd_conv_b, ssd_a_log, ssd_dt_bias, ssd_d, ssd_norm_g, pool_w, pool_scale,
                         w_out, router_w, router_b)
        modtab = jnp.pad(mods[i, :2].reshape(2, 6, d), ((0, 0), (0, SUBLANES - 6), (0, 0)))
        outs = _in_projection(
            xs, moe, modtab, norm1_g[i][None], lp['w_in_p'], lp['qg'], lp['kvg'], lp['wuq_p'], lp['wuk_p'], lp['wuv'],
            lp['qng_p'], lp['kng_p'], rope_c, rope_sa, rope_sb, n_lat_tiles)
        if moe is not None:
            xs, outs = outs[0], outs[1:]
        q, k, v, z, xbc, dt, pool_in = outs
        att = _attention(q, k, v, n_lat, n_ctx)
        xc = _conv(xbc, lp['conv_w'], lp['conv_b'], n_lat_tiles)
        dtt = dt[:, :2 * SUBLANES].T
        yf, yb = _ssd(xc, dt, dtt, lp['bias_row'], lp['bias_col'], lp['alog_row'], lp['alog_col'],
                      lp['dskip_row'], n_lat // SSD_CHUNK)
        x1, h2_slots, route, counts = _mixer_output(
            xs, modtab, att, yf, yb, z, pool_in, lp['ssdg'], lp['poolw_bd'], lp['pools'], lp['wout'],
            norm2_g[i][None], lp['rw_p'], lp['rb_p'], n_lat_tiles, n_lat, n_ctx)
        dest, row_src, block_expert, n_used = _dispatch_plan(route, counts)
        x_sorted = h2_slots.reshape(-1, d).at[row_src].get(mode='promise_in_bounds')
        out = _experts(i, x_sorted, block_expert, n_used,
                       moe_w_gate, moe_b_gate, moe_w_up, moe_b_up, moe_w_down, moe_b_down)
        picked = out.at[dest].get(mode='promise_in_bounds', unique_indices=True)
        xs, moe = x1, (picked.reshape(TOP_K, xs.shape[0], d), route, modtab)
    return _final_combine(xs, *moe, n_lat)[None]
```

```python
import functools

import jax
import jax.numpy as jnp
from jax import lax
from jax.experimental import pallas as pl
from jax.experimental.pallas import tpu as pltpu

F32 = jnp.float32
BF16 = jnp.bfloat16
HIGHEST = lax.Precision.HIGHEST

D_MODEL = 1024
GRID_W = 64
EPS = 1e-6
MLA_HEADS = 6
QK_NOPE = 64
QK_ROPE = 32
QK_DIM = QK_NOPE + QK_ROPE
V_HEAD = 64
Q_LORA = 256
KV_LORA = 128
MLA_WIDTH = MLA_HEADS * V_HEAD
ROPE_THETA = 10000.0
SSD_HEADS = 6
SSD_HEAD_DIM = 64
SSD_WIDTH = SSD_HEADS * SSD_HEAD_DIM
SSD_GROUPS = 2
SSD_STATE = 64
SSD_CONV = 4
SSD_CONV_DIM = SSD_WIDTH + 2 * SSD_GROUPS * SSD_STATE
POOL_WINDOWS = (2, 4, 8, 16)
POOL_GROUP = 64
POOL_WIDTH = len(POOL_WINDOWS) * POOL_GROUP
N_EXPERTS = 32
TOP_K = 4
D_FF = 1024
SWIGLU_LIMIT = 7.0
SWIGLU_ALPHA = 1.702

LANES = 128
SUBLANES = 8
HEAD_PAD = LANES
Q_SCALE = QK_DIM ** -0.5 * 1.4426950408889634

ROW_TILE = 256
SSD_CHUNK = 256
HALO = SUBLANES
ATT_TQ = 1024
ATT_TK = 512
MOE_TILE = 512
VMEM_LIMIT = 56 * 1024 * 1024

COL_QLAT = 0
COL_KVLAT = COL_QLAT + Q_LORA
COL_KROPE = COL_KVLAT + KV_LORA
COL_Z = COL_KROPE + LANES
COL_XBC = COL_Z + SSD_WIDTH
COL_DT = COL_XBC + SSD_CONV_DIM
COL_POOL = COL_DT + LANES
IN_COLS_PACKED = COL_POOL + POOL_WIDTH


def _row_sumsq(x):
    return jnp.dot((x * x).astype(BF16), jnp.ones((x.shape[1], LANES), BF16), preferred_element_type=F32)


def _rms(x):
    width = x.shape[1]
    inv = lax.rsqrt(_row_sumsq(x) * (1.0 / width) + EPS)
    return x * jnp.tile(inv, (1, width // LANES))


def _params(*sem):
    return pltpu.CompilerParams(dimension_semantics=sem, vmem_limit_bytes=VMEM_LIMIT)


def _mod_kernel(c_ref, w_ref, b_ref, o_ref):
    c = c_ref[...]
    s = c / (1.0 + jnp.exp(-c))
    o_ref[0] = jnp.dot(s, w_ref[0], precision=HIGHEST, preferred_element_type=F32) + b_ref[0]


def _modulation(cvec, mod_w, mod_b):
    depth, d, cols = mod_w.shape
    tn = 1536
    return pl.pallas_call(
        _mod_kernel,
        grid=(depth, cols // tn),
        in_specs=[pl.BlockSpec((SUBLANES, d), lambda l, j: (0, 0)),
                  pl.BlockSpec((1, d, tn), lambda l, j: (l, 0, j)),
                  pl.BlockSpec((1, 1, tn), lambda l, j: (l, 0, j))],
        out_specs=pl.BlockSpec((1, SUBLANES, tn), lambda l, j: (l, 0, j)),
        out_shape=jax.ShapeDtypeStruct((depth, SUBLANES, cols), F32),
        compiler_params=_params("arbitrary", "arbitrary"),
        name="modulation",
    )(cvec, mod_w, mod_b.reshape(depth, 1, cols))


def _moe_combine(x1_ref, picked_ref, route_ref, pmod_ref):
    route = route_ref[...]
    y = picked_ref[0].astype(F32) * route[:, TOP_K:TOP_K + 1]
    for kk in range(1, TOP_K):
        y = y + picked_ref[kk].astype(F32) * route[:, TOP_K + kk:TOP_K + kk + 1]
    return x1_ref[...] + pmod_ref[0][5:6] * y


def _combine_kernel(x1_ref, picked_ref, route_ref, pmod_ref, o_ref):
    o_ref[...] = _moe_combine(x1_ref, picked_ref, route_ref, pmod_ref)


def _final_combine(x1, picked, route, pmodtab, n_lat):
    d = x1.shape[1]
    row = lambda cols: pl.BlockSpec((ROW_TILE, cols), lambda i: (i, 0))
    return pl.pallas_call(
        _combine_kernel,
        grid=(n_lat // ROW_TILE,),
        in_specs=[row(d), pl.BlockSpec((TOP_K, ROW_TILE, d), lambda i: (0, i, 0)), row(LANES),
                  pl.BlockSpec((1, SUBLANES, d), lambda i: (0, 0, 0))],
        out_specs=row(d),
        out_shape=jax.ShapeDtypeStruct((n_lat, d), F32),
        compiler_params=_params("arbitrary"),
        name="moe_combine",
    )(x1, picked, route, pmodtab)


def _inproj_kernel(*refs, has_moe):
    if has_moe:
        x1_ref, picked_ref, route_ref, pmod_ref = refs[:4]
        refs = refs[4:]
    else:
        x_ref = refs[0]
        refs = refs[1:]
    (mod_ref, g1_ref, win_ref, qg_ref, kvg_ref, wuq_ref, wuk_ref, wuv_ref, qng_ref, kng_ref,
     rc_ref, rsa_ref, rsb_ref) = refs[:13]
    outs = refs[13:]
    if has_moe:
        xs_ref, outs = outs[0], outs[1:]
        x = _moe_combine(x1_ref, picked_ref, route_ref, pmod_ref)
        xs_ref[...] = x
    else:
        x = x_ref[...]
    q_ref, k_ref, v_ref, z_ref, xbc_ref, dt_ref, pool_ref = outs
    mod = mod_ref[0]
    shift, scale = mod[0:1], mod[1:2]
    h = _rms(x) * g1_ref[...] * (1.0 + scale) + shift
    proj = jnp.dot(h.astype(BF16), win_ref[...], preferred_element_type=F32)
    z_ref[...] = proj[:, COL_Z:COL_XBC]
    xbc_ref[...] = proj[:, COL_XBC:COL_DT]
    dt_ref[...] = proj[:, COL_DT:COL_POOL]
    pool_ref[...] = proj[:, COL_POOL:IN_COLS_PACKED]

    rc, rsa, rsb = rc_ref[...], rsa_ref[...], rsb_ref[...]

    def head_norm_rope(t, gain):
        ms = _row_sumsq(t) * (1.0 / QK_DIM)
        t = t * lax.rsqrt(ms + EPS) * gain
        return (t * rc + pltpu.roll(t, QK_ROPE // 4, axis=1) * rsa
                + pltpu.roll(t, HEAD_PAD - QK_ROPE // 4, axis=1) * rsb)

    qn = (_rms(proj[:, COL_QLAT:COL_KVLAT]) * qg_ref[...]).astype(BF16)
    q_all = jnp.dot(qn, wuq_ref[...], preferred_element_type=F32)
    kvn = (_rms(proj[:, COL_KVLAT:COL_KROPE]) * kvg_ref[...]).astype(BF16)
    k_all = jnp.dot(kvn, wuk_ref[...], preferred_element_type=F32)
    k_rope = proj[:, COL_KROPE:COL_Z]
    vt_all = lax.dot_general(wuv_ref[...], kvn, (((1,), (1,)), ((), ())), preferred_element_type=F32)
    vrow = lax.broadcasted_iota(jnp.int32, vt_all.shape, 0) % HEAD_PAD
    v_ref[0] = jnp.where(vrow == V_HEAD, 1.0, vt_all).astype(BF16)
    qng, kng = qng_ref[...], kng_ref[...]
    for hd in range(MLA_HEADS):
        sl = slice(hd * HEAD_PAD, (hd + 1) * HEAD_PAD)
        q_ref[:, sl] = (head_norm_rope(q_all[:, sl], qng) * Q_SCALE).astype(BF16)
        k_ref[:, sl] = head_norm_rope(k_all[:, sl] + k_rope, kng).astype(BF16)


def _in_projection(x, moe, modtab, g1, w_in_p, qg, kvg, wuq_p, wuk_p, wuv, qng_p, kng_p, rope_c, rope_sa, rope_sb,
                   n_lat_tiles):
    t_rows, d = x.shape
    nt = t_rows // ROW_TILE
    row = lambda cols: pl.BlockSpec((ROW_TILE, cols), lambda i: (i, 0))
    full = lambda a: pl.BlockSpec(a.shape, lambda i: (0,) * a.ndim)
    seg = lambda: pl.BlockSpec((1, SUBLANES, d), lambda i: (i // n_lat_tiles, 0, 0))
    hw = MLA_HEADS * HEAD_PAD
    per = ATT_TK // ROW_TILE
    out_specs = [row(hw), row(hw), pl.BlockSpec((1, hw, ROW_TILE), lambda i: (i // per, 0, i % per)),
                 row(SSD_WIDTH), row(SSD_CONV_DIM), row(LANES), row(POOL_WIDTH)]
    sds = lambda cols, dt: jax.ShapeDtypeStruct((t_rows, cols), dt)
    out_shape = [sds(hw, BF16), sds(hw, BF16), jax.ShapeDtypeStruct((pl.cdiv(nt, per), hw, ATT_TK), BF16),
                 sds(SSD_WIDTH, F32), sds(SSD_CONV_DIM, F32), sds(LANES, F32), sds(POOL_WIDTH, F32)]
    lead_in, lead_specs = [x], [row(d)]
    if moe is not None:
        picked, route, pmodtab = moe
        lead_in += [picked, route, pmodtab]
        lead_specs += [pl.BlockSpec((TOP_K, ROW_TILE, d), lambda i: (0, i, 0)), row(LANES), seg()]
        out_specs = [row(d)] + out_specs
        out_shape = [sds(d, F32)] + out_shape
    return pl.pallas_call(
        functools.partial(_inproj_kernel, has_moe=moe is not None),
        grid=(nt,),
        in_specs=lead_specs + [seg(), full(g1), full(w_in_p), full(qg), full(kvg), full(wuq_p), full(wuk_p),
                               full(wuv), full(qng_p), full(kng_p), row(HEAD_PAD), row(HEAD_PAD), row(HEAD_PAD)],
        out_specs=out_specs,
        out_shape=out_shape,
        compiler_params=_params("arbitrary"),
        name="in_projection",
    )(*lead_in, modtab, g1, w_in_p, qg, kvg, wuq_p, wuk_p, wuv, qng_p, kng_p, rope_c, rope_sa, rope_sb)


def _attention_kernel(q_ref, k_ref, vt_ref, o_ref, sa_ref, sb_ref, st_ref, ca_ref, cb_ref, ct_ref, m_ref, acc_ref,
                      *, n_main, tk, tail):
    m_ref[...] = jnp.full(m_ref.shape, -jnp.inf, F32)
    acc_ref[...] = jnp.zeros(acc_ref.shape, F32)

    def scores(buf, start, size):
        s_ref, cmax_ref = buf
        for hh in range(2):
            q = q_ref[:, hh * HEAD_PAD:(hh + 1) * HEAD_PAD]
            k = k_ref[pl.ds(start, size), hh * HEAD_PAD:(hh + 1) * HEAD_PAD]
            s = lax.dot_general(k, q, (((1,), (1,)), ((), ())), preferred_element_type=F32)
            s_ref[hh] = s
            cmax_ref[hh] = jnp.max(s, axis=0, keepdims=True)

    def consume(buf, c, size):
        s_ref, cmax_ref = buf
        for hh in range(2):
            s = s_ref[hh]
            m_old = m_ref[hh]
            m_new = jnp.maximum(m_old, cmax_ref[hh])
            p = jnp.exp2((s - m_new).astype(BF16))
            vt = vt_ref[c, hh * HEAD_PAD:(hh + 1) * HEAD_PAD, 0:size]
            acc_ref[hh] = jnp.exp2(m_old - m_new) * acc_ref[hh] + jnp.dot(vt, p, preferred_element_type=F32)
            m_ref[hh] = m_new

    row0 = lambda c: pl.multiple_of(c * tk, tk)
    t_start, t_size, t_chunk = tail
    sa_ref, sb_ref, st_ref = (sa_ref, ca_ref), (sb_ref, cb_ref), (st_ref, ct_ref)
    scores(st_ref, t_start, t_size)
    if n_main > 0:
        scores(sa_ref, 0, tk)
    consume(st_ref, t_chunk, t_size)
    if n_main > 0:
        def body(j, carry):
            scores(sb_ref, row0(2 * j + 1), tk)
            consume(sa_ref, 2 * j, tk)
            scores(sa_ref, row0(2 * j + 2), tk)
            consume(sb_ref, 2 * j + 1, tk)
            return carry
        lax.fori_loop(0, n_main // 2 - 1, body, 0)
        scores(sb_ref, (n_main - 1) * tk, tk)
        consume(sa_ref, n_main - 2, tk)
        consume(sb_ref, n_main - 1, tk)
    outs = []
    for hh in range(2):
        acc = acc_ref[hh]
        outs.append(acc[0:V_HEAD] / acc[V_HEAD:V_HEAD + 1])
    o_ref[...] = jnp.concatenate(outs, axis=0).T.astype(o_ref.dtype)


def _attention(q, k, vt, n_lat, n_ctx):
    t_rows = q.shape[0]
    pairs = MLA_HEADS // 2
    n_main = n_lat // ATT_TK
    n_chunks = vt.shape[0]
    assert n_main % 2 == 0 and n_main >= 2 and n_chunks == n_main + 1

    def scratch(tq, tk, t_size):
        return ([pltpu.VMEM((2, tk, tq), F32), pltpu.VMEM((2, tk, tq), F32), pltpu.VMEM((2, t_size, tq), F32)]
                + [pltpu.VMEM((2, 1, tq), F32)] * 4 + [pltpu.VMEM((2, HEAD_PAD, tq), F32)])

    lat = pl.pallas_call(
        functools.partial(_attention_kernel, n_main=n_main, tk=ATT_TK, tail=(n_lat, n_ctx, n_main)),
        grid=(pairs, n_lat // ATT_TQ),
        in_specs=[pl.BlockSpec((ATT_TQ, 2 * HEAD_PAD), lambda p, i: (i, p)),
                  pl.BlockSpec((t_rows, 2 * HEAD_PAD), lambda p, i: (0, p), pipeline_mode=pl.Buffered(1)),
                  pl.BlockSpec((n_chunks, 2 * HEAD_PAD, ATT_TK), lambda p, i: (0, p, 0),
                               pipeline_mode=pl.Buffered(1))],
        out_specs=pl.BlockSpec((ATT_TQ, 2 * V_HEAD), lambda p, i: (i, p)),
        out_shape=jax.ShapeDtypeStruct((n_lat, MLA_WIDTH), BF16),
        scratch_shapes=scratch(ATT_TQ, ATT_TK, n_ctx),
        compiler_params=_params("arbitrary", "arbitrary"),
        name="attention_latent",
    )(q, k, vt)
    cblk = n_lat // n_ctx
    cspec = lambda: pl.BlockSpec((n_ctx, 2 * HEAD_PAD), lambda p: (cblk, p))
    ctx = pl.pallas_call(
        functools.partial(_attention_kernel, n_main=0, tk=SUBLANES, tail=(0, n_ctx, 0)),
        grid=(pairs,),
        in_specs=[cspec(), cspec(), pl.BlockSpec((1, 2 * HEAD_PAD, ATT_TK), lambda p: (n_main, p, 0))],
        out_specs=pl.BlockSpec((n_ctx, 2 * V_HEAD), lambda p: (0, p)),
        out_shape=jax.ShapeDtypeStruct((n_ctx, MLA_WIDTH), BF16),
        scratch_shapes=scratch(n_ctx, SUBLANES, n_ctx),
        compiler_params=_params("arbitrary"),
        name="attention_context",
    )(q, k, vt)
    return jnp.concatenate([lat, ctx], axis=0)


def _halo_specs(cols, tile_rows, n_lat_tiles, n_tiles):
    per = tile_rows // HALO
    last = n_tiles * per - 1
    return [pl.BlockSpec((HALO, cols), lambda i: (jnp.maximum(i * per - 1, 0), 0)),
            pl.BlockSpec((tile_rows, cols), lambda i: (i, 0)),
            pl.BlockSpec((HALO, cols), lambda i: (jnp.minimum((i + 1) * per, last), 0))]


def _with_halo(prev_ref, cur_ref, next_ref, i, n_lat_tiles):
    has_prev = jnp.logical_and(i != 0, i != n_lat_tiles)
    has_next = jnp.logical_and(i != n_lat_tiles - 1, i != pl.num_programs(0) - 1)
    prev = jnp.where(has_prev, prev_ref[...], 0.0)
    nxt = jnp.where(has_next, next_ref[...], 0.0)
    return jnp.concatenate([prev, cur_ref[...], nxt], axis=0)


def _conv_kernel(prev_ref, cur_ref, next_ref, w_ref, b_ref, o_ref, *, n_lat_tiles):
    i = pl.program_id(0)
    ext = _with_halo(prev_ref, cur_ref, next_ref, i, n_lat_tiles)
    rows = ext.shape[0]
    w = w_ref[...]
    y = (pltpu.roll(ext, 2, axis=0) * w[0:1] + pltpu.roll(ext, 1, axis=0) * w[1:2]
         + ext * w[2:3] + pltpu.roll(ext, rows - 1, axis=0) * w[3:4])
    y = y[HALO:rows - HALO] + b_ref[...]
    o_ref[...] = y / (1.0 + jnp.exp(-y))


def _conv(xbc, conv_w, conv_b, n_lat_tiles):
    t_rows, cols = xbc.shape
    nt = t_rows // ROW_TILE
    return pl.pallas_call(
        functools.partial(_conv_kernel, n_lat_tiles=n_lat_tiles),
        grid=(nt,),
        in_specs=_halo_specs(cols, ROW_TILE, n_lat_tiles, nt) + [
            pl.BlockSpec(conv_w.shape, lambda i: (0, 0)), pl.BlockSpec(conv_b.shape, lambda i: (0, 0))],
        out_specs=pl.BlockSpec((ROW_TILE, cols), lambda i: (i, 0)),
        out_shape=jax.ShapeDtypeStruct((t_rows, cols), F32),
        compiler_params=_params("arbitrary"),
        name="ssd_conv",
    )(xbc, xbc, xbc, conv_w, conv_b)


def _softplus(v):
    return jnp.maximum(v, 0.0) + jnp.log(1.0 + jnp.exp(-jnp.abs(v)))


def _ssd_kernel(xf_ref, xb_ref, dtf_ref, dtb_ref, dttf_ref, dttb_ref, bias_ref, biast_ref, alog_ref, alogt_ref,
                dskip_ref, expand_ref, yf_ref, yb_ref, state_ref):
    L = xf_ref.shape[0]
    P, N, H = SSD_HEAD_DIM, SSD_STATE, SSD_HEADS

    @pl.when(pl.program_id(0) == 0)
    def _():
        state_ref[...] = jnp.zeros(state_ref.shape, F32)

    r = lax.broadcasted_iota(jnp.int32, (L, L), 0)
    c = lax.broadcasted_iota(jnp.int32, (L, L), 1)
    lower = r >= c
    upper = r <= c
    lower_f = lower.astype(BF16)
    upper_f = upper.astype(BF16)

    def split3(a):
        a1 = a.astype(BF16)
        r1 = a - a1.astype(F32)
        a2 = r1.astype(BF16)
        return a1, a2, (r1 - a2.astype(F32)).astype(BF16)
    a_row = -jnp.exp(alog_ref[...])
    a_col = -jnp.exp(alogt_ref[...])

    def one_direction(x_ref, dt_ref, dtt_ref, y_ref, base, forward):
        x = x_ref[...]
        dt = _softplus(dt_ref[...] + bias_ref[...])
        dtt = _softplus(dtt_ref[...] + biast_ref[...])
        tri_col = lower_f if forward else upper_f
        tri_row = upper_f if forward else lower_f
        cs = sum(jnp.dot(tri_col, part, preferred_element_type=F32) for part in split3(dt * a_row))
        cst = sum(jnp.dot(part, tri_row, preferred_element_type=F32) for part in split3(dtt * a_col))
        mask = lower if forward else upper
        end = L - 1 if forward else 0
        outs = []
        cb = []
        for g in range(SSD_GROUPS):
            bg = x[:, SSD_WIDTH + g * N:SSD_WIDTH + (g + 1) * N].astype(BF16)
            cg = x[:, SSD_WIDTH + (SSD_GROUPS + g) * N:SSD_WIDTH + (SSD_GROUPS + g + 1) * N].astype(BF16)
            cb.append((bg, cg, lax.dot_general(cg, bg, (((1,), (1,)), ((), ())),
                                               preferred_element_type=F32).astype(BF16)))
        spread = expand_ref[0 if forward else 1]

        def replicate(a):
            hi = a.astype(BF16)
            lo = (a - hi.astype(F32)).astype(BF16)
            return (jnp.dot(hi, spread, preferred_element_type=F32)
                    + jnp.dot(lo, spread, preferred_element_type=F32))

        cs_rep, dt_rep = replicate(cs), replicate(dt)
        for hd in range(H):
            j = base + hd
            bg, cg, cbg = cb[hd // (H // SSD_GROUPS)]
            col = cs_rep[:, hd * LANES:(hd + 1) * LANES]
            rowv = cst[j:j + 1, :]
            total = col[end:end + 1, :]
            gap = jnp.tile(col, (1, L // LANES)) - rowv
            decay = jnp.where(mask, jnp.exp(jnp.minimum(gap, 0.0).astype(BF16)), 0.0)
            xh = (x[:, hd * P:(hd + 1) * P] * dt_rep[:, hd * LANES:hd * LANES + P]).astype(BF16)
            y = jnp.dot(cbg * decay, xh, preferred_element_type=F32)
            st = state_ref[j]
            y = y + jnp.dot(cg, st.astype(BF16), preferred_element_type=F32) * jnp.exp(col[:, :P])
            bw = (bg.astype(F32) * jnp.exp(total[:, :N] - col[:, :N])).astype(BF16)
            state_ref[j] = st * jnp.exp(total[:, :P]) + lax.dot_general(
                bw, xh, (((0,), (0,)), ((), ())), preferred_element_type=F32)
            outs.append(y)
        y_all = jnp.concatenate(outs, axis=1)
        if forward:
            y_all = y_all + x[:, :SSD_WIDTH] * dskip_ref[...]
        y_ref[...] = y_all

    one_direction(xf_ref, dtf_ref, dttf_ref, yf_ref, 0, True)
    one_direction(xb_ref, dtb_ref, dttb_ref, yb_ref, H, False)


def _ssd(xc, dt, dtt, bias_row, bias_col, alog_row, alog_col, dskip_row, n_lat_chunks):
    t_rows, cols = xc.shape
    L = SSD_CHUNK
    nc = t_rows // L
    fwd = lambda j: jnp.where(j == 0, n_lat_chunks, j - 1)
    bwd = lambda j: jnp.where(j == 0, n_lat_chunks, n_lat_chunks - j)
    small = lambda a: pl.BlockSpec(a.shape, lambda j: (0,) * a.ndim)
    r = jnp.arange(LANES)[None, :, None] - jnp.arange(2)[:, None, None] * SSD_HEADS
    spread = (r == jnp.arange(SSD_HEADS * LANES)[None, None, :] // LANES).astype(BF16)
    return pl.pallas_call(
        _ssd_kernel,
        grid=(nc,),
        in_specs=[pl.BlockSpec((L, cols), lambda j: (fwd(j), 0)),
                  pl.BlockSpec((L, cols), lambda j: (bwd(j), 0)),
                  pl.BlockSpec((L, LANES), lambda j: (fwd(j), 0)),
                  pl.BlockSpec((L, LANES), lambda j: (bwd(j), 0)),
                  pl.BlockSpec((2 * SUBLANES, L), lambda j: (0, fwd(j))),
                  pl.BlockSpec((2 * SUBLANES, L), lambda j: (0, bwd(j))),
                  small(bias_row), small(bias_col), small(alog_row), small(alog_col), small(dskip_row),
                  small(spread)],
        out_specs=[pl.BlockSpec((L, SSD_WIDTH), lambda j: (fwd(j), 0)),
                   pl.BlockSpec((L, SSD_WIDTH), lambda j: (bwd(j), 0))],
        out_shape=[jax.ShapeDtypeStruct((t_rows, SSD_WIDTH), F32)] * 2,
        scratch_shapes=[pltpu.VMEM((2 * SSD_HEADS, SSD_STATE, SSD_HEAD_DIM), F32)],
        compiler_params=_params("arbitrary"),
        name="ssd_scan",
    )(xc, xc, dt, dt, dtt, dtt, bias_row, bias_col, alog_row, alog_col, dskip_row, spread)


def _mixout_kernel(x_ref, mod_ref, att_ref, yf_ref, yb_ref, z_ref, pprev_ref, pcur_ref, pnext_ref,
                   ssdg_ref, poolw_ref, pools_ref, wout_ref, g2_ref, rw_ref, rb_ref,
                   x1_ref, h2_ref, route_ref, cnt_ref, base_ref, *, n_lat_tiles, n_lat, n_ctx):
    i = pl.program_id(0)
    mod = mod_ref[0]
    z = z_ref[...]
    ssd = _rms((yf_ref[...] + yb_ref[...]) * (z / (1.0 + jnp.exp(-z)))) * ssdg_ref[...]
    ext = _with_halo(pprev_ref, pcur_ref, pnext_ref, i, n_lat_tiles)
    rows = ext.shape[0]
    tm = rows - 2 * HALO
    w2 = ext + pltpu.roll(ext, 1, axis=0)
    w4 = pltpu.roll(w2, 1, axis=0) + pltpu.roll(w2, rows - 1, axis=0)
    w8 = pltpu.roll(w4, 2, axis=0) + pltpu.roll(w4, rows - 2, axis=0)
    w16 = pltpu.roll(w8, 4, axis=0) + pltpu.roll(w8, rows - 4, axis=0)
    lane = lax.broadcasted_iota(jnp.int32, (tm, POOL_WIDTH), 1)
    grp = lane // POOL_GROUP
    sl = slice(HALO, rows - HALO)
    wsum = jnp.where(grp == 0, w2[sl], jnp.where(grp == 1, w4[sl], jnp.where(grp == 2, w8[sl], w16[sl])))
    is_ctx = i >= n_lat_tiles
    seg_len = jnp.where(is_ctx, n_ctx, n_lat)
    t = lax.broadcasted_iota(jnp.int32, (tm, POOL_WIDTH), 0) + jnp.where(is_ctx, i - n_lat_tiles, i) * tm
    half = jnp.left_shift(1, grp)
    lo = jnp.clip(t - half, 0, seg_len)
    hi = jnp.clip(t + half, 0, seg_len)
    p = wsum / (hi - lo).astype(F32) - pcur_ref[...]
    pool = jnp.dot(p.astype(BF16), poolw_ref[...], preferred_element_type=F32) * pools_ref[...]
    mix = (jnp.dot(att_ref[...], wout_ref[0:MLA_WIDTH], preferred_element_type=F32)
           + jnp.dot(ssd.astype(BF16), wout_ref[MLA_WIDTH:MLA_WIDTH + SSD_WIDTH], preferred_element_type=F32)
           + jnp.dot(pool.astype(BF16), wout_ref[MLA_WIDTH + SSD_WIDTH:], preferred_element_type=F32))
    x1 = x_ref[...] + mod[2:3] * mix
    x1_ref[...] = x1
    h2 = _rms(x1) * g2_ref[...] * (1.0 + mod[4:5]) + mod[3:4]
    h2b = h2.astype(BF16)
    for kk in range(TOP_K):
        h2_ref[kk] = h2b
    h2_lo = (h2 - h2b.astype(F32)).astype(BF16)
    logits = (jnp.dot(h2b, rw_ref[0], preferred_element_type=F32)
              + jnp.dot(h2_lo, rw_ref[0], preferred_element_type=F32)
              + jnp.dot(h2b, rw_ref[1], preferred_element_type=F32)) + rb_ref[...]

    @pl.when(i == 0)
    def _():
        base_ref[...] = jnp.zeros(base_ref.shape, F32)

    elane = lax.broadcasted_iota(jnp.int32, logits.shape, 1)
    lg = jnp.where(elane < N_EXPERTS, logits, -jnp.inf)
    rr = lax.broadcasted_iota(jnp.int32, (tm, tm), 0)
    cc = lax.broadcasted_iota(jnp.int32, (tm, tm), 1)
    earlier = (rr > cc).astype(BF16)
    offset = base_ref[...]
    tops, ids, ranks = [], [], []
    for kk in range(TOP_K):
        top = jnp.max(lg, axis=-1, keepdims=True)
        idx = jnp.min(jnp.where(lg == top, elane, LANES), axis=-1, keepdims=True)
        sel = elane == idx
        lg = jnp.where(sel, -jnp.inf, lg)
        onehot = sel.astype(BF16)
        before = jnp.dot(earlier, onehot, preferred_element_type=F32) + offset
        ranks.append(jnp.sum(jnp.where(sel, before, 0.0), axis=-1, keepdims=True))
        offset = offset + jnp.sum(sel.astype(F32), axis=0, keepdims=True)
        tops.append(top)
        ids.append(idx.astype(F32))
    base_ref[...] = offset
    cnt_ref[...] = jnp.broadcast_to(offset, cnt_ref.shape)
    exps = [jnp.exp(tp - tops[0]) for tp in tops]
    denom = exps[0] + exps[1] + exps[2] + exps[3]
    route = jnp.zeros(logits.shape, F32)
    for kk in range(TOP_K):
        route = jnp.where(elane == kk, ids[kk], route)
        route = jnp.where(elane == TOP_K + kk, exps[kk] / denom, route)
        route = jnp.where(elane == 2 * TOP_K + kk, ranks[kk], route)
    route_ref[...] = route


def _mixer_output(x, modtab, att, yf, yb, z, pool_in, ssdg, poolw_bd, pools, wout, g2, rw_p, rb_p,
                  n_lat_tiles, n_lat, n_ctx):
    t_rows, d = x.shape
    nt = t_rows // ROW_TILE
    row = lambda cols: pl.BlockSpec((ROW_TILE, cols), lambda i: (i, 0))
    full = lambda a: pl.BlockSpec(a.shape, lambda i: (0,) * a.ndim)
    return pl.pallas_call(
        functools.partial(_mixout_kernel, n_lat_tiles=n_lat_tiles, n_lat=n_lat, n_ctx=n_ctx),
        grid=(nt,),
        in_specs=[row(d), pl.BlockSpec((1, SUBLANES, d), lambda i: (i // n_lat_tiles, 0, 0)),
                  row(MLA_WIDTH), row(SSD_WIDTH), row(SSD_WIDTH), row(SSD_WIDTH)]
                 + _halo_specs(POOL_WIDTH, ROW_TILE, n_lat_tiles, nt)
                 + [full(ssdg), full(poolw_bd), full(pools), full(wout), full(g2), full(rw_p), full(rb_p)],
        out_specs=[row(d), pl.BlockSpec((TOP_K, ROW_TILE, d), lambda i: (0, i, 0)), row(LANES),
                   pl.BlockSpec((SUBLANES, LANES), lambda i: (0, 0))],
        out_shape=[jax.ShapeDtypeStruct((t_rows, d), F32), jax.ShapeDtypeStruct((TOP_K, t_rows, d), BF16),
                   jax.ShapeDtypeStruct((t_rows, LANES), F32), jax.ShapeDtypeStruct((SUBLANES, LANES), F32)],
        scratch_shapes=[pltpu.VMEM((1, LANES), F32)],
        compiler_params=_params("arbitrary"),
        name="mixer_output",
    )(x, modtab, att, yf, yb, z, pool_in, pool_in, pool_in, ssdg, poolw_bd, pools, wout, g2, rw_p, rb_p)


def _expert_kernel(be_ref, nu_ref, x_ref, wg_ref, bg_ref, wu_ref, bu_ref, wd_ref, bd_ref, o_ref,
                   wg_s, wu_s, wd_s):
    i = pl.program_id(0)

    @pl.when(jnp.logical_or(i == 0, be_ref[i] != be_ref[jnp.maximum(i - 1, 0)]))
    def _():
        wg_s[...] = wg_ref[0, 0].astype(BF16)
        wu_s[...] = wu_ref[0, 0].astype(BF16)
        wd_s[...] = wd_ref[0, 0].astype(BF16)

    @pl.when(i < nu_ref[0])
    def _():
        x = x_ref[...]
        g = jnp.minimum(jnp.dot(x, wg_s[...], preferred_element_type=F32) + bg_ref[0, 0], SWIGLU_LIMIT)
        u = jnp.clip(jnp.dot(x, wu_s[...], preferred_element_type=F32) + bu_ref[0, 0],
                     -SWIGLU_LIMIT, SWIGLU_LIMIT)
        a = g / (1.0 + jnp.exp(-SWIGLU_ALPHA * g)) * (u + 1.0)
        y = jnp.dot(a.astype(BF16), wd_s[...], preferred_element_type=F32) + bd_ref[0, 0]
        o_ref[...] = y.astype(o_ref.dtype)

    @pl.when(i >= nu_ref[0])
    def _():
        o_ref[...] = jnp.zeros(o_ref.shape, o_ref.dtype)


def _experts(layer, x_sorted, block_expert, n_used, wg, bg, wu, bu, wd, bd):
    n_rows, d = x_sorted.shape
    depth, ne, _, dff = wg.shape
    rowspec = lambda cols: pl.BlockSpec((MOE_TILE, cols), lambda i, be, nu: (i, 0))
    wspec = lambda a, b: pl.BlockSpec((1, 1, a, b), lambda i, be, nu: (layer, be[i], 0, 0))
    grid_spec = pltpu.PrefetchScalarGridSpec(
        num_scalar_prefetch=2,
        grid=(n_rows // MOE_TILE,),
        in_specs=[rowspec(d), wspec(d, dff), wspec(1, dff), wspec(d, dff), wspec(1, dff),
                  wspec(dff, d), wspec(1, d)],
        out_specs=rowspec(d),
        scratch_shapes=[pltpu.VMEM((d, dff), BF16), pltpu.VMEM((d, dff), BF16), pltpu.VMEM((dff, d), BF16)],
    )
    return pl.pallas_call(
        _expert_kernel,
        grid_spec=grid_spec,
        out_shape=jax.ShapeDtypeStruct((n_rows, d), BF16),
        compiler_params=_params("arbitrary"),
        name="moe_experts",
    )(block_expert, n_used, x_sorted, wg, bg.reshape(depth, ne, 1, dff), wu, bu.reshape(depth, ne, 1, dff),
      wd, bd.reshape(depth, ne, 1, d))


def _dispatch_plan(route, counts_f):
    t_rows = route.shape[0]
    n_assign = t_rows * TOP_K
    ids = route[:, 0:TOP_K].astype(jnp.int32)
    rank =route[:, 2 * TOP_K:3 * TOP_K].astype(jnp.int32)
    counts = counts_f[0, :N_EXPERTS].astype(jnp.int32)
    padded = (counts + MOE_TILE - 1) // MOE_TILE * MOE_TILE
    pad_end = jnp.cumsum(padded)
    pad_start = pad_end - padded
    experts = jnp.arange(N_EXPERTS, dtype=jnp.int32)
    first_row = jnp.sum(jnp.where(ids[:, :, None] == experts[None, None, :], pad_start[None, None, :], 0), axis=-1)
    dest = (first_row + rank).T.reshape(-1)
    n_rows = n_assign + N_EXPERTS * MOE_TILE
    n_blocks = n_rows // MOE_TILE
    block_expert = jnp.minimum(
        jnp.sum((jnp.arange(n_blocks, dtype=jnp.int32)[:, None] * MOE_TILE >= pad_end[None, :]).astype(jnp.int32),
                axis=1), N_EXPERTS - 1).astype(jnp.int32)
    n_pad = n_rows - n_assign
    pads = jnp.concatenate([padded - counts, n_rows - pad_end[-1:]])
    pads_end = jnp.cumsum(pads)
    pad_first = jnp.concatenate([pad_start + counts, pad_end[-1:]])
    j = jnp.arange(n_pad, dtype=jnp.int32)
    grp = jnp.sum((j[:, None] >= pads_end[None, :]).astype(jnp.int32), axis=1)
    hit = grp[:, None] == jnp.arange(N_EXPERTS + 1, dtype=jnp.int32)[None, :]
    pad_rows = j + jnp.sum(jnp.where(hit, (pad_first - (pads_end - pads))[None, :], 0), axis=1)
    keys = jnp.concatenate([dest, pad_rows]).astype(jnp.int32)
    vals = jnp.concatenate([jnp.arange(n_assign, dtype=jnp.int32), j])
    row_src = lax.sort((keys, vals), num_keys=1)[1]
    n_used = (pad_end[-1:] // MOE_TILE).astype(jnp.int32)
    return dest, row_src, block_expert, n_used


def _rope_tables(n_lat, n_ctx):
    rows = n_lat // GRID_W
    row = jnp.repeat(jnp.arange(rows), GRID_W).astype(F32)
    col = jnp.tile(jnp.arange(GRID_W), rows).astype(F32)
    n_freq = QK_ROPE // 4
    inv_freq = ROPE_THETA ** (-jnp.arange(n_freq, dtype=F32) / n_freq)
    ang = jnp.stack([row[:, None] * inv_freq, col[:, None] * inv_freq], axis=1)
    cos, sin = jnp.cos(ang), jnp.sin(ang)
    zero = jnp.zeros_like(sin)
    ones = lambda w: jnp.ones((n_lat, w), F32)
    zeros = lambda w: jnp.zeros((n_lat, w), F32)
    per_axis = lambda a, b: jnp.stack([a, b], axis=2).reshape(n_lat, QK_ROPE)
    tail = HEAD_PAD - QK_DIM
    c = jnp.concatenate([ones(QK_NOPE), per_axis(cos, cos), ones(tail)], axis=1)
    sa = jnp.concatenate([zeros(QK_NOPE), per_axis(zero, sin), zeros(tail)], axis=1)
    sb = jnp.concatenate([zeros(QK_NOPE), per_axis(-sin, zero), zeros(tail)], axis=1)
    pad = lambda a, v: jnp.concatenate([a, jnp.full((n_ctx, HEAD_PAD), v, F32)], axis=0)
    return pad(c, 1.0), pad(sa, 0.0), pad(sb, 0.0)


def _pad_cols(a, width):
    return jnp.pad(a, ((0, 0), (0, width - a.shape[1])))


def _split_bf16(a):
    hi = a.astype(BF16)
    return jnp.stack([hi, (a - hi.astype(F32)).astype(BF16)])


def _pack_layer(i, w_in, q_g, kv_g, w_uq, w_ukv, qn_g, kn_g, conv_w, conv_b, a_log, dt_bias, d_skip, ssd_g,
                pool_w, pool_scale, w_out, router_w, router_b):
    d = w_in.shape[1]
    pts = [0, Q_LORA, KV_LORA, QK_ROPE, SSD_WIDTH, SSD_CONV_DIM, 2 * SSD_HEADS, POOL_WIDTH]
    offs = [sum(pts[:j + 1]) for j in range(len(pts))]
    seg = lambda j: w_in[i][:, offs[j]:offs[j + 1]]
    zc = lambda w: jnp.zeros((d, w), F32)
    w_in_p = jnp.concatenate([
        seg(0), seg(1), zc(QK_NOPE), seg(2), zc(LANES - QK_DIM), seg(3), seg(4),
        seg(5), zc(LANES - 2 * SSD_HEADS), seg(6)], axis=1).astype(BF16)
    wuq = w_uq[i].reshape(Q_LORA, MLA_HEADS, QK_DIM)
    wuq_p = jnp.pad(wuq, ((0, 0), (0, 0), (0, HEAD_PAD - QK_DIM))).reshape(Q_LORA, MLA_HEADS * HEAD_PAD)
    wukv = w_ukv[i].reshape(KV_LORA, MLA_HEADS, QK_NOPE + V_HEAD)
    wuk_p = jnp.pad(wukv[:, :, :QK_NOPE], ((0, 0), (0, 0), (0, HEAD_PAD - QK_NOPE))).reshape(
        KV_LORA, MLA_HEADS * HEAD_PAD)
    wuv = jnp.pad(wukv[:, :, QK_NOPE:], ((0, 0), (0, 0), (0, HEAD_PAD - V_HEAD))).reshape(
        KV_LORA, MLA_HEADS * HEAD_PAD).T
    flat12 = lambda a: a[i].reshape(1, 2 * SSD_HEADS)
    poolw_bd = jax.scipy.linalg.block_diag(*[pool_w[i][g] for g in range(len(POOL_WINDOWS))])
    return dict(
        w_in_p=w_in_p, qg=q_g[i][None], kvg=kv_g[i][None],
        wuq_p=wuq_p.astype(BF16), wuk_p=wuk_p.astype(BF16), wuv=wuv.astype(BF16),
        qng_p=_pad_cols(qn_g[i][None], HEAD_PAD), kng_p=_pad_cols(kn_g[i][None], HEAD_PAD),
        conv_w=conv_w[i], conv_b=conv_b[i][None],
        bias_row=_pad_cols(flat12(dt_bias), LANES), bias_col=_pad_cols(flat12(dt_bias), 2 * SUBLANES).T,
        alog_row=_pad_cols(flat12(a_log), LANES), alog_col=_pad_cols(flat12(a_log), 2 * SUBLANES).T,
        dskip_row=jnp.repeat(d_skip[i], SSD_HEAD_DIM)[None],
        ssdg=ssd_g[i][None], poolw_bd=poolw_bd.astype(BF16), pools=pool_scale[i][None],
        wout=w_out[i].astype(BF16), rw_p=_split_bf16(_pad_cols(router_w[i], LANES)),
        rb_p=_pad_cols(router_b[i][None], LANES))


def kernel(x, c, ctx, c_ctx, norm1_g, norm2_g, mod_w, mod_b, w_in, mla_q_norm_g, mla_kv_norm_g, mla_w_uq, mla_w_ukv, mla_qn_g, mla_kn_g, ssd_conv_w, ssd_conv_b, ssd_a_log, ssd_dt_bias, ssd_d, ssd_norm_g, pool_w, pool_scale, w_out, router_w, router_b, moe_w_gate, moe_b_gate, moe_w_up, moe_b_up, moe_w_down, moe_b_down):
    batch, n_lat, d = x.shape
    n_ctx = ctx.shape[1]
    depth = mod_w.shape[0]
    assert batch == 1 and d == D_MODEL and n_ctx == SSD_CHUNK == ROW_TILE
    assert n_lat % ATT_TQ == 0 and n_lat % ATT_TK == 0 and n_lat % GRID_W == 0
    n_lat_tiles = n_lat // ROW_TILE

    cvec = jnp.zeros((SUBLANES, d), F32).at[0].set(c[0]).at[1].set(c_ctx)
    mods = _modulation(cvec, mod_w, mod_b)
    rope_c, rope_sa, rope_sb = _rope_tables(n_lat, n_ctx)

    xs = jnp.concatenate([x[0], ctx[0]], axis=0)
    moe = None
    for i in range(depth):
        lp = _pack_layer(i, w_in, mla_q_norm_g, mla_kv_norm_g, mla_w_uq, mla_w_ukv, mla_qn_g, mla_kn_g,
                         ssd_conv_w, ssd_conv_b, ssd_a_log, ssd_dt_bias, ssd_d, ssd_norm_g, pool_w, pool_scale,
                         w_out, router_w, router_b)
        modtab = jnp.pad(mods[i, :2].reshape(2, 6, d), ((0, 0), (0, SUBLANES - 6), (0, 0)))
        outs = _in_projection(
            xs, moe, modtab, norm1_g[i][None], lp['w_in_p'], lp['qg'], lp['kvg'], lp['wuq_p'], lp['wuk_p'], lp['wuv'],
            lp['qng_p'], lp['kng_p'], rope_c, rope_sa, rope_sb, n_lat_tiles)
        if moe is not None:
            xs, outs = outs[0], outs[1:]
        q, k, v, z, xbc, dt, pool_in = outs
        att = _attention(q, k, v, n_lat, n_ctx)
        xc = _conv(xbc, lp['conv_w'], lp['conv_b'], n_lat_tiles)
        dtt = dt[:, :2 * SUBLANES].T
        yf, yb = _ssd(xc, dt, dtt, lp['bias_row'], lp['bias_col'], lp['alog_row'], lp['alog_col'],
                      lp['dskip_row'], n_lat // SSD_CHUNK)
        x1, h2_slots, route, counts = _mixer_output(
            xs, modtab, att, yf, yb, z, pool_in, lp['ssdg'], lp['poolw_bd'], lp['pools'], lp['wout'],
            norm2_g[i][None], lp['rw_p'], lp['rb_p'], n_lat_tiles, n_lat, n_ctx)
        dest, row_src, block_expert, n_used = _dispatch_plan(route, counts)
        x_sorted = h2_slots.reshape(-1, d).at[row_src].get(mode='promise_in_bounds')
        out = _experts(i, x_sorted, block_expert, n_used,
                       moe_w_gate, moe_b_gate, moe_w_up, moe_b_up, moe_w_down, moe_b_down)
        picked = out.at[dest].get(mode='promise_in_bounds', unique_indices=True)
        xs, moe = x1, (picked.reshape(TOP_K, xs.shape[0], d), route, modtab)
    return _final_combine(xs, *moe, n_lat)[None]
```

```python
import functools

import jax
import jax.numpy as jnp
from jax import lax
from jax.experimental import pallas as pl
from jax.experimental.pallas import tpu as pltpu

F32 = jnp.float32
BF16 = jnp.bfloat16
HIGHEST = lax.Precision.HIGHEST

D_MODEL = 1024
GRID_W = 64
EPS = 1e-6
MLA_HEADS = 6
QK_NOPE = 64
QK_ROPE = 32
QK_DIM = QK_NOPE + QK_ROPE
V_HEAD = 64
Q_LORA = 256
KV_LORA = 128
MLA_WIDTH = MLA_HEADS * V_HEAD
ROPE_THETA = 10000.0
SSD_HEADS = 6
SSD_HEAD_DIM = 64
SSD_WIDTH = SSD_HEADS * SSD_HEAD_DIM
SSD_GROUPS = 2
SSD_STATE = 64
SSD_CONV = 4
SSD_CONV_DIM = SSD_WIDTH + 2 * SSD_GROUPS * SSD_STATE
POOL_WINDOWS = (2, 4, 8, 16)
POOL_GROUP = 64
POOL_WIDTH = len(POOL_WINDOWS) * POOL_GROUP
N_EXPERTS = 32
TOP_K = 4
D_FF = 1024
SWIGLU_LIMIT = 7.0
SWIGLU_ALPHA = 1.702

LANES = 128
SUBLANES = 8
HEAD_PAD = LANES
V_ROWS = V_HEAD + 16
Q_SCALE = QK_DIM ** -0.5 * 1.4426950408889634

ROW_TILE = 256
SSD_CHUNK = 256
HALO = SUBLANES
ATT_TQ = 1024
ATT_TK = 512
MOE_TILE = 512
VMEM_LIMIT = 56 * 1024 * 1024

COL_QLAT = 0
COL_KVLAT = COL_QLAT + Q_LORA
COL_KROPE = COL_KVLAT + KV_LORA
COL_Z = COL_KROPE + LANES
COL_XBC = COL_Z + SSD_WIDTH
COL_DT = COL_XBC + SSD_CONV_DIM
COL_POOL = COL_DT + LANES
IN_COLS_PACKED = COL_POOL + POOL_WIDTH


def _row_sumsq(x):
    return jnp.dot((x * x).astype(BF16), jnp.ones((x.shape[1], LANES), BF16), preferred_element_type=F32)


def _rms(x):
    width = x.shape[1]
    inv = lax.rsqrt(_row_sumsq(x) * (1.0 / width) + EPS)
    return x * jnp.tile(inv, (1, width // LANES))


def _params(*sem):
    return pltpu.CompilerParams(dimension_semantics=sem, vmem_limit_bytes=VMEM_LIMIT)


def _mod_kernel(c_ref, w_ref, b_ref, o_ref):
    c = c_ref[...]
    s = c / (1.0 + jnp.exp(-c))
    o_ref[0] = jnp.dot(s, w_ref[0], precision=HIGHEST, preferred_element_type=F32) + b_ref[0]


def _modulation(cvec, mod_w, mod_b):
    depth, d, cols = mod_w.shape
    tn = 1536
    return pl.pallas_call(
        _mod_kernel,
        grid=(depth, cols // tn),
        in_specs=[pl.BlockSpec((SUBLANES, d), lambda l, j: (0, 0)),
                  pl.BlockSpec((1, d, tn), lambda l, j: (l, 0, j)),
                  pl.BlockSpec((1, 1, tn), lambda l, j: (l, 0, j))],
        out_specs=pl.BlockSpec((1, SUBLANES, tn), lambda l, j: (l, 0, j)),
        out_shape=jax.ShapeDtypeStruct((depth, SUBLANES, cols), F32),
        compiler_params=_params("arbitrary", "arbitrary"),
        name="modulation",
    )(cvec, mod_w, mod_b.reshape(depth, 1, cols))


def _moe_combine(x1_ref, picked_ref, route_ref, pmod_ref):
    route = route_ref[...]
    y = picked_ref[0].astype(F32) * route[:, TOP_K:TOP_K + 1]
    for kk in range(1, TOP_K):
        y = y + picked_ref[kk].astype(F32) * route[:, TOP_K + kk:TOP_K + kk + 1]
    return x1_ref[...] + pmod_ref[0][5:6] * y


def _combine_kernel(x1_ref, picked_ref, route_ref, pmod_ref, o_ref):
    o_ref[...] = _moe_combine(x1_ref, picked_ref, route_ref, pmod_ref)


def _final_combine(x1, picked, route, pmodtab, n_lat):
    d = x1.shape[1]
    row = lambda cols: pl.BlockSpec((ROW_TILE, cols), lambda i: (i, 0))
    return pl.pallas_call(
        _combine_kernel,
        grid=(n_lat // ROW_TILE,),
        in_specs=[row(d), pl.BlockSpec((TOP_K, ROW_TILE, d), lambda i: (0, i, 0)), row(LANES),
                  pl.BlockSpec((1, SUBLANES, d), lambda i: (0, 0, 0))],
        out_specs=row(d),
        out_shape=jax.ShapeDtypeStruct((n_lat, d), F32),
        compiler_params=_params("arbitrary"),
        name="moe_combine",
    )(x1, picked, route, pmodtab)


def _inproj_kernel(*refs, has_moe):
    if has_moe:
        x1_ref, picked_ref, route_ref, pmod_ref = refs[:4]
        refs = refs[4:]
    else:
        x_ref = refs[0]
        refs = refs[1:]
    (mod_ref, g1_ref, win_ref, qg_ref, kvg_ref, wuq_ref, wuk_ref, wuv_ref, qng_ref, kng_ref,
     rc_ref, rsa_ref, rsb_ref) = refs[:13]
    outs = refs[13:]
    if has_moe:
        xs_ref, outs = outs[0], outs[1:]
        x = _moe_combine(x1_ref, picked_ref, route_ref, pmod_ref)
        xs_ref[...] = x
    else:
        x = x_ref[...]
    q_ref, k_ref, v_ref, z_ref, xbc_ref, dt_ref, pool_ref = outs
    mod = mod_ref[0]
    shift, scale = mod[0:1], mod[1:2]
    h = _rms(x) * g1_ref[...] * (1.0 + scale) + shift
    proj = jnp.dot(h.astype(BF16), win_ref[...], preferred_element_type=F32)
    z_ref[...] = proj[:, COL_Z:COL_XBC]
    xbc_ref[...] = proj[:, COL_XBC:COL_DT]
    dt_ref[...] = proj[:, COL_DT:COL_POOL]
    pool_ref[...] = proj[:, COL_POOL:IN_COLS_PACKED]

    rc, rsa, rsb = rc_ref[...], rsa_ref[...], rsb_ref[...]

    def head_norm_rope(t, gain):
        ms = _row_sumsq(t) * (1.0 / QK_DIM)
        t = t * lax.rsqrt(ms + EPS) * gain
        return (t * rc + pltpu.roll(t, QK_ROPE // 4, axis=1) * rsa
                + pltpu.roll(t, HEAD_PAD - QK_ROPE // 4, axis=1) * rsb)

    qn = (_rms(proj[:, COL_QLAT:COL_KVLAT]) * qg_ref[...]).astype(BF16)
    q_all = jnp.dot(qn, wuq_ref[...], preferred_element_type=F32)
    kvn = (_rms(proj[:, COL_KVLAT:COL_KROPE]) * kvg_ref[...]).astype(BF16)
    k_all = jnp.dot(kvn, wuk_ref[...], preferred_element_type=F32)
    k_rope = proj[:, COL_KROPE:COL_Z]
    vt_all = lax.dot_general(wuv_ref[...], kvn, (((1,), (1,)), ((), ())), preferred_element_type=F32)
    vrow = lax.broadcasted_iota(jnp.int32, vt_all.shape, 0) % V_ROWS
    v_ref[0] = jnp.where(vrow == V_HEAD, 1.0, vt_all).astype(BF16)
    qng, kng = qng_ref[...], kng_ref[...]
    for hd in range(MLA_HEADS):
        sl = slice(hd * HEAD_PAD, (hd + 1) * HEAD_PAD)
        q_ref[:, sl] = (head_norm_rope(q_all[:, sl], qng) * Q_SCALE).astype(BF16)
        k_ref[:, sl] = head_norm_rope(k_all[:, sl] + k_rope, kng).astype(BF16)


def _in_projection(x, moe, modtab, g1, w_in_p, qg, kvg, wuq_p, wuk_p, wuv, qng_p, kng_p, rope_c, rope_sa, rope_sb,
                   n_lat_tiles):
    t_rows, d = x.shape
    nt = t_rows // ROW_TILE
    row = lambda cols: pl.BlockSpec((ROW_TILE, cols), lambda i: (i, 0))
    full = lambda a: pl.BlockSpec(a.shape, lambda i: (0,) * a.ndim)
    seg = lambda: pl.BlockSpec((1, SUBLANES, d), lambda i: (i // n_lat_tiles, 0, 0))
    hw = MLA_HEADS * HEAD_PAD
    per = ATT_TK // ROW_TILE
    vw = MLA_HEADS * V_ROWS
    out_specs = [row(hw), row(hw), pl.BlockSpec((1, vw, ROW_TILE), lambda i: (i // per, 0, i % per)),
                 row(SSD_WIDTH), row(SSD_CONV_DIM), row(LANES), row(POOL_WIDTH)]
    sds = lambda cols, dt: jax.ShapeDtypeStruct((t_rows, cols), dt)
    out_shape = [sds(hw, BF16), sds(hw, BF16), jax.ShapeDtypeStruct((pl.cdiv(nt, per), vw, ATT_TK), BF16),
                 sds(SSD_WIDTH, F32), sds(SSD_CONV_DIM, F32), sds(LANES, F32), sds(POOL_WIDTH, F32)]
    lead_in, lead_specs = [x], [row(d)]
    if moe is not None:
        picked, route, pmodtab = moe
        lead_in += [picked, route, pmodtab]
        lead_specs += [pl.BlockSpec((TOP_K, ROW_TILE, d), lambda i: (0, i, 0)), row(LANES), seg()]
        out_specs = [row(d)] + out_specs
        out_shape = [sds(d, F32)] + out_shape
    return pl.pallas_call(
        functools.partial(_inproj_kernel, has_moe=moe is not None),
        grid=(nt,),
        in_specs=lead_specs + [seg(), full(g1), full(w_in_p), full(qg), full(kvg), full(wuq_p), full(wuk_p),
                               full(wuv), full(qng_p), full(kng_p), row(HEAD_PAD), row(HEAD_PAD), row(HEAD_PAD)],
        out_specs=out_specs,
        out_shape=out_shape,
        compiler_params=_params("arbitrary"),
        name="in_projection",
    )(*lead_in, modtab, g1, w_in_p, qg, kvg, wuq_p, wuk_p, wuv, qng_p, kng_p, rope_c, rope_sa, rope_sb)


def _attention_kernel(q_ref, k_ref, vt_ref, o_ref, sa_ref, sb_ref, st_ref, ca_ref, cb_ref, ct_ref, m_ref, acc_ref,
                      *, n_main, tk, tail):
    m_ref[...] = jnp.full(m_ref.shape, -jnp.inf, F32)
    acc_ref[...] = jnp.zeros(acc_ref.shape, F32)

    def scores(buf, start, size):
        s_ref, cmax_ref = buf
        for hh in range(2):
            q = q_ref[:, hh * HEAD_PAD:(hh + 1) * HEAD_PAD]
            k = k_ref[pl.ds(start, size), hh * HEAD_PAD:(hh + 1) * HEAD_PAD]
            s = lax.dot_general(k, q, (((1,), (1,)), ((), ())), preferred_element_type=F32)
            s_ref[hh] = s
            cmax_ref[hh] = jnp.max(s, axis=0, keepdims=True)

    def consume(buf, c, size):
        s_ref, cmax_ref = buf
        for hh in range(2):
            s = s_ref[hh]
            m_old = m_ref[hh]
            m_new = jnp.maximum(m_old, cmax_ref[hh])
            p = jnp.exp2((s - m_new).astype(BF16))
            vt = vt_ref[c, hh * V_ROWS:(hh + 1) * V_ROWS, 0:size]
            acc_ref[hh] = jnp.exp2(m_old - m_new) * acc_ref[hh] + jnp.dot(vt, p, preferred_element_type=F32)
            m_ref[hh] = m_new

    row0 = lambda c: pl.multiple_of(c * tk, tk)
    t_start, t_size, t_chunk = tail
    sa_ref, sb_ref, st_ref = (sa_ref, ca_ref), (sb_ref, cb_ref), (st_ref, ct_ref)
    scores(st_ref, t_start, t_size)
    if n_main > 0:
        scores(sa_ref, 0, tk)
    consume(st_ref, t_chunk, t_size)
    if n_main > 0:
        def body(j, carry):
            scores(sb_ref, row0(2 * j + 1), tk)
            consume(sa_ref, 2 * j, tk)
            scores(sa_ref, row0(2 * j + 2), tk)
            consume(sb_ref, 2 * j + 1, tk)
            return carry
        lax.fori_loop(0, n_main // 2 - 1, body, 0)
        scores(sb_ref, (n_main - 1) * tk, tk)
        consume(sa_ref, n_main - 2, tk)
        consume(sb_ref, n_main - 1, tk)
    outs = []
    for hh in range(2):
        acc = acc_ref[hh]
        outs.append(acc[0:V_HEAD] / acc[V_HEAD:V_HEAD + 1])
    o_ref[...] = jnp.concatenate(outs, axis=0).T.astype(o_ref.dtype)


def _attention(q, k, vt, n_lat, n_ctx):
    t_rows = q.shape[0]
    pairs = MLA_HEADS // 2
    n_main = n_lat // ATT_TK
    n_chunks = vt.shape[0]
    assert n_main % 2 == 0 and n_main >= 2 and n_chunks == n_main + 1

    def scratch(tq, tk, t_size):
        return ([pltpu.VMEM((2, tk, tq), F32), pltpu.VMEM((2, tk, tq), F32), pltpu.VMEM((2, t_size, tq), F32)]
                + [pltpu.VMEM((2, 1, tq), F32)] * 4 + [pltpu.VMEM((2, V_ROWS, tq), F32)])

    lat = pl.pallas_call(
        functools.partial(_attention_kernel, n_main=n_main, tk=ATT_TK, tail=(n_lat, n_ctx, n_main)),
        grid=(pairs, n_lat // ATT_TQ),
        in_specs=[pl.BlockSpec((ATT_TQ, 2 * HEAD_PAD), lambda p, i: (i, p)),
                  pl.BlockSpec((t_rows, 2 * HEAD_PAD), lambda p, i: (0, p), pipeline_mode=pl.Buffered(1)),
                  pl.BlockSpec((n_chunks, 2 * V_ROWS, ATT_TK), lambda p, i: (0, p, 0),
                               pipeline_mode=pl.Buffered(1))],
        out_specs=pl.BlockSpec((ATT_TQ, 2 * V_HEAD), lambda p, i: (i, p)),
        out_shape=jax.ShapeDtypeStruct((n_lat, MLA_WIDTH), BF16),
        scratch_shapes=scratch(ATT_TQ, ATT_TK, n_ctx),
        compiler_params=_params("arbitrary", "arbitrary"),
        name="attention_latent",
    )(q, k, vt)
    cblk = n_lat // n_ctx
    cspec = lambda: pl.BlockSpec((n_ctx, 2 * HEAD_PAD), lambda p: (cblk, p))
    ctx = pl.pallas_call(
        functools.partial(_attention_kernel, n_main=0, tk=SUBLANES, tail=(0, n_ctx, 0)),
        grid=(pairs,),
        in_specs=[cspec(), cspec(), pl.BlockSpec((1, 2 * V_ROWS, ATT_TK), lambda p: (n_main, p, 0))],
        out_specs=pl.BlockSpec((n_ctx, 2 * V_HEAD), lambda p: (0, p)),
        out_shape=jax.ShapeDtypeStruct((n_ctx, MLA_WIDTH), BF16),
        scratch_shapes=scratch(n_ctx, SUBLANES, n_ctx),
        compiler_params=_params("arbitrary"),
        name="attention_context",
    )(q, k, vt)
    return jnp.concatenate([lat, ctx], axis=0)


def _halo_specs(cols, tile_rows, n_lat_tiles, n_tiles):
    per = tile_rows // HALO
    last = n_tiles * per - 1
    return [pl.BlockSpec((HALO, cols), lambda i: (jnp.maximum(i * per - 1, 0), 0)),
            pl.BlockSpec((tile_rows, cols), lambda i: (i, 0)),
            pl.BlockSpec((HALO, cols), lambda i: (jnp.minimum((i + 1) * per, last), 0))]


def _with_halo(prev_ref, cur_ref, next_ref, i, n_lat_tiles):
    has_prev = jnp.logical_and(i != 0, i != n_lat_tiles)
    has_next = jnp.logical_and(i != n_lat_tiles - 1, i != pl.num_programs(0) - 1)
    prev = jnp.where(has_prev, prev_ref[...], 0.0)
    nxt = jnp.where(has_next, next_ref[...], 0.0)
    return jnp.concatenate([prev, cur_ref[...], nxt], axis=0)


def _conv_kernel(prev_ref, cur_ref, next_ref, w_ref, b_ref, o_ref, *, n_lat_tiles):
    i = pl.program_id(0)
    ext = _with_halo(prev_ref, cur_ref, next_ref, i, n_lat_tiles)
    rows = ext.shape[0]
    w = w_ref[...]
    y = (pltpu.roll(ext, 2, axis=0) * w[0:1] + pltpu.roll(ext, 1, axis=0) * w[1:2]
         + ext * w[2:3] + pltpu.roll(ext, rows - 1, axis=0) * w[3:4])
    y = y[HALO:rows - HALO] + b_ref[...]
    o_ref[...] = y / (1.0 + jnp.exp(-y))


def _conv(xbc, conv_w, conv_b, n_lat_tiles):
    t_rows, cols = xbc.shape
    nt = t_rows // ROW_TILE
    return pl.pallas_call(
        functools.partial(_conv_kernel, n_lat_tiles=n_lat_tiles),
        grid=(nt,),
        in_specs=_halo_specs(cols, ROW_TILE, n_lat_tiles, nt) + [
            pl.BlockSpec(conv_w.shape, lambda i: (0, 0)), pl.BlockSpec(conv_b.shape, lambda i: (0, 0))],
        out_specs=pl.BlockSpec((ROW_TILE, cols), lambda i: (i, 0)),
        out_shape=jax.ShapeDtypeStruct((t_rows, cols), F32),
        compiler_params=_params("arbitrary"),
        name="ssd_conv",
    )(xbc, xbc, xbc, conv_w, conv_b)


def _softplus(v):
    return jnp.maximum(v, 0.0) + jnp.log(1.0 + jnp.exp(-jnp.abs(v)))


def _ssd_kernel(xf_ref, xb_ref, dtf_ref, dtb_ref, dttf_ref, dttb_ref, bias_ref, biast_ref, alog_ref, alogt_ref,
                dskip_ref, expand_ref, yf_ref, yb_ref, state_ref):
    L = xf_ref.shape[0]
    P, N, H = SSD_HEAD_DIM, SSD_STATE, SSD_HEADS

    @pl.when(pl.program_id(0) == 0)
    def _():
        state_ref[...] = jnp.zeros(state_ref.shape, F32)

    r = lax.broadcasted_iota(jnp.int32, (L, L), 0)
    c = lax.broadcasted_iota(jnp.int32, (L, L), 1)
    lower = r >= c
    upper = r <= c
    lower_f = lower.astype(BF16)
    upper_f = upper.astype(BF16)

    def split3(a):
        a1 = a.astype(BF16)
        r1 = a - a1.astype(F32)
        a2 = r1.astype(BF16)
        return a1, a2, (r1 - a2.astype(F32)).astype(BF16)
    a_row = -jnp.exp(alog_ref[...])
    a_col = -jnp.exp(alogt_ref[...])

    def one_direction(x_ref, dt_ref, dtt_ref, y_ref, base, forward):
        x = x_ref[...]
        dt = _softplus(dt_ref[...] + bias_ref[...])
        dtt = _softplus(dtt_ref[...] + biast_ref[...])
        tri_col = lower_f if forward else upper_f
        tri_row = upper_f if forward else lower_f
        cs = sum(jnp.dot(tri_col, part, preferred_element_type=F32) for part in split3(dt * a_row))
        cst = sum(jnp.dot(part, tri_row, preferred_element_type=F32) for part in split3(dtt * a_col))
        mask = lower if forward else upper
        end = L - 1 if forward else 0
        outs = []
        cb = []
        for g in range(SSD_GROUPS):
            bg = x[:, SSD_WIDTH + g * N:SSD_WIDTH + (g + 1) * N].astype(BF16)
            cg = x[:, SSD_WIDTH + (SSD_GROUPS + g) * N:SSD_WIDTH + (SSD_GROUPS + g + 1) * N].astype(BF16)
            cb.append((bg, cg, lax.dot_general(cg, bg, (((1,), (1,)), ((), ())),
                                               preferred_element_type=F32).astype(BF16)))
        spread = expand_ref[0 if forward else 1]

        def replicate(a):
            hi = a.astype(BF16)
            lo = (a - hi.astype(F32)).astype(BF16)
            return (jnp.dot(hi, spread, preferred_element_type=F32)
                    + jnp.dot(lo, spread, preferred_element_type=F32))

        cs_rep, dt_rep = replicate(cs), replicate(dt)
        for hd in range(H):
            j = base + hd
            bg, cg, cbg = cb[hd // (H // SSD_GROUPS)]
            col = cs_rep[:, hd * LANES:(hd + 1) * LANES]
            rowv = cst[j:j + 1, :]
            total = col[end:end + 1, :]
            gap = jnp.tile(col, (1, L // LANES)) - rowv
            decay = jnp.where(mask, jnp.exp(jnp.minimum(gap, 0.0).astype(BF16)), 0.0)
            xh = (x[:, hd * P:(hd + 1) * P] * dt_rep[:, hd * LANES:hd * LANES + P]).astype(BF16)
            y = jnp.dot(cbg * decay, xh, preferred_element_type=F32)
            st = state_ref[j]
            y = y + jnp.dot(cg, st.astype(BF16), preferred_element_type=F32) * jnp.exp(col[:, :P])
            bw = (bg.astype(F32) * jnp.exp(total[:, :N] - col[:, :N])).astype(BF16)
            state_ref[j] = st * jnp.exp(total[:, :P]) + lax.dot_general(
                bw, xh, (((0,), (0,)), ((), ())), preferred_element_type=F32)
            outs.append(y)
        y_all = jnp.concatenate(outs, axis=1)
        if forward:
            y_all = y_all + x[:, :SSD_WIDTH] * dskip_ref[...]
        y_ref[...] = y_all

    one_direction(xf_ref, dtf_ref, dttf_ref, yf_ref, 0, True)
    one_direction(xb_ref, dtb_ref, dttb_ref, yb_ref, H, False)


def _ssd(xc, dt, dtt, bias_row, bias_col, alog_row, alog_col, dskip_row, n_lat_chunks):
    t_rows, cols = xc.shape
    L = SSD_CHUNK
    nc = t_rows // L
    fwd = lambda j: jnp.where(j == 0, n_lat_chunks, j - 1)
    bwd = lambda j: jnp.where(j == 0, n_lat_chunks, n_lat_chunks - j)
    small = lambda a: pl.BlockSpec(a.shape, lambda j: (0,) * a.ndim)
    r = jnp.arange(LANES)[None, :, None] - jnp.arange(2)[:, None, None] * SSD_HEADS
    spread = (r == jnp.arange(SSD_HEADS * LANES)[None, None, :] // LANES).astype(BF16)
    return pl.pallas_call(
        _ssd_kernel,
        grid=(nc,),
        in_specs=[pl.BlockSpec((L, cols), lambda j: (fwd(j), 0)),
                  pl.BlockSpec((L, cols), lambda j: (bwd(j), 0)),
                  pl.BlockSpec((L, LANES), lambda j: (fwd(j), 0)),
                  pl.BlockSpec((L, LANES), lambda j: (bwd(j), 0)),
                  pl.BlockSpec((2 * SUBLANES, L), lambda j: (0, fwd(j))),
                  pl.BlockSpec((2 * SUBLANES, L), lambda j: (0, bwd(j))),
                  small(bias_row), small(bias_col), small(alog_row), small(alog_col), small(dskip_row),
                  small(spread)],
        out_specs=[pl.BlockSpec((L, SSD_WIDTH), lambda j: (fwd(j), 0)),
                   pl.BlockSpec((L, SSD_WIDTH), lambda j: (bwd(j), 0))],
        out_shape=[jax.ShapeDtypeStruct((t_rows, SSD_WIDTH), F32)] * 2,
        scratch_shapes=[pltpu.VMEM((2 * SSD_HEADS, SSD_STATE, SSD_HEAD_DIM), F32)],
        compiler_params=_params("arbitrary"),
        name="ssd_scan",
    )(xc, xc, dt, dt, dtt, dtt, bias_row, bias_col, alog_row, alog_col, dskip_row, spread)


def _mixout_kernel(x_ref, mod_ref, att_ref, yf_ref, yb_ref, z_ref, pprev_ref, pcur_ref, pnext_ref,
                   ssdg_ref, poolw_ref, pools_ref, wout_ref, g2_ref, rw_ref, rb_ref,
                   x1_ref, h2_ref, route_ref, cnt_ref, base_ref, *, n_lat_tiles, n_lat, n_ctx):
    i = pl.program_id(0)
    mod = mod_ref[0]
    z = z_ref[...]
    ssd = _rms((yf_ref[...] + yb_ref[...]) * (z / (1.0 + jnp.exp(-z)))) * ssdg_ref[...]
    ext = _with_halo(pprev_ref, pcur_ref, pnext_ref, i, n_lat_tiles)
    rows = ext.shape[0]
    tm = rows - 2 * HALO
    w2 = ext + pltpu.roll(ext, 1, axis=0)
    w4 = pltpu.roll(w2, 1, axis=0) + pltpu.roll(w2, rows - 1, axis=0)
    w8 = pltpu.roll(w4, 2, axis=0) + pltpu.roll(w4, rows - 2, axis=0)
    w16 = pltpu.roll(w8, 4, axis=0) + pltpu.roll(w8, rows - 4, axis=0)
    lane = lax.broadcasted_iota(jnp.int32, (tm, POOL_WIDTH), 1)
    grp = lane // POOL_GROUP
    sl = slice(HALO, rows - HALO)
    wsum = jnp.where(grp == 0, w2[sl], jnp.where(grp == 1, w4[sl], jnp.where(grp == 2, w8[sl], w16[sl])))
    is_ctx = i >= n_lat_tiles
    seg_len = jnp.where(is_ctx, n_ctx, n_lat)
    t = lax.broadcasted_iota(jnp.int32, (tm, POOL_WIDTH), 0) + jnp.where(is_ctx, i - n_lat_tiles, i) * tm
    half = jnp.left_shift(1, grp)
    lo = jnp.clip(t - half, 0, seg_len)
    hi = jnp.clip(t + half, 0, seg_len)
    p = wsum / (hi - lo).astype(F32) - pcur_ref[...]
    pool = jnp.dot(p.astype(BF16), poolw_ref[...], preferred_element_type=F32) * pools_ref[...]
    mix = (jnp.dot(att_ref[...], wout_ref[0:MLA_WIDTH], preferred_element_type=F32)
           + jnp.dot(ssd.astype(BF16), wout_ref[MLA_WIDTH:MLA_WIDTH + SSD_WIDTH], preferred_element_type=F32)
           + jnp.dot(pool.astype(BF16), wout_ref[MLA_WIDTH + SSD_WIDTH:], preferred_element_type=F32))
    x1 = x_ref[...] + mod[2:3] * mix
    x1_ref[...] = x1
    h2 = _rms(x1) * g2_ref[...] * (1.0 + mod[4:5]) + mod[3:4]
    h2b = h2.astype(BF16)
    for kk in range(TOP_K):
        h2_ref[kk] = h2b
    h2_lo = (h2 - h2b.astype(F32)).astype(BF16)
    logits = (jnp.dot(h2b, rw_ref[0], preferred_element_type=F32)
              + jnp.dot(h2_lo, rw_ref[0], preferred_element_type=F32)
              + jnp.dot(h2b, rw_ref[1], preferred_element_type=F32)) + rb_ref[...]

    @pl.when(i == 0)
    def _():
        base_ref[...] = jnp.zeros(base_ref.shape, F32)

    elane = lax.broadcasted_iota(jnp.int32, logits.shape, 1)
    lg = jnp.where(elane < N_EXPERTS, logits, -jnp.inf)
    rr = lax.broadcasted_iota(jnp.int32, (tm, tm), 0)
    cc = lax.broadcasted_iota(jnp.int32, (tm, tm), 1)
    earlier = (rr > cc).astype(BF16)
    offset = base_ref[...]
    tops, ids, ranks = [], [], []
    for kk in range(TOP_K):
        top = jnp.max(lg, axis=-1, keepdims=True)
        idx = jnp.min(jnp.where(lg == top, elane, LANES), axis=-1, keepdims=True)
        sel = elane == idx
        lg = jnp.where(sel, -jnp.inf, lg)
        onehot = sel.astype(BF16)
        before = jnp.dot(earlier, onehot, preferred_element_type=F32) + offset
        ranks.append(jnp.sum(jnp.where(sel, before, 0.0), axis=-1, keepdims=True))
        offset = offset + jnp.sum(sel.astype(F32), axis=0, keepdims=True)
        tops.append(top)
        ids.append(idx.astype(F32))
    base_ref[...] = offset
    cnt_ref[...] = jnp.broadcast_to(offset, cnt_ref.shape)
    exps = [jnp.exp(tp - tops[0]) for tp in tops]
    denom = exps[0] + exps[1] + exps[2] + exps[3]
    route = jnp.zeros(logits.shape, F32)
    for kk in range(TOP_K):
        route = jnp.where(elane == kk, ids[kk], route)
        route = jnp.where(elane == TOP_K + kk, exps[kk] / denom, route)
        route = jnp.where(elane == 2 * TOP_K + kk, ranks[kk], route)
    route_ref[...] = route


def _mixer_output(x, modtab, att, yf, yb, z, pool_in, ssdg, poolw_bd, pools, wout, g2, rw_p, rb_p,
                  n_lat_tiles, n_lat, n_ctx):
    t_rows, d = x.shape
    nt = t_rows // ROW_TILE
    row = lambda cols: pl.BlockSpec((ROW_TILE, cols), lambda i: (i, 0))
    full = lambda a: pl.BlockSpec(a.shape, lambda i: (0,) * a.ndim)
    return pl.pallas_call(
        functools.partial(_mixout_kernel, n_lat_tiles=n_lat_tiles, n_lat=n_lat, n_ctx=n_ctx),
        grid=(nt,),
        in_specs=[row(d), pl.BlockSpec((1, SUBLANES, d), lambda i: (i // n_lat_tiles, 0, 0)),
                  row(MLA_WIDTH), row(SSD_WIDTH), row(SSD_WIDTH), row(SSD_WIDTH)]
                 + _halo_specs(POOL_WIDTH, ROW_TILE, n_lat_tiles, nt)
                 + [full(ssdg), full(poolw_bd), full(pools), full(wout), full(g2), full(rw_p), full(rb_p)],
        out_specs=[row(d), pl.BlockSpec((TOP_K, ROW_TILE, d), lambda i: (0, i, 0)), row(LANES),
                   pl.BlockSpec((SUBLANES, LANES), lambda i: (0, 0))],
        out_shape=[jax.ShapeDtypeStruct((t_rows, d), F32), jax.ShapeDtypeStruct((TOP_K, t_rows, d), BF16),
                   jax.ShapeDtypeStruct((t_rows, LANES), F32), jax.ShapeDtypeStruct((SUBLANES, LANES), F32)],
        scratch_shapes=[pltpu.VMEM((1, LANES), F32)],
        compiler_params=_params("arbitrary"),
        name="mixer_output",
    )(x, modtab, att, yf, yb, z, pool_in, pool_in, pool_in, ssdg, poolw_bd, pools, wout, g2, rw_p, rb_p)


def _expert_kernel(be_ref, nu_ref, x_ref, wg_ref, bg_ref, wu_ref, bu_ref, wd_ref, bd_ref, o_ref,
                   wg_s, wu_s, wd_s):
    i = pl.program_id(0)

    @pl.when(jnp.logical_or(i == 0, be_ref[i] != be_ref[jnp.maximum(i - 1, 0)]))
    def _():
        wg_s[...] = wg_ref[0, 0].astype(BF16)
        wu_s[...] = wu_ref[0, 0].astype(BF16)
        wd_s[...] = wd_ref[0, 0].astype(BF16)

    @pl.when(i < nu_ref[0])
    def _():
        x = x_ref[...]
        g = jnp.minimum(jnp.dot(x, wg_s[...], preferred_element_type=F32) + bg_ref[0, 0], SWIGLU_LIMIT)
        u = jnp.clip(jnp.dot(x, wu_s[...], preferred_element_type=F32) + bu_ref[0, 0],
                     -SWIGLU_LIMIT, SWIGLU_LIMIT)
        a = g / (1.0 + jnp.exp(-SWIGLU_ALPHA * g)) * (u + 1.0)
        y = jnp.dot(a.astype(BF16), wd_s[...], preferred_element_type=F32) + bd_ref[0, 0]
        o_ref[...] = y.astype(o_ref.dtype)

    @pl.when(i >= nu_ref[0])
    def _():
        o_ref[...] = jnp.zeros(o_ref.shape, o_ref.dtype)


def _experts(layer, x_sorted, block_expert, n_used, wg, bg, wu, bu, wd, bd):
    n_rows, d = x_sorted.shape
    depth, ne, _, dff = wg.shape
    rowspec = lambda cols: pl.BlockSpec((MOE_TILE, cols), lambda i, be, nu: (i, 0))
    wspec = lambda a, b: pl.BlockSpec((1, 1, a, b), lambda i, be, nu: (layer, be[i], 0, 0))
    grid_spec = pltpu.PrefetchScalarGridSpec(
        num_scalar_prefetch=2,
        grid=(n_rows // MOE_TILE,),
        in_specs=[rowspec(d), wspec(d, dff), wspec(1, dff), wspec(d, dff), wspec(1, dff),
                  wspec(dff, d), wspec(1, d)],
        out_specs=rowspec(d),
        scratch_shapes=[pltpu.VMEM((d, dff), BF16), pltpu.VMEM((d, dff), BF16), pltpu.VMEM((dff, d), BF16)],
    )
    return pl.pallas_call(
        _expert_kernel,
        grid_spec=grid_spec,
        out_shape=jax.ShapeDtypeStruct((n_rows, d), BF16),
        compiler_params=_params("arbitrary"),
        name="moe_experts",
    )(block_expert, n_used, x_sorted, wg, bg.reshape(depth, ne, 1, dff), wu, bu.reshape(depth, ne, 1, dff),
      wd, bd.reshape(depth, ne, 1, d))


def _dispatch_plan(route, counts_f):
    t_rows = route.shape[0]
    n_assign = t_rows * TOP_K
    ids = route[:, 0:TOP_K].astype(jnp.int32)
    rank =route[:, 2 * TOP_K:3 * TOP_K].astype(jnp.int32)
    counts = counts_f[0, :N_EXPERTS].astype(jnp.int32)
    padded = (counts + MOE_TILE - 1) // MOE_TILE * MOE_TILE
    pad_end = jnp.cumsum(padded)
    pad_start = pad_end - padded
    experts = jnp.arange(N_EXPERTS, dtype=jnp.int32)
    first_row = jnp.sum(jnp.where(ids[:, :, None] == experts[None, None, :], pad_start[None, None, :], 0), axis=-1)
    dest = (first_row + rank).T.reshape(-1)
    n_rows = n_assign + N_EXPERTS * MOE_TILE
    n_blocks = n_rows // MOE_TILE
    block_expert = jnp.minimum(
        jnp.sum((jnp.arange(n_blocks, dtype=jnp.int32)[:, None] * MOE_TILE >= pad_end[None, :]).astype(jnp.int32),
                axis=1), N_EXPERTS - 1).astype(jnp.int32)
    n_pad = n_rows - n_assign
    pads = jnp.concatenate([padded - counts, n_rows - pad_end[-1:]])
    pads_end = jnp.cumsum(pads)
    pad_first = jnp.concatenate([pad_start + counts, pad_end[-1:]])
    j = jnp.arange(n_pad, dtype=jnp.int32)
    grp = jnp.sum((j[:, None] >= pads_end[None, :]).astype(jnp.int32), axis=1)
    hit = grp[:, None] == jnp.arange(N_EXPERTS + 1, dtype=jnp.int32)[None, :]
    pad_rows = j + jnp.sum(jnp.where(hit, (pad_first - (pads_end - pads))[None, :], 0), axis=1)
    keys = jnp.concatenate([dest, pad_rows]).astype(jnp.int32)
    vals = jnp.concatenate([jnp.arange(n_assign, dtype=jnp.int32), j])
    row_src = lax.sort((keys, vals), num_keys=1)[1]
    n_used = (pad_end[-1:] // MOE_TILE).astype(jnp.int32)
    return dest, row_src, block_expert, n_used


def _rope_tables(n_lat, n_ctx):
    rows = n_lat // GRID_W
    row = jnp.repeat(jnp.arange(rows), GRID_W).astype(F32)
    col = jnp.tile(jnp.arange(GRID_W), rows).astype(F32)
    n_freq = QK_ROPE // 4
    inv_freq = ROPE_THETA ** (-jnp.arange(n_freq, dtype=F32) / n_freq)
    ang = jnp.stack([row[:, None] * inv_freq, col[:, None] * inv_freq], axis=1)
    cos, sin = jnp.cos(ang), jnp.sin(ang)
    zero = jnp.zeros_like(sin)
    ones = lambda w: jnp.ones((n_lat, w), F32)
    zeros = lambda w: jnp.zeros((n_lat, w), F32)
    per_axis = lambda a, b: jnp.stack([a, b], axis=2).reshape(n_lat, QK_ROPE)
    tail = HEAD_PAD - QK_DIM
    c = jnp.concatenate([ones(QK_NOPE), per_axis(cos, cos), ones(tail)], axis=1)
    sa = jnp.concatenate([zeros(QK_NOPE), per_axis(zero, sin), zeros(tail)], axis=1)
    sb = jnp.concatenate([zeros(QK_NOPE), per_axis(-sin, zero), zeros(tail)], axis=1)
    pad = lambda a, v: jnp.concatenate([a, jnp.full((n_ctx, HEAD_PAD), v, F32)], axis=0)
    return pad(c, 1.0), pad(sa, 0.0), pad(sb, 0.0)


def _pad_cols(a, width):
    return jnp.pad(a, ((0, 0), (0, width - a.shape[1])))


def _split_bf16(a):
    hi = a.astype(BF16)
    return jnp.stack([hi, (a - hi.astype(F32)).astype(BF16)])


def _pack_layer(i, w_in, q_g, kv_g, w_uq, w_ukv, qn_g, kn_g, conv_w, conv_b, a_log, dt_bias, d_skip, ssd_g,
                pool_w, pool_scale, w_out, router_w, router_b):
    d = w_in.shape[1]
    pts = [0, Q_LORA, KV_LORA, QK_ROPE, SSD_WIDTH, SSD_CONV_DIM, 2 * SSD_HEADS, POOL_WIDTH]
    offs = [sum(pts[:j + 1]) for j in range(len(pts))]
    seg = lambda j: w_in[i][:, offs[j]:offs[j + 1]]
    zc = lambda w: jnp.zeros((d, w), F32)
    w_in_p = jnp.concatenate([
        seg(0), seg(1), zc(QK_NOPE), seg(2), zc(LANES - QK_DIM), seg(3), seg(4),
        seg(5), zc(LANES - 2 * SSD_HEADS), seg(6)], axis=1).astype(BF16)
    wuq = w_uq[i].reshape(Q_LORA, MLA_HEADS, QK_DIM)
    wuq_p = jnp.pad(wuq, ((0, 0), (0, 0), (0, HEAD_PAD - QK_DIM))).reshape(Q_LORA, MLA_HEADS * HEAD_PAD)
    wukv = w_ukv[i].reshape(KV_LORA, MLA_HEADS, QK_NOPE + V_HEAD)
    wuk_p = jnp.pad(wukv[:, :, :QK_NOPE], ((0, 0), (0, 0), (0, HEAD_PAD - QK_NOPE))).reshape(
        KV_LORA, MLA_HEADS * HEAD_PAD)
    wuv = jnp.pad(wukv[:, :, QK_NOPE:], ((0, 0), (0, 0), (0, V_ROWS - V_HEAD))).reshape(
        KV_LORA, MLA_HEADS * V_ROWS).T
    flat12 = lambda a: a[i].reshape(1, 2 * SSD_HEADS)
    poolw_bd = jax.scipy.linalg.block_diag(*[pool_w[i][g] for g in range(len(POOL_WINDOWS))])
    return dict(
        w_in_p=w_in_p, qg=q_g[i][None], kvg=kv_g[i][None],
        wuq_p=wuq_p.astype(BF16), wuk_p=wuk_p.astype(BF16), wuv=wuv.astype(BF16),
        qng_p=_pad_cols(qn_g[i][None], HEAD_PAD), kng_p=_pad_cols(kn_g[i][None], HEAD_PAD),
        conv_w=conv_w[i], conv_b=conv_b[i][None],
        bias_row=_pad_cols(flat12(dt_bias), LANES), bias_col=_pad_cols(flat12(dt_bias), 2 * SUBLANES).T,
        alog_row=_pad_cols(flat12(a_log), LANES), alog_col=_pad_cols(flat12(a_log), 2 * SUBLANES).T,
        dskip_row=jnp.repeat(d_skip[i], SSD_HEAD_DIM)[None],
        ssdg=ssd_g[i][None], poolw_bd=poolw_bd.astype(BF16), pools=pool_scale[i][None],
        wout=w_out[i].astype(BF16), rw_p=_split_bf16(_pad_cols(router_w[i], LANES)),
        rb_p=_pad_cols(router_b[i][None], LANES))


def kernel(x, c, ctx, c_ctx, norm1_g, norm2_g, mod_w, mod_b, w_in, mla_q_norm_g, mla_kv_norm_g, mla_w_uq, mla_w_ukv, mla_qn_g, mla_kn_g, ssd_conv_w, ssd_conv_b, ssd_a_log, ssd_dt_bias, ssd_d, ssd_norm_g, pool_w, pool_scale, w_out, router_w, router_b, moe_w_gate, moe_b_gate, moe_w_up, moe_b_up, moe_w_down, moe_b_down):
    batch, n_lat, d = x.shape
    n_ctx = ctx.shape[1]
    depth = mod_w.shape[0]
    assert batch == 1 and d == D_MODEL and n_ctx == SSD_CHUNK == ROW_TILE
    assert n_lat % ATT_TQ == 0 and n_lat % ATT_TK == 0 and n_lat % GRID_W == 0
    n_lat_tiles = n_lat // ROW_TILE

    cvec = jnp.zeros((SUBLANES, d), F32).at[0].set(c[0]).at[1].set(c_ctx)
    mods = _modulation(cvec, mod_w, mod_b)
    rope_c, rope_sa, rope_sb = _rope_tables(n_lat, n_ctx)

    xs = jnp.concatenate([x[0], ctx[0]], axis=0)
    moe = None
    for i in range(depth):
        lp = _pack_layer(i, w_in, mla_q_norm_g, mla_kv_norm_g, mla_w_uq, mla_w_ukv, mla_qn_g, mla_kn_g,
                         ssd_conv_w, ssd_conv_b, ssd_a_log, ssd_dt_bias, ssd_d, ssd_norm_g, pool_w, pool_scale,
                         w_out, router_w, router_b)
        modtab = jnp.pad(mods[i, :2].reshape(2, 6, d), ((0, 0), (0, SUBLANES - 6), (0, 0)))
        outs = _in_projection(
            xs, moe, modtab, norm1_g[i][None], lp['w_in_p'], lp['qg'], lp['kvg'], lp['wuq_p'], lp['wuk_p'], lp['wuv'],
            lp['qng_p'], lp['kng_p'], rope_c, rope_sa, rope_sb, n_lat_tiles)
        if moe is not None:
            xs, outs = outs[0], outs[1:]
        q, k, v, z, xbc, dt, pool_in = outs
        att = _attention(q, k, v, n_lat, n_ctx)
        xc = _conv(xbc, lp['conv_w'], lp['conv_b'], n_lat_tiles)
        dtt = dt[:, :2 * SUBLANES].T
        yf, yb = _ssd(xc, dt, dtt, lp['bias_row'], lp['bias_col'], lp['alog_row'], lp['alog_col'],
                      lp['dskip_row'], n_lat // SSD_CHUNK)
        x1, h2_slots, route, counts = _mixer_output(
            xs, modtab, att, yf, yb, z, pool_in, lp['ssdg'], lp['poolw_bd'], lp['pools'], lp['wout'],
            norm2_g[i][None], lp['rw_p'], lp['rb_p'], n_lat_tiles, n_lat, n_ctx)
        dest, row_src, block_expert, n_used = _dispatch_plan(route, counts)
        x_sorted = h2_slots.reshape(-1, d).at[row_src].get(mode='promise_in_bounds')
        out = _experts(i, x_sorted, block_expert, n_used,
                       moe_w_gate, moe_b_gate, moe_w_up, moe_b_up, moe_w_down, moe_b_down)
        picked = out.at[dest].get(mode='promise_in_bounds', unique_indices=True)
        xs, moe = x1, (picked.reshape(TOP_K, xs.shape[0], d), route, modtab)
    return _final_combine(xs, *moe, n_lat)[None]
```

```python
import functools

import jax
import jax.numpy as jnp
from jax import lax
from jax.experimental import pallas as pl
from jax.experimental.pallas import tpu as pltpu

F32 = jnp.float32
BF16 = jnp.bfloat16
HIGHEST = lax.Precision.HIGHEST

D_MODEL = 1024
GRID_W = 64
EPS = 1e-6
MLA_HEADS = 6
QK_NOPE = 64
QK_ROPE = 32
QK_DIM = QK_NOPE + QK_ROPE
V_HEAD = 64
Q_LORA = 256
KV_LORA = 128
MLA_WIDTH = MLA_HEADS * V_HEAD
ROPE_THETA = 10000.0
SSD_HEADS = 6
SSD_HEAD_DIM = 64
SSD_WIDTH = SSD_HEADS * SSD_HEAD_DIM
SSD_GROUPS = 2
SSD_STATE = 64
SSD_CONV = 4
SSD_CONV_DIM = SSD_WIDTH + 2 * SSD_GROUPS * SSD_STATE
POOL_WINDOWS = (2, 4, 8, 16)
POOL_GROUP = 64
POOL_WIDTH = len(POOL_WINDOWS) * POOL_GROUP
N_EXPERTS = 32
TOP_K = 4
D_FF = 1024
SWIGLU_LIMIT = 7.0
SWIGLU_ALPHA = 1.702

LANES = 128
SUBLANES = 8
HEAD_PAD = LANES
V_ROWS = HEAD_PAD
Q_SCALE = QK_DIM ** -0.5 * 1.4426950408889634

ROW_TILE = 512
SSD_CHUNK = 256
HALO = SUBLANES
ATT_TQ = 1024
ATT_TK = 512
MOE_TILE = 512
VMEM_LIMIT = 56 * 1024 * 1024

COL_QLAT = 0
COL_KVLAT = COL_QLAT + Q_LORA
COL_KROPE = COL_KVLAT + KV_LORA
COL_Z = COL_KROPE + LANES
COL_XBC = COL_Z + SSD_WIDTH
COL_DT = COL_XBC + SSD_CONV_DIM
COL_POOL = COL_DT + LANES
IN_COLS_PACKED = COL_POOL + POOL_WIDTH


def _row_sumsq(x):
    return jnp.dot((x * x).astype(BF16), jnp.ones((x.shape[1], LANES), BF16), preferred_element_type=F32)


def _rms(x):
    width = x.shape[1]
    inv = lax.rsqrt(_row_sumsq(x) * (1.0 / width) + EPS)
    return x * jnp.tile(inv, (1, width // LANES))


def _params(*sem):
    return pltpu.CompilerParams(dimension_semantics=sem, vmem_limit_bytes=VMEM_LIMIT)


def _mod_kernel(c_ref, w_ref, b_ref, o_ref):
    c = c_ref[...]
    s = c / (1.0 + jnp.exp(-c))
    o_ref[0] = jnp.dot(s, w_ref[0], precision=HIGHEST, preferred_element_type=F32) + b_ref[0]


def _modulation(cvec, mod_w, mod_b):
    depth, d, cols = mod_w.shape
    tn = 1536
    return pl.pallas_call(
        _mod_kernel,
        grid=(depth, cols // tn),
        in_specs=[pl.BlockSpec((SUBLANES, d), lambda l, j: (0, 0)),
                  pl.BlockSpec((1, d, tn), lambda l, j: (l, 0, j)),
                  pl.BlockSpec((1, 1, tn), lambda l, j: (l, 0, j))],
        out_specs=pl.BlockSpec((1, SUBLANES, tn), lambda l, j: (l, 0, j)),
        out_shape=jax.ShapeDtypeStruct((depth, SUBLANES, cols), F32),
        compiler_params=_params("arbitrary", "arbitrary"),
        name="modulation",
    )(cvec, mod_w, mod_b.reshape(depth, 1, cols))


def _moe_combine(x1_ref, picked_ref, route_ref, pmod_ref):
    route = route_ref[...]
    y = picked_ref[0].astype(F32) * route[:, TOP_K:TOP_K + 1]
    for kk in range(1, TOP_K):
        y = y + picked_ref[kk].astype(F32) * route[:, TOP_K + kk:TOP_K + kk + 1]
    return x1_ref[...] + pmod_ref[0][5:6] * y


def _combine_kernel(x1_ref, picked_ref, route_ref, pmod_ref, o_ref):
    o_ref[...] = _moe_combine(x1_ref, picked_ref, route_ref, pmod_ref)


def _final_combine(x1, picked, route, pmodtab, n_lat):
    d = x1.shape[1]
    row = lambda cols: pl.BlockSpec((ROW_TILE, cols), lambda i: (i, 0))
    return pl.pallas_call(
        _combine_kernel,
        grid=(n_lat // ROW_TILE,),
        in_specs=[row(d), pl.BlockSpec((TOP_K, ROW_TILE, d), lambda i: (0, i, 0)), row(LANES),
                  pl.BlockSpec((1, SUBLANES, d), lambda i: (0, 0, 0))],
        out_specs=row(d),
        out_shape=jax.ShapeDtypeStruct((n_lat, d), F32),
        compiler_params=_params("arbitrary"),
        name="moe_combine",
    )(x1, picked, route, pmodtab)


def _inproj_kernel(*refs, has_moe):
    if has_moe:
        x1_ref, picked_ref, route_ref, pmod_ref = refs[:4]
        refs = refs[4:]
    else:
        x_ref = refs[0]
        refs = refs[1:]
    (mod_ref, g1_ref, win_ref, qg_ref, kvg_ref, wuq_ref, wuk_ref, wuv_ref, qng_ref, kng_ref,
     rc_ref, rsa_ref, rsb_ref) = refs[:13]
    outs = refs[13:]
    if has_moe:
        xs_ref, outs = outs[0], outs[1:]
        x = _moe_combine(x1_ref, picked_ref, route_ref, pmod_ref)
        xs_ref[...] = x
    else:
        x = x_ref[...]
    q_ref, k_ref, v_ref, z_ref, xbc_ref, dt_ref, pool_ref = outs
    mod = mod_ref[0]
    shift, scale = mod[0:1], mod[1:2]
    h = _rms(x) * g1_ref[...] * (1.0 + scale) + shift
    proj = jnp.dot(h.astype(BF16), win_ref[...], preferred_element_type=F32)
    z_ref[...] = proj[:, COL_Z:COL_XBC]
    xbc_ref[...] = proj[:, COL_XBC:COL_DT]
    dt_ref[...] = proj[:, COL_DT:COL_POOL]
    pool_ref[...] = proj[:, COL_POOL:IN_COLS_PACKED]

    rc, rsa, rsb = rc_ref[...], rsa_ref[...], rsb_ref[...]

    def head_norm_rope(t, gain):
        ms = _row_sumsq(t) * (1.0 / QK_DIM)
        t = t * lax.rsqrt(ms + EPS) * gain
        return (t * rc + pltpu.roll(t, QK_ROPE // 4, axis=1) * rsa
                + pltpu.roll(t, HEAD_PAD - QK_ROPE // 4, axis=1) * rsb)

    qn = (_rms(proj[:, COL_QLAT:COL_KVLAT]) * qg_ref[...]).astype(BF16)
    q_all = jnp.dot(qn, wuq_ref[...], preferred_element_type=F32)
    kvn = (_rms(proj[:, COL_KVLAT:COL_KROPE]) * kvg_ref[...]).astype(BF16)
    k_all = jnp.dot(kvn, wuk_ref[...], preferred_element_type=F32)
    k_rope = proj[:, COL_KROPE:COL_Z]
    vt_all = lax.dot_general(wuv_ref[...], kvn, (((1,), (1,)), ((), ())), preferred_element_type=F32)
    vrow = lax.broadcasted_iota(jnp.int32, vt_all.shape, 0) % V_ROWS
    v_ref[0] = jnp.where(vrow == V_HEAD, 1.0, vt_all).astype(BF16)
    qng, kng = qng_ref[...], kng_ref[...]
    for hd in range(MLA_HEADS):
        sl = slice(hd * HEAD_PAD, (hd + 1) * HEAD_PAD)
        q_ref[:, sl] = (head_norm_rope(q_all[:, sl], qng) * Q_SCALE).astype(BF16)
        k_ref[:, sl] = head_norm_rope(k_all[:, sl] + k_rope, kng).astype(BF16)


def _in_projection(x, moe, modtab, g1, w_in_p, qg, kvg, wuq_p, wuk_p, wuv, qng_p, kng_p, rope_c, rope_sa, rope_sb,
                   n_lat_tiles):
    t_rows, d = x.shape
    nt = pl.cdiv(t_rows, ROW_TILE)
    row = lambda cols: pl.BlockSpec((ROW_TILE, cols), lambda i: (i, 0))
    full = lambda a: pl.BlockSpec(a.shape, lambda i: (0,) * a.ndim)
    seg = lambda: pl.BlockSpec((1, SUBLANES, d), lambda i: (i // n_lat_tiles, 0, 0))
    hw = MLA_HEADS * HEAD_PAD
    per = ATT_TK // ROW_TILE
    vw = MLA_HEADS * V_ROWS
    out_specs = [row(hw), row(hw), pl.BlockSpec((1, vw, ROW_TILE), lambda i: (i // per, 0, i % per)),
                 row(SSD_WIDTH), row(SSD_CONV_DIM), row(LANES), row(POOL_WIDTH)]
    sds = lambda cols, dt: jax.ShapeDtypeStruct((t_rows, cols), dt)
    out_shape = [sds(hw, BF16), sds(hw, BF16), jax.ShapeDtypeStruct((pl.cdiv(nt, per), vw, ATT_TK), BF16),
                 sds(SSD_WIDTH, F32), sds(SSD_CONV_DIM, F32), sds(LANES, F32), sds(POOL_WIDTH, F32)]
    lead_in, lead_specs = [x], [row(d)]
    if moe is not None:
        picked, route, pmodtab = moe
        lead_in += [picked, route, pmodtab]
        lead_specs += [pl.BlockSpec((TOP_K, ROW_TILE, d), lambda i: (0, i, 0)), row(LANES), seg()]
        out_specs = [row(d)] + out_specs
        out_shape = [sds(d, F32)] + out_shape
    return pl.pallas_call(
        functools.partial(_inproj_kernel, has_moe=moe is not None),
        grid=(nt,),
        in_specs=lead_specs + [seg(), full(g1), full(w_in_p), full(qg), full(kvg), full(wuq_p), full(wuk_p),
                               full(wuv), full(qng_p), full(kng_p), row(HEAD_PAD), row(HEAD_PAD), row(HEAD_PAD)],
        out_specs=out_specs,
        out_shape=out_shape,
        compiler_params=_params("arbitrary"),
        name="in_projection",
    )(*lead_in, modtab, g1, w_in_p, qg, kvg, wuq_p, wuk_p, wuv, qng_p, kng_p, rope_c, rope_sa, rope_sb)


def _attention_kernel(q_ref, k_ref, vt_ref, o_ref, sa_ref, sb_ref, st_ref, ca_ref, cb_ref, ct_ref, m_ref, acc_ref,
                      *, n_main, tk, tail):
    m_ref[...] = jnp.full(m_ref.shape, -jnp.inf, F32)
    acc_ref[...] = jnp.zeros(acc_ref.shape, F32)

    def scores(buf, start, size):
        s_ref, cmax_ref = buf
        for hh in range(2):
            q = q_ref[:, hh * HEAD_PAD:(hh + 1) * HEAD_PAD]
            k = k_ref[pl.ds(start, size), hh * HEAD_PAD:(hh + 1) * HEAD_PAD]
            s = lax.dot_general(k, q, (((1,), (1,)), ((), ())), preferred_element_type=F32)
            s_ref[hh] = s
            cmax_ref[hh] = jnp.max(s, axis=0, keepdims=True)

    def consume(buf, c, size):
        s_ref, cmax_ref = buf
        for hh in range(2):
            s = s_ref[hh]
            m_old = m_ref[hh]
            m_new = jnp.maximum(m_old, cmax_ref[hh])
            p = jnp.exp2((s - m_new).astype(BF16))
            vt = vt_ref[c, hh * V_ROWS:(hh + 1) * V_ROWS, 0:size]
            acc_ref[hh] = jnp.exp2(m_old - m_new) * acc_ref[hh] + jnp.dot(vt, p, preferred_element_type=F32)
            m_ref[hh] = m_new

    row0 = lambda c: pl.multiple_of(c * tk, tk)
    t_start, t_size, t_chunk = tail
    sa_ref, sb_ref, st_ref = (sa_ref, ca_ref), (sb_ref, cb_ref), (st_ref, ct_ref)
    scores(st_ref, t_start, t_size)
    if n_main > 0:
        scores(sa_ref, 0, tk)
    consume(st_ref, t_chunk, t_size)
    if n_main > 0:
        def body(j, carry):
            scores(sb_ref, row0(2 * j + 1), tk)
            consume(sa_ref, 2 * j, tk)
            scores(sa_ref, row0(2 * j + 2), tk)
            consume(sb_ref, 2 * j + 1, tk)
            return carry
        lax.fori_loop(0, n_main // 2 - 1, body, 0)
        scores(sb_ref, (n_main - 1) * tk, tk)
        consume(sa_ref, n_main - 2, tk)
        consume(sb_ref, n_main - 1, tk)
    outs = []
    for hh in range(2):
        acc = acc_ref[hh]
        outs.append(acc[0:V_HEAD] / acc[V_HEAD:V_HEAD + 1])
    o_ref[...] = jnp.concatenate(outs, axis=0).T.astype(o_ref.dtype)


def _attention(q, k, vt, n_lat, n_ctx):
    t_rows = q.shape[0]
    pairs = MLA_HEADS // 2
    n_main = n_lat // ATT_TK
    n_chunks = vt.shape[0]
    assert n_main % 2 == 0 and n_main >= 2 and n_chunks == n_main + 1

    def scratch(tq, tk, t_size):
        return ([pltpu.VMEM((2, tk, tq), F32), pltpu.VMEM((2, tk, tq), F32), pltpu.VMEM((2, t_size, tq), F32)]
                + [pltpu.VMEM((2, 1, tq), F32)] * 4 + [pltpu.VMEM((2, V_ROWS, tq), F32)])

    lat = pl.pallas_call(
        functools.partial(_attention_kernel, n_main=n_main, tk=ATT_TK, tail=(n_lat, n_ctx, n_main)),
        grid=(pairs, n_lat // ATT_TQ),
        in_specs=[pl.BlockSpec((ATT_TQ, 2 * HEAD_PAD), lambda p, i: (i, p)),
                  pl.BlockSpec((t_rows, 2 * HEAD_PAD), lambda p, i: (0, p), pipeline_mode=pl.Buffered(1)),
                  pl.BlockSpec((n_chunks, 2 * V_ROWS, ATT_TK), lambda p, i: (0, p, 0),
                               pipeline_mode=pl.Buffered(1))],
        out_specs=pl.BlockSpec((ATT_TQ, 2 * V_HEAD), lambda p, i: (i, p)),
        out_shape=jax.ShapeDtypeStruct((n_lat, MLA_WIDTH), BF16),
        scratch_shapes=scratch(ATT_TQ, ATT_TK, n_ctx),
        compiler_params=_params("arbitrary", "arbitrary"),
        name="attention_latent",
    )(q, k, vt)
    cblk = n_lat // n_ctx
    cspec = lambda: pl.BlockSpec((n_ctx, 2 * HEAD_PAD), lambda p: (cblk, p))
    ctx = pl.pallas_call(
        functools.partial(_attention_kernel, n_main=0, tk=SUBLANES, tail=(0, n_ctx, 0)),
        grid=(pairs,),
        in_specs=[cspec(), cspec(), pl.BlockSpec((1, 2 * V_ROWS, ATT_TK), lambda p: (n_main, p, 0))],
        out_specs=pl.BlockSpec((n_ctx, 2 * V_HEAD), lambda p: (0, p)),
        out_shape=jax.ShapeDtypeStruct((n_ctx, MLA_WIDTH), BF16),
        scratch_shapes=scratch(n_ctx, SUBLANES, n_ctx),
        compiler_params=_params("arbitrary"),
        name="attention_context",
    )(q, k, vt)
    return jnp.concatenate([lat, ctx], axis=0)


def _halo_specs(cols, tile_rows, t_rows):
    per = tile_rows // HALO
    last = t_rows // HALO - 1
    return [pl.BlockSpec((HALO, cols), lambda i: (jnp.maximum(i * per - 1, 0), 0)),
            pl.BlockSpec((tile_rows, cols), lambda i: (i, 0)),
            pl.BlockSpec((HALO, cols), lambda i: (jnp.minimum((i + 1) * per, last), 0))]


def _with_halo(prev_ref, cur_ref, next_ref, i, n_lat_tiles, t_rows):
    has_prev = jnp.logical_and(i != 0, i != n_lat_tiles)
    has_next = jnp.logical_and(i != n_lat_tiles - 1, i != pl.num_programs(0) - 1)
    prev = jnp.where(has_prev, prev_ref[...], 0.0)
    nxt = jnp.where(has_next, next_ref[...], 0.0)
    cur = cur_ref[...]
    row = lax.broadcasted_iota(jnp.int32, cur.shape, 0)
    cur = jnp.where(row < t_rows - i * cur.shape[0], cur, 0.0)
    return jnp.concatenate([prev, cur, nxt], axis=0)


def _conv_kernel(prev_ref, cur_ref, next_ref, w_ref, b_ref, o_ref, *, n_lat_tiles, t_rows):
    i = pl.program_id(0)
    ext = _with_halo(prev_ref, cur_ref, next_ref, i, n_lat_tiles, t_rows)
    rows = ext.shape[0]
    w = w_ref[...]
    y = (pltpu.roll(ext, 2, axis=0) * w[0:1] + pltpu.roll(ext, 1, axis=0) * w[1:2]
         + ext * w[2:3] + pltpu.roll(ext, rows - 1, axis=0) * w[3:4])
    y = y[HALO:rows - HALO] + b_ref[...]
    o_ref[...] = y / (1.0 + jnp.exp(-y))


def _conv(xbc, conv_w, conv_b, n_lat_tiles):
    t_rows, cols = xbc.shape
    nt = pl.cdiv(t_rows, ROW_TILE)
    return pl.pallas_call(
        functools.partial(_conv_kernel, n_lat_tiles=n_lat_tiles, t_rows=t_rows),
        grid=(nt,),
        in_specs=_halo_specs(cols, ROW_TILE, t_rows) + [
            pl.BlockSpec(conv_w.shape, lambda i: (0, 0)), pl.BlockSpec(conv_b.shape, lambda i: (0, 0))],
        out_specs=pl.BlockSpec((ROW_TILE, cols), lambda i: (i, 0)),
        out_shape=jax.ShapeDtypeStruct((t_rows, cols), F32),
        compiler_params=_params("arbitrary"),
        name="ssd_conv",
    )(xbc, xbc, xbc, conv_w, conv_b)


def _softplus(v):
    return jnp.maximum(v, 0.0) + jnp.log(1.0 + jnp.exp(-jnp.abs(v)))


def _ssd_kernel(xf_ref, xb_ref, dtf_ref, dtb_ref, dttf_ref, dttb_ref, bias_ref, biast_ref, alog_ref, alogt_ref,
                dskip_ref, expand_ref, yf_ref, yb_ref, state_ref):
    L = xf_ref.shape[0]
    P, N, H = SSD_HEAD_DIM, SSD_STATE, SSD_HEADS

    @pl.when(pl.program_id(0) == 0)
    def _():
        state_ref[...] = jnp.zeros(state_ref.shape, F32)

    r = lax.broadcasted_iota(jnp.int32, (L, L), 0)
    c = lax.broadcasted_iota(jnp.int32, (L, L), 1)
    lower = r >= c
    upper = r <= c
    lower_f = lower.astype(BF16)
    upper_f = upper.astype(BF16)

    def split3(a):
        a1 = a.astype(BF16)
        r1 = a - a1.astype(F32)
        a2 = r1.astype(BF16)
        return a1, a2, (r1 - a2.astype(F32)).astype(BF16)
    a_row = -jnp.exp(alog_ref[...])
    a_col = -jnp.exp(alogt_ref[...])

    def one_direction(x_ref, dt_ref, dtt_ref, y_ref, base, forward):
        x = x_ref[...]
        dt = _softplus(dt_ref[...] + bias_ref[...])
        dtt = _softplus(dtt_ref[...] + biast_ref[...])
        tri_col = lower_f if forward else upper_f
        tri_row = upper_f if forward else lower_f
        cs = sum(jnp.dot(tri_col, part, preferred_element_type=F32) for part in split3(dt * a_row))
        cst = sum(jnp.dot(part, tri_row, preferred_element_type=F32) for part in split3(dtt * a_col))
        mask = lower if forward else upper
        end = L - 1 if forward else 0
        outs = []
        cb = []
        for g in range(SSD_GROUPS):
            bg = x[:, SSD_WIDTH + g * N:SSD_WIDTH + (g + 1) * N].astype(BF16)
            cg = x[:, SSD_WIDTH + (SSD_GROUPS + g) * N:SSD_WIDTH + (SSD_GROUPS + g + 1) * N].astype(BF16)
            cb.append((bg, cg, lax.dot_general(cg, bg, (((1,), (1,)), ((), ())),
                                               preferred_element_type=F32).astype(BF16)))
        spread = expand_ref[0 if forward else 1]

        def replicate(a):
            hi = a.astype(BF16)
            lo = (a - hi.astype(F32)).astype(BF16)
            return (jnp.dot(hi, spread, preferred_element_type=F32)
                    + jnp.dot(lo, spread, preferred_element_type=F32))

        cs_rep, dt_rep = replicate(cs), replicate(dt)
        for hd in range(H):
            j = base + hd
            bg, cg, cbg = cb[hd // (H // SSD_GROUPS)]
            col = cs_rep[:, hd * LANES:(hd + 1) * LANES]
            rowv = cst[j:j + 1, :]
            total = col[end:end + 1, :]
            gap = jnp.tile(col, (1, L // LANES)) - rowv
            decay = jnp.where(mask, jnp.exp(jnp.minimum(gap, 0.0).astype(BF16)), 0.0)
            xh = (x[:, hd * P:(hd + 1) * P] * dt_rep[:, hd * LANES:hd * LANES + P]).astype(BF16)
            y = jnp.dot(cbg * decay, xh, preferred_element_type=F32)
            st = state_ref[j]
            y = y + jnp.dot(cg, st.astype(BF16), preferred_element_type=F32) * jnp.exp(col[:, :P])
            bw = (bg.astype(F32) * jnp.exp(total[:, :N] - col[:, :N])).astype(BF16)
            state_ref[j] = st * jnp.exp(total[:, :P]) + lax.dot_general(
                bw, xh, (((0,), (0,)), ((), ())), preferred_element_type=F32)
            outs.append(y)
        y_all = jnp.concatenate(outs, axis=1)
        if forward:
            y_all = y_all + x[:, :SSD_WIDTH] * dskip_ref[...]
        y_ref[...] = y_all

    one_direction(xf_ref, dtf_ref, dttf_ref, yf_ref, 0, True)
    one_direction(xb_ref, dtb_ref, dttb_ref, yb_ref, H, False)


def _ssd(xc, dt, dtt, bias_row, bias_col, alog_row, alog_col, dskip_row, n_lat_chunks):
    t_rows, cols = xc.shape
    L = SSD_CHUNK
    nc = t_rows // L
    fwd = lambda j: jnp.where(j == 0, n_lat_chunks, j - 1)
    bwd = lambda j: jnp.where(j == 0, n_lat_chunks, n_lat_chunks - j)
    small = lambda a: pl.BlockSpec(a.shape, lambda j: (0,) * a.ndim)
    r = jnp.arange(LANES)[None, :, None] - jnp.arange(2)[:, None, None] * SSD_HEADS
    spread = (r == jnp.arange(SSD_HEADS * LANES)[None, None, :] // LANES).astype(BF16)
    return pl.pallas_call(
        _ssd_kernel,
        grid=(nc,),
        in_specs=[pl.BlockSpec((L, cols), lambda j: (fwd(j), 0)),
                  pl.BlockSpec((L, cols), lambda j: (bwd(j), 0)),
                  pl.BlockSpec((L, LANES), lambda j: (fwd(j), 0)),
                  pl.BlockSpec((L, LANES), lambda j: (bwd(j), 0)),
                  pl.BlockSpec((2 * SUBLANES, L), lambda j: (0, fwd(j))),
                  pl.BlockSpec((2 * SUBLANES, L), lambda j: (0, bwd(j))),
                  small(bias_row), small(bias_col), small(alog_row), small(alog_col), small(dskip_row),
                  small(spread)],
        out_specs=[pl.BlockSpec((L, SSD_WIDTH), lambda j: (fwd(j), 0)),
                   pl.BlockSpec((L, SSD_WIDTH), lambda j: (bwd(j), 0))],
        out_shape=[jax.ShapeDtypeStruct((t_rows, SSD_WIDTH), F32)] * 2,
        scratch_shapes=[pltpu.VMEM((2 * SSD_HEADS, SSD_STATE, SSD_HEAD_DIM), F32)],
        compiler_params=_params("arbitrary"),
        name="ssd_scan",
    )(xc, xc, dt, dt, dtt, dtt, bias_row, bias_col, alog_row, alog_col, dskip_row, spread)


def _mixout_kernel(x_ref, mod_ref, att_ref, yf_ref, yb_ref, z_ref, pprev_ref, pcur_ref, pnext_ref,
                   ssdg_ref, poolw_ref, pools_ref, wout_ref, g2_ref, rw_ref, rb_ref,
                   x1_ref, h2_ref, route_ref, cnt_ref, base_ref, *, n_lat_tiles, n_lat, n_ctx, t_rows):
    i = pl.program_id(0)
    mod = mod_ref[0]
    z = z_ref[...]
    ssd = _rms((yf_ref[...] + yb_ref[...]) * (z / (1.0 + jnp.exp(-z)))) * ssdg_ref[...]
    ext = _with_halo(pprev_ref, pcur_ref, pnext_ref, i, n_lat_tiles, t_rows)
    rows = ext.shape[0]
    tm = rows - 2 * HALO
    w2 = ext + pltpu.roll(ext, 1, axis=0)
    w4 = pltpu.roll(w2, 1, axis=0) + pltpu.roll(w2, rows - 1, axis=0)
    w8 = pltpu.roll(w4, 2, axis=0) + pltpu.roll(w4, rows - 2, axis=0)
    w16 = pltpu.roll(w8, 4, axis=0) + pltpu.roll(w8, rows - 4, axis=0)
    lane = lax.broadcasted_iota(jnp.int32, (tm, POOL_WIDTH), 1)
    grp = lane // POOL_GROUP
    sl = slice(HALO, rows - HALO)
    wsum = jnp.where(grp == 0, w2[sl], jnp.where(grp == 1, w4[sl], jnp.where(grp == 2, w8[sl], w16[sl])))
    is_ctx = i >= n_lat_tiles
    seg_len = jnp.where(is_ctx, n_ctx, n_lat)
    t = lax.broadcasted_iota(jnp.int32, (tm, POOL_WIDTH), 0) + jnp.where(is_ctx, i - n_lat_tiles, i) * tm
    half = jnp.left_shift(1, grp)
    lo = jnp.clip(t - half, 0, seg_len)
    hi = jnp.clip(t + half, 0, seg_len)
    p = wsum / (hi - lo).astype(F32) - pcur_ref[...]
    pool = jnp.dot(p.astype(BF16), poolw_ref[...], preferred_element_type=F32) * pools_ref[...]
    mix = (jnp.dot(att_ref[...], wout_ref[0:MLA_WIDTH], preferred_element_type=F32)
           + jnp.dot(ssd.astype(BF16), wout_ref[MLA_WIDTH:MLA_WIDTH + SSD_WIDTH], preferred_element_type=F32)
           + jnp.dot(pool.astype(BF16), wout_ref[MLA_WIDTH + SSD_WIDTH:], preferred_element_type=F32))
    x1 = x_ref[...] + mod[2:3] * mix
    x1_ref[...] = x1
    h2 = _rms(x1) * g2_ref[...] * (1.0 + mod[4:5]) + mod[3:4]
    h2b = h2.astype(BF16)
    for kk in range(TOP_K):
        h2_ref[kk] = h2b
    h2_lo = (h2 - h2b.astype(F32)).astype(BF16)
    logits = (jnp.dot(h2b, rw_ref[0], preferred_element_type=F32)
              + jnp.dot(h2_lo, rw_ref[0], preferred_element_type=F32)
              + jnp.dot(h2b, rw_ref[1], preferred_element_type=F32)) + rb_ref[...]

    @pl.when(i == 0)
    def _():
        base_ref[...] = jnp.zeros(base_ref.shape, F32)

    elane = lax.broadcasted_iota(jnp.int32, logits.shape, 1)
    row_ok = lax.broadcasted_iota(jnp.int32, logits.shape, 0) < t_rows - i * tm
    lg = jnp.where(elane < N_EXPERTS, logits, -jnp.inf)
    rr = lax.broadcasted_iota(jnp.int32, (tm, tm), 0)
    cc = lax.broadcasted_iota(jnp.int32, (tm, tm), 1)
    earlier = (rr > cc).astype(BF16)
    offset = base_ref[...]
    tops, ids, ranks = [], [], []
    for kk in range(TOP_K):
        top = jnp.max(lg, axis=-1, keepdims=True)
        idx = jnp.min(jnp.where(lg == top, elane, LANES), axis=-1, keepdims=True)
        sel = jnp.logical_and(elane == idx, row_ok)
        lg = jnp.where(sel, -jnp.inf, lg)
        onehot = sel.astype(BF16)
        before = jnp.dot(earlier, onehot, preferred_element_type=F32) + offset
        ranks.append(jnp.sum(jnp.where(sel, before, 0.0), axis=-1, keepdims=True))
        offset = offset + jnp.sum(sel.astype(F32), axis=0, keepdims=True)
        tops.append(top)
        ids.append(idx.astype(F32))
    base_ref[...] = offset
    cnt_ref[...] = jnp.broadcast_to(offset, cnt_ref.shape)
    exps = [jnp.exp(tp - tops[0]) for tp in tops]
    denom = exps[0] + exps[1] + exps[2] + exps[3]
    route = jnp.zeros(logits.shape, F32)
    for kk in range(TOP_K):
        route = jnp.where(elane == kk, ids[kk], route)
        route = jnp.where(elane == TOP_K + kk, exps[kk] / denom, route)
        route = jnp.where(elane == 2 * TOP_K + kk, ranks[kk], route)
    route_ref[...] = route


def _mixer_output(x, modtab, att, yf, yb, z, pool_in, ssdg, poolw_bd, pools, wout, g2, rw_p, rb_p,
                  n_lat_tiles, n_lat, n_ctx):
    t_rows, d = x.shape
    nt = pl.cdiv(t_rows, ROW_TILE)
    row = lambda cols: pl.BlockSpec((ROW_TILE, cols), lambda i: (i, 0))
    full = lambda a: pl.BlockSpec(a.shape, lambda i: (0,) * a.ndim)
    return pl.pallas_call(
        functools.partial(_mixout_kernel, n_lat_tiles=n_lat_tiles, n_lat=n_lat, n_ctx=n_ctx, t_rows=t_rows),
        grid=(nt,),
        in_specs=[row(d), pl.BlockSpec((1, SUBLANES, d), lambda i: (i // n_lat_tiles, 0, 0)),
                  row(MLA_WIDTH), row(SSD_WIDTH), row(SSD_WIDTH), row(SSD_WIDTH)]
                 + _halo_specs(POOL_WIDTH, ROW_TILE, t_rows)
                 + [full(ssdg), full(poolw_bd), full(pools), full(wout), full(g2), full(rw_p), full(rb_p)],
        out_specs=[row(d), pl.BlockSpec((TOP_K, ROW_TILE, d), lambda i: (0, i, 0)), row(LANES),
                   pl.BlockSpec((SUBLANES, LANES), lambda i: (0, 0))],
        out_shape=[jax.ShapeDtypeStruct((t_rows, d), F32), jax.ShapeDtypeStruct((TOP_K, t_rows, d), BF16),
                   jax.ShapeDtypeStruct((t_rows, LANES), F32), jax.ShapeDtypeStruct((SUBLANES, LANES), F32)],
        scratch_shapes=[pltpu.VMEM((1, LANES), F32)],
        compiler_params=_params("arbitrary"),
        name="mixer_output",
    )(x, modtab, att, yf, yb, z, pool_in, pool_in, pool_in, ssdg, poolw_bd, pools, wout, g2, rw_p, rb_p)


def _expert_kernel(be_ref, nu_ref, x_ref, wg_ref, bg_ref, wu_ref, bu_ref, wd_ref, bd_ref, o_ref,
                   wg_s, wu_s, wd_s):
    i = pl.program_id(0)

    @pl.when(jnp.logical_or(i == 0, be_ref[i] != be_ref[jnp.maximum(i - 1, 0)]))
    def _():
        wg_s[...] = wg_ref[0, 0].astype(BF16)
        wu_s[...] = wu_ref[0, 0].astype(BF16)
        wd_s[...] = wd_ref[0, 0].astype(BF16)

    @pl.when(i < nu_ref[0])
    def _():
        x = x_ref[...]
        g = jnp.minimum(jnp.dot(x, wg_s[...], preferred_element_type=F32) + bg_ref[0, 0], SWIGLU_LIMIT)
        u = jnp.clip(jnp.dot(x, wu_s[...], preferred_element_type=F32) + bu_ref[0, 0],
                     -SWIGLU_LIMIT, SWIGLU_LIMIT)
        a = g / (1.0 + jnp.exp(-SWIGLU_ALPHA * g)) * (u + 1.0)
        y = jnp.dot(a.astype(BF16), wd_s[...], preferred_element_type=F32) + bd_ref[0, 0]
        o_ref[...] = y.astype(o_ref.dtype)

    @pl.when(i >= nu_ref[0])
    def _():
        o_ref[...] = jnp.zeros(o_ref.shape, o_ref.dtype)


def _experts(layer, x_sorted, block_expert, n_used, wg, bg, wu, bu, wd, bd):
    n_rows, d = x_sorted.shape
    depth, ne, _, dff = wg.shape
    rowspec = lambda cols: pl.BlockSpec((MOE_TILE, cols), lambda i, be, nu: (i, 0))
    wspec = lambda a, b: pl.BlockSpec((1, 1, a, b), lambda i, be, nu: (layer, be[i], 0, 0))
    grid_spec = pltpu.PrefetchScalarGridSpec(
        num_scalar_prefetch=2,
        grid=(n_rows // MOE_TILE,),
        in_specs=[rowspec(d), wspec(d, dff), wspec(1, dff), wspec(d, dff), wspec(1, dff),
                  wspec(dff, d), wspec(1, d)],
        out_specs=rowspec(d),
        scratch_shapes=[pltpu.VMEM((d, dff), BF16), pltpu.VMEM((d, dff), BF16), pltpu.VMEM((dff, d), BF16)],
    )
    return pl.pallas_call(
        _expert_kernel,
        grid_spec=grid_spec,
        out_shape=jax.ShapeDtypeStruct((n_rows, d), BF16),
        compiler_params=_params("arbitrary"),
        name="moe_experts",
    )(block_expert, n_used, x_sorted, wg, bg.reshape(depth, ne, 1, dff), wu, bu.reshape(depth, ne, 1, dff),
      wd, bd.reshape(depth, ne, 1, d))


def _dispatch_plan(route, counts_f):
    t_rows = route.shape[0]
    n_assign = t_rows * TOP_K
    ids = route[:, 0:TOP_K].astype(jnp.int32)
    rank =route[:, 2 * TOP_K:3 * TOP_K].astype(jnp.int32)
    counts = counts_f[0, :N_EXPERTS].astype(jnp.int32)
    padded = (counts + MOE_TILE - 1) // MOE_TILE * MOE_TILE
    pad_end = jnp.cumsum(padded)
    pad_start = pad_end - padded
    experts = jnp.arange(N_EXPERTS, dtype=jnp.int32)
    first_row = jnp.sum(jnp.where(ids[:, :, None] == experts[None, None, :], pad_start[None, None, :], 0), axis=-1)
    dest = (first_row + rank).T.reshape(-1)
    n_rows = n_assign + N_EXPERTS * MOE_TILE
    n_blocks = n_rows // MOE_TILE
    block_expert = jnp.minimum(
        jnp.sum((jnp.arange(n_blocks, dtype=jnp.int32)[:, None] * MOE_TILE >= pad_end[None, :]).astype(jnp.int32),
                axis=1), N_EXPERTS - 1).astype(jnp.int32)
    n_pad = n_rows - n_assign
    pads = jnp.concatenate([padded - counts, n_rows - pad_end[-1:]])
    pads_end = jnp.cumsum(pads)
    pad_first = jnp.concatenate([pad_start + counts, pad_end[-1:]])
    j = jnp.arange(n_pad, dtype=jnp.int32)
    grp = jnp.sum((j[:, None] >= pads_end[None, :]).astype(jnp.int32), axis=1)
    hit = grp[:, None] == jnp.arange(N_EXPERTS + 1, dtype=jnp.int32)[None, :]
    pad_rows = j + jnp.sum(jnp.where(hit, (pad_first - (pads_end - pads))[None, :], 0), axis=1)
    keys = jnp.concatenate([dest, pad_rows]).astype(jnp.int32)
    vals = jnp.concatenate([jnp.arange(n_assign, dtype=jnp.int32), j])
    row_src = lax.sort((keys, vals), num_keys=1)[1]
    n_used = (pad_end[-1:] // MOE_TILE).astype(jnp.int32)
    return dest, row_src, block_expert, n_used


def _rope_tables(n_lat, n_ctx):
    rows = n_lat // GRID_W
    row = jnp.repeat(jnp.arange(rows), GRID_W).astype(F32)
    col = jnp.tile(jnp.arange(GRID_W), rows).astype(F32)
    n_freq = QK_ROPE // 4
    inv_freq = ROPE_THETA ** (-jnp.arange(n_freq, dtype=F32) / n_freq)
    ang = jnp.stack([row[:, None] * inv_freq, col[:, None] * inv_freq], axis=1)
    cos, sin = jnp.cos(ang), jnp.sin(ang)
    zero = jnp.zeros_like(sin)
    ones = lambda w: jnp.ones((n_lat, w), F32)
    zeros = lambda w: jnp.zeros((n_lat, w), F32)
    per_axis = lambda a, b: jnp.stack([a, b], axis=2).reshape(n_lat, QK_ROPE)
    tail = HEAD_PAD - QK_DIM
    c = jnp.concatenate([ones(QK_NOPE), per_axis(cos, cos), ones(tail)], axis=1)
    sa = jnp.concatenate([zeros(QK_NOPE), per_axis(zero, sin), zeros(tail)], axis=1)
    sb = jnp.concatenate([zeros(QK_NOPE), per_axis(-sin, zero), zeros(tail)], axis=1)
    pad = lambda a, v: jnp.concatenate([a, jnp.full((n_ctx, HEAD_PAD), v, F32)], axis=0)
    return pad(c, 1.0), pad(sa, 0.0), pad(sb, 0.0)


def _pad_cols(a, width):
    return jnp.pad(a, ((0, 0), (0, width - a.shape[1])))


def _split_bf16(a):
    hi = a.astype(BF16)
    return jnp.stack([hi, (a - hi.astype(F32)).astype(BF16)])


def _pack_layer(i, w_in, q_g, kv_g, w_uq, w_ukv, qn_g, kn_g, conv_w, conv_b, a_log, dt_bias, d_skip, ssd_g,
                pool_w, pool_scale, w_out, router_w, router_b):
    d = w_in.shape[1]
    pts = [0, Q_LORA, KV_LORA, QK_ROPE, SSD_WIDTH, SSD_CONV_DIM, 2 * SSD_HEADS, POOL_WIDTH]
    offs = [sum(pts[:j + 1]) for j in range(len(pts))]
    seg = lambda j: w_in[i][:, offs[j]:offs[j + 1]]
    zc = lambda w: jnp.zeros((d, w), F32)
    w_in_p = jnp.concatenate([
        seg(0), seg(1), zc(QK_NOPE), seg(2), zc(LANES - QK_DIM), seg(3), seg(4),
        seg(5), zc(LANES - 2 * SSD_HEADS), seg(6)], axis=1).astype(BF16)
    wuq = w_uq[i].reshape(Q_LORA, MLA_HEADS, QK_DIM)
    wuq_p = jnp.pad(wuq, ((0, 0), (0, 0), (0, HEAD_PAD - QK_DIM))).reshape(Q_LORA, MLA_HEADS * HEAD_PAD)
    wukv = w_ukv[i].reshape(KV_LORA, MLA_HEADS, QK_NOPE + V_HEAD)
    wuk_p = jnp.pad(wukv[:, :, :QK_NOPE], ((0, 0), (0, 0), (0, HEAD_PAD - QK_NOPE))).reshape(
        KV_LORA, MLA_HEADS * HEAD_PAD)
    wuv = jnp.pad(wukv[:, :, QK_NOPE:], ((0, 0), (0, 0), (0, V_ROWS - V_HEAD))).reshape(
        KV_LORA, MLA_HEADS * V_ROWS).T
    flat12 = lambda a: a[i].reshape(1, 2 * SSD_HEADS)
    poolw_bd = jax.scipy.linalg.block_diag(*[pool_w[i][g] for g in range(len(POOL_WINDOWS))])
    return dict(
        w_in_p=w_in_p, qg=q_g[i][None], kvg=kv_g[i][None],
        wuq_p=wuq_p.astype(BF16), wuk_p=wuk_p.astype(BF16), wuv=wuv.astype(BF16),
        qng_p=_pad_cols(qn_g[i][None], HEAD_PAD), kng_p=_pad_cols(kn_g[i][None], HEAD_PAD),
        conv_w=conv_w[i], conv_b=conv_b[i][None],
        bias_row=_pad_cols(flat12(dt_bias), LANES), bias_col=_pad_cols(flat12(dt_bias), 2 * SUBLANES).T,
        alog_row=_pad_cols(flat12(a_log), LANES), alog_col=_pad_cols(flat12(a_log), 2 * SUBLANES).T,
        dskip_row=jnp.repeat(d_skip[i], SSD_HEAD_DIM)[None],
        ssdg=ssd_g[i][None], poolw_bd=poolw_bd.astype(BF16), pools=pool_scale[i][None],
        wout=w_out[i].astype(BF16), rw_p=_split_bf16(_pad_cols(router_w[i], LANES)),
        rb_p=_pad_cols(router_b[i][None], LANES))


def kernel(x, c, ctx, c_ctx, norm1_g, norm2_g, mod_w, mod_b, w_in, mla_q_norm_g, mla_kv_norm_g, mla_w_uq, mla_w_ukv, mla_qn_g, mla_kn_g, ssd_conv_w, ssd_conv_b, ssd_a_log, ssd_dt_bias, ssd_d, ssd_norm_g, pool_w, pool_scale, w_out, router_w, router_b, moe_w_gate, moe_b_gate, moe_w_up, moe_b_up, moe_w_down, moe_b_down):
    batch, n_lat, d = x.shape
    n_ctx = ctx.shape[1]
    depth = mod_w.shape[0]
    assert batch == 1 and d == D_MODEL and n_ctx == SSD_CHUNK and n_ctx <= ROW_TILE and n_lat % ROW_TILE == 0
    assert n_lat % ATT_TQ == 0 and n_lat % ATT_TK == 0 and n_lat % GRID_W == 0
    n_lat_tiles = n_lat // ROW_TILE

    cvec = jnp.zeros((SUBLANES, d), F32).at[0].set(c[0]).at[1].set(c_ctx)
    mods = _modulation(cvec, mod_w, mod_b)
    rope_c, rope_sa, rope_sb = _rope_tables(n_lat, n_ctx)

    xs = jnp.concatenate([x[0], ctx[0]], axis=0)
    moe = None
    for i in range(depth):
        lp = _pack_layer(i, w_in, mla_q_norm_g, mla_kv_norm_g, mla_w_uq, mla_w_ukv, mla_qn_g, mla_kn_g,
                         ssd_conv_w, ssd_conv_b, ssd_a_log, ssd_dt_bias, ssd_d, ssd_norm_g, pool_w, pool_scale,
                         w_out, router_w, router_b)
        modtab = jnp.pad(mods[i, :2].reshape(2, 6, d), ((0, 0), (0, SUBLANES - 6), (0, 0)))
        outs = _in_projection(
            xs, moe, modtab, norm1_g[i][None], lp['w_in_p'], lp['qg'], lp['kvg'], lp['wuq_p'], lp['wuk_p'], lp['wuv'],
            lp['qng_p'], lp['kng_p'], rope_c, rope_sa, rope_sb, n_lat_tiles)
        if moe is not None:
            xs, outs = outs[0], outs[1:]
        q, k, v, z, xbc, dt, pool_in = outs
        att = _attention(q, k, v, n_lat, n_ctx)
        xc = _conv(xbc, lp['conv_w'], lp['conv_b'], n_lat_tiles)
        dtt = dt[:, :2 * SUBLANES].T
        yf, yb = _ssd(xc, dt, dtt, lp['bias_row'], lp['bias_col'], lp['alog_row'], lp['alog_col'],
                      lp['dskip_row'], n_lat // SSD_CHUNK)
        x1, h2_slots, route, counts = _mixer_output(
            xs, modtab, att, yf, yb, z, pool_in, lp['ssdg'], lp['poolw_bd'], lp['pools'], lp['wout'],
            norm2_g[i][None], lp['rw_p'], lp['rb_p'], n_lat_tiles, n_lat, n_ctx)
        dest, row_src, block_expert, n_used = _dispatch_plan(route, counts)
        x_sorted = h2_slots.reshape(-1, d).at[row_src].get(mode='promise_in_bounds')
        out = _experts(i, x_sorted, block_expert, n_used,
                       moe_w_gate, moe_b_gate, moe_w_up, moe_b_up, moe_w_down, moe_b_down)
        picked = out.at[dest].get(mode='promise_in_bounds', unique_indices=True)
        xs, moe = x1, (picked.reshape(TOP_K, xs.shape[0], d), route, modtab)
    return _final_combine(xs, *moe, n_lat)[None]
```

```python
import functools

import jax
import jax.numpy as jnp
from jax import lax
from jax.experimental import pallas as pl
from jax.experimental.pallas import tpu as pltpu

F32 = jnp.float32
BF16 = jnp.bfloat16
HIGHEST = lax.Precision.HIGHEST

D_MODEL = 1024
GRID_W = 64
EPS = 1e-6
MLA_HEADS = 6
QK_NOPE = 64
QK_ROPE = 32
QK_DIM = QK_NOPE + QK_ROPE
V_HEAD = 64
Q_LORA = 256
KV_LORA = 128
MLA_WIDTH = MLA_HEADS * V_HEAD
ROPE_THETA = 10000.0
SSD_HEADS = 6
SSD_HEAD_DIM = 64
SSD_WIDTH = SSD_HEADS * SSD_HEAD_DIM
SSD_GROUPS = 2
SSD_STATE = 64
SSD_CONV = 4
SSD_CONV_DIM = SSD_WIDTH + 2 * SSD_GROUPS * SSD_STATE
POOL_WINDOWS = (2, 4, 8, 16)
POOL_GROUP = 64
POOL_WIDTH = len(POOL_WINDOWS) * POOL_GROUP
N_EXPERTS = 32
TOP_K = 4
D_FF = 1024
SWIGLU_LIMIT = 7.0
SWIGLU_ALPHA = 1.702

LANES = 128
SUBLANES = 8
HEAD_PAD = LANES
V_ROWS = HEAD_PAD
Q_SCALE = QK_DIM ** -0.5 * 1.4426950408889634

ROW_TILE = 512
SSD_CHUNK = 256
HALO = SUBLANES
ATT_TQ = 1024
ATT_TK = 512
MOE_TILE = 512
VMEM_LIMIT = 56 * 1024 * 1024

COL_QLAT = 0
COL_KVLAT = COL_QLAT + Q_LORA
COL_KROPE = COL_KVLAT + KV_LORA
COL_Z = COL_KROPE + LANES
COL_XBC = COL_Z + SSD_WIDTH
COL_DT = COL_XBC + SSD_CONV_DIM
COL_POOL = COL_DT + LANES
IN_COLS_PACKED = COL_POOL + POOL_WIDTH


def _row_sumsq(x):
    return jnp.dot((x * x).astype(BF16), jnp.ones((x.shape[1], LANES), BF16), preferred_element_type=F32)


def _rms(x):
    width = x.shape[1]
    inv = lax.rsqrt(_row_sumsq(x) * (1.0 / width) + EPS)
    return x * jnp.tile(inv, (1, width // LANES))


def _params(*sem):
    return pltpu.CompilerParams(dimension_semantics=sem, vmem_limit_bytes=VMEM_LIMIT)


def _mod_kernel(c_ref, w_ref, b_ref, o_ref):
    c = c_ref[...]
    s = c / (1.0 + jnp.exp(-c))
    o_ref[0] = jnp.dot(s, w_ref[0], precision=HIGHEST, preferred_element_type=F32) + b_ref[0]


def _modulation(cvec, mod_w, mod_b):
    depth, d, cols = mod_w.shape
    tn = 1536
    return pl.pallas_call(
        _mod_kernel,
        grid=(depth, cols // tn),
        in_specs=[pl.BlockSpec((SUBLANES, d), lambda l, j: (0, 0)),
                  pl.BlockSpec((1, d, tn), lambda l, j: (l, 0, j)),
                  pl.BlockSpec((1, 1, tn), lambda l, j: (l, 0, j))],
        out_specs=pl.BlockSpec((1, SUBLANES, tn), lambda l, j: (l, 0, j)),
        out_shape=jax.ShapeDtypeStruct((depth, SUBLANES, cols), F32),
        compiler_params=_params("arbitrary", "arbitrary"),
        name="modulation",
    )(cvec, mod_w, mod_b.reshape(depth, 1, cols))


def _moe_combine(x1_ref, picked_ref, route_ref, pmod_ref):
    route = route_ref[...]
    y = picked_ref[0].astype(F32) * route[:, TOP_K:TOP_K + 1]
    for kk in range(1, TOP_K):
        y = y + picked_ref[kk].astype(F32) * route[:, TOP_K + kk:TOP_K + kk + 1]
    return x1_ref[...] + pmod_ref[0][5:6] * y


def _combine_kernel(x1_ref, picked_ref, route_ref, pmod_ref, o_ref):
    o_ref[...] = _moe_combine(x1_ref, picked_ref, route_ref, pmod_ref)


def _final_combine(x1, picked, route, pmodtab, n_lat):
    d = x1.shape[1]
    row = lambda cols: pl.BlockSpec((ROW_TILE, cols), lambda i: (i, 0))
    return pl.pallas_call(
        _combine_kernel,
        grid=(n_lat // ROW_TILE,),
        in_specs=[row(d), pl.BlockSpec((TOP_K, ROW_TILE, d), lambda i: (0, i, 0)), row(LANES),
                  pl.BlockSpec((1, SUBLANES, d), lambda i: (0, 0, 0))],
        out_specs=row(d),
        out_shape=jax.ShapeDtypeStruct((n_lat, d), F32),
        compiler_params=_params("arbitrary"),
        name="moe_combine",
    )(x1, picked, route, pmodtab)


def _inproj_kernel(*refs, has_moe):
    if has_moe:
        x1_ref, picked_ref, route_ref, pmod_ref = refs[:4]
        refs = refs[4:]
    else:
        x_ref = refs[0]
        refs = refs[1:]
    (mod_ref, g1_ref, win_ref, qg_ref, kvg_ref, wuq_ref, wuk_ref, wuv_ref, qng_ref, kng_ref,
     rc_ref, rsa_ref, rsb_ref) = refs[:13]
    outs = refs[13:]
    if has_moe:
        xs_ref, outs = outs[0], outs[1:]
        x = _moe_combine(x1_ref, picked_ref, route_ref, pmod_ref)
        xs_ref[...] = x
    else:
        x = x_ref[...]
    q_ref, k_ref, v_ref, z_ref, xbc_ref, dt_ref, pool_ref = outs
    mod = mod_ref[0]
    shift, scale = mod[0:1], mod[1:2]
    h = _rms(x) * g1_ref[...] * (1.0 + scale) + shift
    proj = jnp.dot(h.astype(BF16), win_ref[...], preferred_element_type=F32)
    z_ref[...] = proj[:, COL_Z:COL_XBC]
    xbc_ref[...] = proj[:, COL_XBC:COL_DT]
    dt_ref[...] = proj[:, COL_DT:COL_POOL]
    pool_ref[...] = proj[:, COL_POOL:IN_COLS_PACKED]

    rc, rsa, rsb = rc_ref[...], rsa_ref[...], rsb_ref[...]

    def head_norm_rope(t, gain):
        ms = _row_sumsq(t) * (1.0 / QK_DIM)
        t = t * lax.rsqrt(ms + EPS) * gain
        return (t * rc + pltpu.roll(t, QK_ROPE // 4, axis=1) * rsa
                + pltpu.roll(t, HEAD_PAD - QK_ROPE // 4, axis=1) * rsb)

    qn = (_rms(proj[:, COL_QLAT:COL_KVLAT]) * qg_ref[...]).astype(BF16)
    q_all = jnp.dot(qn, wuq_ref[...], preferred_element_type=F32)
    kvn = (_rms(proj[:, COL_KVLAT:COL_KROPE]) * kvg_ref[...]).astype(BF16)
    k_all = jnp.dot(kvn, wuk_ref[...], preferred_element_type=F32)
    k_rope = proj[:, COL_KROPE:COL_Z]
    vt_all = lax.dot_general(wuv_ref[...], kvn, (((1,), (1,)), ((), ())), preferred_element_type=F32)
    vrow = lax.broadcasted_iota(jnp.int32, vt_all.shape, 0) % V_ROWS
    v_ref[0] = jnp.where(vrow == V_HEAD, 1.0, vt_all).astype(BF16)
    qng, kng = qng_ref[...], kng_ref[...]
    for hd in range(MLA_HEADS):
        sl = slice(hd * HEAD_PAD, (hd + 1) * HEAD_PAD)
        q_ref[:, sl] = (head_norm_rope(q_all[:, sl], qng) * Q_SCALE).astype(BF16)
        k_ref[:, sl] = head_norm_rope(k_all[:, sl] + k_rope, kng).astype(BF16)


def _in_projection(x, moe, modtab, g1, w_in_p, qg, kvg, wuq_p, wuk_p, wuv, qng_p, kng_p, rope_c, rope_sa, rope_sb,
                   n_lat_tiles):
    t_rows, d = x.shape
    nt = pl.cdiv(t_rows, ROW_TILE)
    row = lambda cols: pl.BlockSpec((ROW_TILE, cols), lambda i: (i, 0))
    full = lambda a: pl.BlockSpec(a.shape, lambda i: (0,) * a.ndim)
    seg = lambda: pl.BlockSpec((1, SUBLANES, d), lambda i: (i // n_lat_tiles, 0, 0))
    hw = MLA_HEADS * HEAD_PAD
    per = ATT_TK // ROW_TILE
    vw = MLA_HEADS * V_ROWS
    out_specs = [row(hw), row(hw), pl.BlockSpec((1, vw, ROW_TILE), lambda i: (i // per, 0, i % per)),
                 row(SSD_WIDTH), row(SSD_CONV_DIM), row(LANES), row(POOL_WIDTH)]
    sds = lambda cols, dt: jax.ShapeDtypeStruct((t_rows, cols), dt)
    out_shape = [sds(hw, BF16), sds(hw, BF16), jax.ShapeDtypeStruct((pl.cdiv(nt, per), vw, ATT_TK), BF16),
                 sds(SSD_WIDTH, F32), sds(SSD_CONV_DIM, F32), sds(LANES, F32), sds(POOL_WIDTH, F32)]
    lead_in, lead_specs = [x], [row(d)]
    if moe is not None:
        picked, route, pmodtab = moe
        lead_in += [picked, route, pmodtab]
        lead_specs += [pl.BlockSpec((TOP_K, ROW_TILE, d), lambda i: (0, i, 0)), row(LANES), seg()]
        out_specs = [row(d)] + out_specs
        out_shape = [sds(d, F32)] + out_shape
    return pl.pallas_call(
        functools.partial(_inproj_kernel, has_moe=moe is not None),
        grid=(nt,),
        in_specs=lead_specs + [seg(), full(g1), full(w_in_p), full(qg), full(kvg), full(wuq_p), full(wuk_p),
                               full(wuv), full(qng_p), full(kng_p), row(HEAD_PAD), row(HEAD_PAD), row(HEAD_PAD)],
        out_specs=out_specs,
        out_shape=out_shape,
        compiler_params=_params("arbitrary"),
        name="in_projection",
    )(*lead_in, modtab, g1, w_in_p, qg, kvg, wuq_p, wuk_p, wuv, qng_p, kng_p, rope_c, rope_sa, rope_sb)


def _attention_kernel(q_ref, k_ref, vt_ref, o_ref, sa_ref, sb_ref, st_ref, ca_ref, cb_ref, ct_ref, m_ref, acc_ref,
                      *, n_main, tk, tail):
    m_ref[...] = jnp.full(m_ref.shape, -jnp.inf, F32)
    acc_ref[...] = jnp.zeros(acc_ref.shape, F32)

    def scores(buf, start, size):
        s_ref, cmax_ref = buf
        for hh in range(2):
            q = q_ref[:, hh * HEAD_PAD:(hh + 1) * HEAD_PAD]
            k = k_ref[pl.ds(start, size), hh * HEAD_PAD:(hh + 1) * HEAD_PAD]
            s = lax.dot_general(k, q, (((1,), (1,)), ((), ())), preferred_element_type=F32)
            s_ref[hh] = s
            cmax_ref[hh] = jnp.max(s, axis=0, keepdims=True)

    def consume(buf, c, size):
        s_ref, cmax_ref = buf
        for hh in range(2):
            s = s_ref[hh]
            m_old = m_ref[hh]
            m_new = jnp.maximum(m_old, cmax_ref[hh])
            p = jnp.exp2((s - m_new).astype(BF16))
            vt = vt_ref[c, hh * V_ROWS:(hh + 1) * V_ROWS, 0:size]
            acc_ref[hh] = jnp.exp2(m_old - m_new) * acc_ref[hh] + jnp.dot(vt, p, preferred_element_type=F32)
            m_ref[hh] = m_new

    row0 = lambda c: pl.multiple_of(c * tk, tk)
    t_start, t_size, t_chunk = tail
    sa_ref, sb_ref, st_ref = (sa_ref, ca_ref), (sb_ref, cb_ref), (st_ref, ct_ref)
    scores(st_ref, t_start, t_size)
    if n_main > 0:
        scores(sa_ref, 0, tk)
    consume(st_ref, t_chunk, t_size)
    if n_main > 0:
        def body(j, carry):
            scores(sb_ref, row0(2 * j + 1), tk)
            consume(sa_ref, 2 * j, tk)
            scores(sa_ref, row0(2 * j + 2), tk)
            consume(sb_ref, 2 * j + 1, tk)
            return carry
        lax.fori_loop(0, n_main // 2 - 1, body, 0)
        scores(sb_ref, (n_main - 1) * tk, tk)
        consume(sa_ref, n_main - 2, tk)
        consume(sb_ref, n_main - 1, tk)
    outs = []
    for hh in range(2):
        acc = acc_ref[hh]
        outs.append(acc[0:V_HEAD] / acc[V_HEAD:V_HEAD + 1])
    o_ref[...] = jnp.concatenate(outs, axis=0).T.astype(o_ref.dtype)


def _attention(q, k, vt, n_lat, n_ctx):
    t_rows = q.shape[0]
    pairs = MLA_HEADS // 2
    n_main = n_lat // ATT_TK
    n_chunks = vt.shape[0]
    assert n_main % 2 == 0 and n_main >= 2 and n_chunks == n_main + 1

    def scratch(tq, tk, t_size):
        return ([pltpu.VMEM((2, tk, tq), F32), pltpu.VMEM((2, tk, tq), F32), pltpu.VMEM((2, t_size, tq), F32)]
                + [pltpu.VMEM((2, 1, tq), F32)] * 4 + [pltpu.VMEM((2, V_ROWS, tq), F32)])

    lat = pl.pallas_call(
        functools.partial(_attention_kernel, n_main=n_main, tk=ATT_TK, tail=(n_lat, n_ctx, n_main)),
        grid=(pairs, n_lat // ATT_TQ),
        in_specs=[pl.BlockSpec((ATT_TQ, 2 * HEAD_PAD), lambda p, i: (i, p)),
                  pl.BlockSpec((t_rows, 2 * HEAD_PAD), lambda p, i: (0, p), pipeline_mode=pl.Buffered(1)),
                  pl.BlockSpec((n_chunks, 2 * V_ROWS, ATT_TK), lambda p, i: (0, p, 0),
                               pipeline_mode=pl.Buffered(1))],
        out_specs=pl.BlockSpec((ATT_TQ, 2 * V_HEAD), lambda p, i: (i, p)),
        out_shape=jax.ShapeDtypeStruct((n_lat, MLA_WIDTH), BF16),
        scratch_shapes=scratch(ATT_TQ, ATT_TK, n_ctx),
        compiler_params=_params("arbitrary", "arbitrary"),
        name="attention_latent",
    )(q, k, vt)
    cblk = n_lat // n_ctx
    cspec = lambda: pl.BlockSpec((n_ctx, 2 * HEAD_PAD), lambda p: (cblk, p))
    ctx = pl.pallas_call(
        functools.partial(_attention_kernel, n_main=0, tk=SUBLANES, tail=(0, n_ctx, 0)),
        grid=(pairs,),
        in_specs=[cspec(), cspec(), pl.BlockSpec((1, 2 * V_ROWS, ATT_TK), lambda p: (n_main, p, 0))],
        out_specs=pl.BlockSpec((n_ctx, 2 * V_HEAD), lambda p: (0, p)),
        out_shape=jax.ShapeDtypeStruct((n_ctx, MLA_WIDTH), BF16),
        scratch_shapes=scratch(n_ctx, SUBLANES, n_ctx),
        compiler_params=_params("arbitrary"),
        name="attention_context",
    )(q, k, vt)
    return jnp.concatenate([lat, ctx], axis=0)


def _halo_specs(cols, tile_rows, t_rows):
    per = tile_rows // HALO
    last = t_rows // HALO - 1
    return [pl.BlockSpec((HALO, cols), lambda i: (jnp.maximum(i * per - 1, 0), 0)),
            pl.BlockSpec((tile_rows, cols), lambda i: (i, 0)),
            pl.BlockSpec((HALO, cols), lambda i: (jnp.minimum((i + 1) * per, last), 0))]


def _with_halo(prev_ref, cur_ref, next_ref, i, n_lat_tiles, t_rows):
    has_prev = jnp.logical_and(i != 0, i != n_lat_tiles)
    has_next = jnp.logical_and(i != n_lat_tiles - 1, i != pl.num_programs(0) - 1)
    prev = jnp.where(has_prev, prev_ref[...], 0.0)
    nxt = jnp.where(has_next, next_ref[...], 0.0)
    cur = cur_ref[...]
    row = lax.broadcasted_iota(jnp.int32, cur.shape, 0)
    cur = jnp.where(row < t_rows - i * cur.shape[0], cur, 0.0)
    return jnp.concatenate([prev, cur, nxt], axis=0)


def _conv_kernel(prev_ref, cur_ref, next_ref, w_ref, b_ref, o_ref, *, n_lat_tiles, t_rows):
    i = pl.program_id(0)
    ext = _with_halo(prev_ref, cur_ref, next_ref, i, n_lat_tiles, t_rows)
    rows = ext.shape[0]
    w = w_ref[...]
    y = (pltpu.roll(ext, 2, axis=0) * w[0:1] + pltpu.roll(ext, 1, axis=0) * w[1:2]
         + ext * w[2:3] + pltpu.roll(ext, rows - 1, axis=0) * w[3:4])
    y = y[HALO:rows - HALO] + b_ref[...]
    o_ref[...] = y / (1.0 + jnp.exp(-y))


def _conv(xbc, conv_w, conv_b, n_lat_tiles):
    t_rows, cols = xbc.shape
    nt = pl.cdiv(t_rows, ROW_TILE)
    return pl.pallas_call(
        functools.partial(_conv_kernel, n_lat_tiles=n_lat_tiles, t_rows=t_rows),
        grid=(nt,),
        in_specs=_halo_specs(cols, ROW_TILE, t_rows) + [
            pl.BlockSpec(conv_w.shape, lambda i: (0, 0)), pl.BlockSpec(conv_b.shape, lambda i: (0, 0))],
        out_specs=pl.BlockSpec((ROW_TILE, cols), lambda i: (i, 0)),
        out_shape=jax.ShapeDtypeStruct((t_rows, cols), F32),
        compiler_params=_params("arbitrary"),
        name="ssd_conv",
    )(xbc, xbc, xbc, conv_w, conv_b)


def _softplus(v):
    return jnp.maximum(v, 0.0) + jnp.log(1.0 + jnp.exp(-jnp.abs(v)))


def _ssd_kernel(xf_ref, xb_ref, dtf_ref, dtb_ref, dttf_ref, dttb_ref, bias_ref, biast_ref, alog_ref, alogt_ref,
                dskip_ref, expand_ref, yf_ref, yb_ref, state_ref):
    L = xf_ref.shape[0]
    P, N, H = SSD_HEAD_DIM, SSD_STATE, SSD_HEADS

    @pl.when(pl.program_id(0) == 0)
    def _():
        state_ref[...] = jnp.zeros(state_ref.shape, F32)

    r = lax.broadcasted_iota(jnp.int32, (L, L), 0)
    c = lax.broadcasted_iota(jnp.int32, (L, L), 1)
    lower = r >= c
    upper = r <= c
    lower_f = lower.astype(BF16)
    upper_f = upper.astype(BF16)

    def split3(a):
        a1 = a.astype(BF16)
        r1 = a - a1.astype(F32)
        a2 = r1.astype(BF16)
        return a1, a2, (r1 - a2.astype(F32)).astype(BF16)
    a_row = -jnp.exp(alog_ref[...])
    a_col = -jnp.exp(alogt_ref[...])

    def one_direction(x_ref, dt_ref, dtt_ref, y_ref, base, forward):
        x = x_ref[...]
        dt = _softplus(dt_ref[...] + bias_ref[...])
        dtt = _softplus(dtt_ref[...] + biast_ref[...])
        tri_col = lower_f if forward else upper_f
        tri_row = upper_f if forward else lower_f
        cs = sum(jnp.dot(tri_col, part, preferred_element_type=F32) for part in split3(dt * a_row))
        cst = sum(jnp.dot(part, tri_row, preferred_element_type=F32) for part in split3(dtt * a_col))
        mask = lower if forward else upper
        end = L - 1 if forward else 0
        outs = []
        cb = []
        for g in range(SSD_GROUPS):
            bg = x[:, SSD_WIDTH + g * N:SSD_WIDTH + (g + 1) * N].astype(BF16)
            cg = x[:, SSD_WIDTH + (SSD_GROUPS + g) * N:SSD_WIDTH + (SSD_GROUPS + g + 1) * N].astype(BF16)
            cb.append((bg, cg, lax.dot_general(cg, bg, (((1,), (1,)), ((), ())),
                                               preferred_element_type=F32).astype(BF16)))
        spread = expand_ref[0 if forward else 1]

        def replicate(a):
            hi = a.astype(BF16)
            lo = (a - hi.astype(F32)).astype(BF16)
            return (jnp.dot(hi, spread, preferred_element_type=F32)
                    + jnp.dot(lo, spread, preferred_element_type=F32))

        cs_rep, dt_rep = replicate(cs), replicate(dt)
        for hd in range(H):
            j = base + hd
            bg, cg, cbg = cb[hd // (H // SSD_GROUPS)]
            col = cs_rep[:, hd * LANES:(hd + 1) * LANES]
            rowv = cst[j:j + 1, :]
            total = col[end:end + 1, :]
            gap = jnp.tile(col, (1, L // LANES)) - rowv
            decay = jnp.where(mask, jnp.exp(jnp.minimum(gap, 0.0).astype(BF16)), 0.0)
            xh = (x[:, hd * P:(hd + 1) * P] * dt_rep[:, hd * LANES:hd * LANES + P]).astype(BF16)
            y = jnp.dot(cbg * decay, xh, preferred_element_type=F32)
            st = state_ref[j]
            y = y + jnp.dot(cg, st.astype(BF16), preferred_element_type=F32) * jnp.exp(col[:, :P])
            bw = (bg.astype(F32) * jnp.exp(total[:, :N] - col[:, :N])).astype(BF16)
            state_ref[j] = st * jnp.exp(total[:, :P]) + lax.dot_general(
                bw, xh, (((0,), (0,)), ((), ())), preferred_element_type=F32)
            outs.append(y)
        y_all = jnp.concatenate(outs, axis=1)
        if forward:
            y_all = y_all + x[:, :SSD_WIDTH] * dskip_ref[...]
        y_ref[...] = y_all

    one_direction(xf_ref, dtf_ref, dttf_ref, yf_ref, 0, True)
    one_direction(xb_ref, dtb_ref, dttb_ref, yb_ref, H, False)


def _ssd(xc, dt, dtt, bias_row, bias_col, alog_row, alog_col, dskip_row, n_lat_chunks):
    t_rows, cols = xc.shape
    L = SSD_CHUNK
    nc = t_rows // L
    fwd = lambda j: jnp.where(j == 0, n_lat_chunks, j - 1)
    bwd = lambda j: jnp.where(j == 0, n_lat_chunks, n_lat_chunks - j)
    small = lambda a: pl.BlockSpec(a.shape, lambda j: (0,) * a.ndim)
    r = jnp.arange(LANES)[None, :, None] - jnp.arange(2)[:, None, None] * SSD_HEADS
    spread = (r == jnp.arange(SSD_HEADS * LANES)[None, None, :] // LANES).astype(BF16)
    return pl.pallas_call(
        _ssd_kernel,
        grid=(nc,),
        in_specs=[pl.BlockSpec((L, cols), lambda j: (fwd(j), 0)),
                  pl.BlockSpec((L, cols), lambda j: (bwd(j), 0)),
                  pl.BlockSpec((L, LANES), lambda j: (fwd(j), 0)),
                  pl.BlockSpec((L, LANES), lambda j: (bwd(j), 0)),
                  pl.BlockSpec((2 * SUBLANES, L), lambda j: (0, fwd(j))),
                  pl.BlockSpec((2 * SUBLANES, L), lambda j: (0, bwd(j))),
                  small(bias_row), small(bias_col), small(alog_row), small(alog_col), small(dskip_row),
                  small(spread)],
        out_specs=[pl.BlockSpec((L, SSD_WIDTH), lambda j: (fwd(j), 0)),
                   pl.BlockSpec((L, SSD_WIDTH), lambda j: (bwd(j), 0))],
        out_shape=[jax.ShapeDtypeStruct((t_rows, SSD_WIDTH), F32)] * 2,
        scratch_shapes=[pltpu.VMEM((2 * SSD_HEADS, SSD_STATE, SSD_HEAD_DIM), F32)],
        compiler_params=_params("arbitrary"),
        name="ssd_scan",
    )(xc, xc, dt, dt, dtt, dtt, bias_row, bias_col, alog_row, alog_col, dskip_row, spread)


def _mixout_kernel(x_ref, mod_ref, att_ref, yf_ref, yb_ref, z_ref, pprev_ref, pcur_ref, pnext_ref,
                   ssdg_ref, poolw_ref, pools_ref, wout_ref, g2_ref, rw_ref, rb_ref,
                   x1_ref, h2_ref, route_ref, cnt_ref, base_ref, *, n_lat_tiles, n_lat, n_ctx, t_rows):
    i = pl.program_id(0)
    mod = mod_ref[0]
    z = z_ref[...]
    ssd = _rms((yf_ref[...] + yb_ref[...]) * (z / (1.0 + jnp.exp(-z)))) * ssdg_ref[...]
    ext = _with_halo(pprev_ref, pcur_ref, pnext_ref, i, n_lat_tiles, t_rows)
    rows = ext.shape[0]
    tm = rows - 2 * HALO
    w2 = ext + pltpu.roll(ext, 1, axis=0)
    w4 = pltpu.roll(w2, 1, axis=0) + pltpu.roll(w2, rows - 1, axis=0)
    w8 = pltpu.roll(w4, 2, axis=0) + pltpu.roll(w4, rows - 2, axis=0)
    w16 = pltpu.roll(w8, 4, axis=0) + pltpu.roll(w8, rows - 4, axis=0)
    lane = lax.broadcasted_iota(jnp.int32, (tm, POOL_WIDTH), 1)
    grp = lane // POOL_GROUP
    sl = slice(HALO, rows - HALO)
    wsum = jnp.where(grp == 0, w2[sl], jnp.where(grp == 1, w4[sl], jnp.where(grp == 2, w8[sl], w16[sl])))
    is_ctx = i >= n_lat_tiles
    seg_len = jnp.where(is_ctx, n_ctx, n_lat)
    t = lax.broadcasted_iota(jnp.int32, (tm, POOL_WIDTH), 0) + jnp.where(is_ctx, i - n_lat_tiles, i) * tm
    half = jnp.left_shift(1, grp)
    lo = jnp.clip(t - half, 0, seg_len)
    hi = jnp.clip(t + half, 0, seg_len)
    p = wsum / (hi - lo).astype(F32) - pcur_ref[...]
    pool = jnp.dot(p.astype(BF16), poolw_ref[...], preferred_element_type=F32) * pools_ref[...]
    mix = (jnp.dot(att_ref[...], wout_ref[0:MLA_WIDTH], preferred_element_type=F32)
           + jnp.dot(ssd.astype(BF16), wout_ref[MLA_WIDTH:MLA_WIDTH + SSD_WIDTH], preferred_element_type=F32)
           + jnp.dot(pool.astype(BF16), wout_ref[MLA_WIDTH + SSD_WIDTH:], preferred_element_type=F32))
    x1 = x_ref[...] + mod[2:3] * mix
    x1_ref[...] = x1
    h2 = _rms(x1) * g2_ref[...] * (1.0 + mod[4:5]) + mod[3:4]
    h2b = h2.astype(BF16)
    for kk in range(TOP_K):
        h2_ref[kk] = h2b
    h2_lo = (h2 - h2b.astype(F32)).astype(BF16)
    logits = (jnp.dot(h2b, rw_ref[0], preferred_element_type=F32)
              + jnp.dot(h2_lo, rw_ref[0], preferred_element_type=F32)
              + jnp.dot(h2b, rw_ref[1], preferred_element_type=F32)) + rb_ref[...]

    @pl.when(i == 0)
    def _():
        base_ref[...] = jnp.zeros(base_ref.shape, F32)

    elane = lax.broadcasted_iota(jnp.int32, logits.shape, 1)
    row_ok = lax.broadcasted_iota(jnp.int32, logits.shape, 0) < t_rows - i * tm
    lg = jnp.where(elane < N_EXPERTS, logits, -jnp.inf)
    rr = lax.broadcasted_iota(jnp.int32, (tm, tm), 0)
    cc = lax.broadcasted_iota(jnp.int32, (tm, tm), 1)
    earlier = (rr > cc).astype(BF16)
    offset = base_ref[...]
    tops, ids, ranks = [], [], []
    for kk in range(TOP_K):
        top = jnp.max(lg, axis=-1, keepdims=True)
        idx = jnp.min(jnp.where(lg == top, elane, LANES), axis=-1, keepdims=True)
        sel = jnp.logical_and(elane == idx, row_ok)
        lg = jnp.where(sel, -jnp.inf, lg)
        onehot = sel.astype(BF16)
        before = jnp.dot(earlier, onehot, preferred_element_type=F32) + offset
        ranks.append(jnp.sum(jnp.where(sel, before, 0.0), axis=-1, keepdims=True))
        offset = offset + jnp.sum(sel.astype(F32), axis=0, keepdims=True)
        tops.append(top)
        ids.append(idx.astype(F32))
    base_ref[...] = offset
    cnt_ref[...] = jnp.broadcast_to(offset, cnt_ref.shape)
    exps = [jnp.exp(tp - tops[0]) for tp in tops]
    denom = exps[0] + exps[1] + exps[2] + exps[3]
    route = jnp.zeros(logits.shape, F32)
    for kk in range(TOP_K):
        route = jnp.where(elane == kk, ids[kk], route)
        route = jnp.where(elane == TOP_K + kk, exps[kk] / denom, route)
        route = jnp.where(elane == 2 * TOP_K + kk, ranks[kk], route)
    route_ref[...] = route


def _mixer_output(x, modtab, att, yf, yb, z, pool_in, ssdg, poolw_bd, pools, wout, g2, rw_p, rb_p,
                  n_lat_tiles, n_lat, n_ctx):
    t_rows, d = x.shape
    nt = pl.cdiv(t_rows, ROW_TILE)
    row = lambda cols: pl.BlockSpec((ROW_TILE, cols), lambda i: (i, 0))
    full = lambda a: pl.BlockSpec(a.shape, lambda i: (0,) * a.ndim)
    return pl.pallas_call(
        functools.partial(_mixout_kernel, n_lat_tiles=n_lat_tiles, n_lat=n_lat, n_ctx=n_ctx, t_rows=t_rows),
        grid=(nt,),
        in_specs=[row(d), pl.BlockSpec((1, SUBLANES, d), lambda i: (i // n_lat_tiles, 0, 0)),
                  row(MLA_WIDTH), row(SSD_WIDTH), row(SSD_WIDTH), row(SSD_WIDTH)]
                 + _halo_specs(POOL_WIDTH, ROW_TILE, t_rows)
                 + [full(ssdg), full(poolw_bd), full(pools), full(wout), full(g2), full(rw_p), full(rb_p)],
        out_specs=[row(d), pl.BlockSpec((TOP_K, ROW_TILE, d), lambda i: (0, i, 0)), row(LANES),
                   pl.BlockSpec((SUBLANES, LANES), lambda i: (0, 0))],
        out_shape=[jax.ShapeDtypeStruct((t_rows, d), F32), jax.ShapeDtypeStruct((TOP_K, t_rows, d), BF16),
                   jax.ShapeDtypeStruct((t_rows, LANES), F32), jax.ShapeDtypeStruct((SUBLANES, LANES), F32)],
        scratch_shapes=[pltpu.VMEM((1, LANES), F32)],
        compiler_params=_params("arbitrary"),
        name="mixer_output",
    )(x, modtab, att, yf, yb, z, pool_in, pool_in, pool_in, ssdg, poolw_bd, pools, wout, g2, rw_p, rb_p)


def _expert_kernel(be_ref, nu_ref, x_ref, wg_ref, bg_ref, wu_ref, bu_ref, wd_ref, bd_ref, *rest, first_block):
    o_ref, wg_s, wu_s, wd_s = rest[-4:]
    i = pl.program_id(0)
    blk = i + first_block

    @pl.when(jnp.logical_or(i == 0, be_ref[blk] != be_ref[jnp.maximum(blk - 1, 0)]))
    def _():
        wg_s[...] = wg_ref[0, 0].astype(BF16)
        wu_s[...] = wu_ref[0, 0].astype(BF16)
        wd_s[...] = wd_ref[0, 0].astype(BF16)

    @pl.when(blk < nu_ref[0])
    def _():
        x = x_ref[...]
        g = jnp.minimum(jnp.dot(x, wg_s[...], preferred_element_type=F32) + bg_ref[0, 0], SWIGLU_LIMIT)
        u = jnp.clip(jnp.dot(x, wu_s[...], preferred_element_type=F32) + bu_ref[0, 0],
                     -SWIGLU_LIMIT, SWIGLU_LIMIT)
        a = g / (1.0 + jnp.exp(-SWIGLU_ALPHA * g)) * (u + 1.0)
        y = jnp.dot(a.astype(BF16), wd_s[...], preferred_element_type=F32) + bd_ref[0, 0]
        o_ref[...] = y.astype(o_ref.dtype)

    @pl.when(blk >= nu_ref[0])
    def _():
        o_ref[...] = jnp.zeros(o_ref.shape, o_ref.dtype)


def _experts(layer, x_part, first_block, n_rows, out_prev, block_expert, n_used, wg, bg, wu, bu, wd, bd):
    d = x_part.shape[1]
    depth, ne, _, dff = wg.shape
    wspec = lambda a, b: pl.BlockSpec((1, 1, a, b), lambda i, be, nu: (layer, be[i + first_block], 0, 0))
    in_specs = [pl.BlockSpec((MOE_TILE, d), lambda i, be, nu: (i, 0)),
                wspec(d, dff), wspec(1, dff), wspec(d, dff), wspec(1, dff), wspec(dff, d), wspec(1, d)]
    operands = [block_expert, n_used, x_part, wg, bg.reshape(depth, ne, 1, dff), wu, bu.reshape(depth, ne, 1, dff),
                wd, bd.reshape(depth, ne, 1, d)]
    aliases = {}
    if out_prev is not None:
        in_specs.append(pl.BlockSpec(memory_space=pl.ANY))
        aliases = {len(operands): 0}
        operands.append(out_prev)
    grid_spec = pltpu.PrefetchScalarGridSpec(
        num_scalar_prefetch=2,
        grid=(x_part.shape[0] // MOE_TILE,),
        in_specs=in_specs,
        out_specs=pl.BlockSpec((MOE_TILE, d), lambda i, be, nu: (i + first_block, 0)),
        scratch_shapes=[pltpu.VMEM((d, dff), BF16), pltpu.VMEM((d, dff), BF16), pltpu.VMEM((dff, d), BF16)],
    )
    return pl.pallas_call(
        functools.partial(_expert_kernel, first_block=first_block),
        grid_spec=grid_spec,
        out_shape=jax.ShapeDtypeStruct((n_rows, d), BF16),
        input_output_aliases=aliases,
        compiler_params=_params("arbitrary"),
        name="moe_experts",
    )(*operands)


def _dispatch_plan(route, counts_f):
    t_rows = route.shape[0]
    n_assign = t_rows * TOP_K
    ids = route[:, 0:TOP_K].astype(jnp.int32)
    rank =route[:, 2 * TOP_K:3 * TOP_K].astype(jnp.int32)
    counts = counts_f[0, :N_EXPERTS].astype(jnp.int32)
    padded = (counts + MOE_TILE - 1) // MOE_TILE * MOE_TILE
    pad_end = jnp.cumsum(padded)
    pad_start = pad_end - padded
    experts = jnp.arange(N_EXPERTS, dtype=jnp.int32)
    first_row = jnp.sum(jnp.where(ids[:, :, None] == experts[None, None, :], pad_start[None, None, :], 0), axis=-1)
    dest = (first_row + rank).T.reshape(-1)
    n_rows = n_assign + N_EXPERTS * MOE_TILE
    n_blocks = n_rows // MOE_TILE
    block_expert = jnp.minimum(
        jnp.sum((jnp.arange(n_blocks, dtype=jnp.int32)[:, None] * MOE_TILE >= pad_end[None, :]).astype(jnp.int32),
                axis=1), N_EXPERTS - 1).astype(jnp.int32)
    n_pad = n_rows - n_assign
    pads = jnp.concatenate([padded - counts, n_rows - pad_end[-1:]])
    pads_end = jnp.cumsum(pads)
    pad_first = jnp.concatenate([pad_start + counts, pad_end[-1:]])
    j = jnp.arange(n_pad, dtype=jnp.int32)
    grp = jnp.sum((j[:, None] >= pads_end[None, :]).astype(jnp.int32), axis=1)
    hit = grp[:, None] == jnp.arange(N_EXPERTS + 1, dtype=jnp.int32)[None, :]
    pad_rows = j + jnp.sum(jnp.where(hit, (pad_first - (pads_end - pads))[None, :], 0), axis=1)
    keys = jnp.concatenate([dest, pad_rows]).astype(jnp.int32)
    vals = jnp.concatenate([jnp.arange(n_assign, dtype=jnp.int32), j])
    row_src = lax.sort((keys, vals), num_keys=1)[1]
    n_used = (pad_end[-1:] // MOE_TILE).astype(jnp.int32)
    return dest, row_src, block_expert, n_used


def _rope_tables(n_lat, n_ctx):
    rows = n_lat // GRID_W
    row = jnp.repeat(jnp.arange(rows), GRID_W).astype(F32)
    col = jnp.tile(jnp.arange(GRID_W), rows).astype(F32)
    n_freq = QK_ROPE // 4
    inv_freq = ROPE_THETA ** (-jnp.arange(n_freq, dtype=F32) / n_freq)
    ang = jnp.stack([row[:, None] * inv_freq, col[:, None] * inv_freq], axis=1)
    cos, sin = jnp.cos(ang), jnp.sin(ang)
    zero = jnp.zeros_like(sin)
    ones = lambda w: jnp.ones((n_lat, w), F32)
    zeros = lambda w: jnp.zeros((n_lat, w), F32)
    per_axis = lambda a, b: jnp.stack([a, b], axis=2).reshape(n_lat, QK_ROPE)
    tail = HEAD_PAD - QK_DIM
    c = jnp.concatenate([ones(QK_NOPE), per_axis(cos, cos), ones(tail)], axis=1)
    sa = jnp.concatenate([zeros(QK_NOPE), per_axis(zero, sin), zeros(tail)], axis=1)
    sb = jnp.concatenate([zeros(QK_NOPE), per_axis(-sin, zero), zeros(tail)], axis=1)
    pad = lambda a, v: jnp.concatenate([a, jnp.full((n_ctx, HEAD_PAD), v, F32)], axis=0)
    return pad(c, 1.0), pad(sa, 0.0), pad(sb, 0.0)


def _pad_cols(a, width):
    return jnp.pad(a, ((0, 0), (0, width - a.shape[1])))


def _split_bf16(a):
    hi = a.astype(BF16)
    return jnp.stack([hi, (a - hi.astype(F32)).astype(BF16)])


def _pack_layer(i, w_in, q_g, kv_g, w_uq, w_ukv, qn_g, kn_g, conv_w, conv_b, a_log, dt_bias, d_skip, ssd_g,
                pool_w, pool_scale, w_out, router_w, router_b):
    d = w_in.shape[1]
    pts = [0, Q_LORA, KV_LORA, QK_ROPE, SSD_WIDTH, SSD_CONV_DIM, 2 * SSD_HEADS, POOL_WIDTH]
    offs = [sum(pts[:j + 1]) for j in range(len(pts))]
    seg = lambda j: w_in[i][:, offs[j]:offs[j + 1]]
    zc = lambda w: jnp.zeros((d, w), F32)
    w_in_p = jnp.concatenate([
        seg(0), seg(1), zc(QK_NOPE), seg(2), zc(LANES - QK_DIM), seg(3), seg(4),
        seg(5), zc(LANES - 2 * SSD_HEADS), seg(6)], axis=1).astype(BF16)
    wuq = w_uq[i].reshape(Q_LORA, MLA_HEADS, QK_DIM)
    wuq_p = jnp.pad(wuq, ((0, 0), (0, 0), (0, HEAD_PAD - QK_DIM))).reshape(Q_LORA, MLA_HEADS * HEAD_PAD)
    wukv = w_ukv[i].reshape(KV_LORA, MLA_HEADS, QK_NOPE + V_HEAD)
    wuk_p = jnp.pad(wukv[:, :, :QK_NOPE], ((0, 0), (0, 0), (0, HEAD_PAD - QK_NOPE))).reshape(
        KV_LORA, MLA_HEADS * HEAD_PAD)
    wuv = jnp.pad(wukv[:, :, QK_NOPE:], ((0, 0), (0, 0), (0, V_ROWS - V_HEAD))).reshape(
        KV_LORA, MLA_HEADS * V_ROWS).T
    flat12 = lambda a: a[i].reshape(1, 2 * SSD_HEADS)
    poolw_bd = jax.scipy.linalg.block_diag(*[pool_w[i][g] for g in range(len(POOL_WINDOWS))])
    return dict(
        w_in_p=w_in_p, qg=q_g[i][None], kvg=kv_g[i][None],
        wuq_p=wuq_p.astype(BF16), wuk_p=wuk_p.astype(BF16), wuv=wuv.astype(BF16),
        qng_p=_pad_cols(qn_g[i][None], HEAD_PAD), kng_p=_pad_cols(kn_g[i][None], HEAD_PAD),
        conv_w=conv_w[i], conv_b=conv_b[i][None],
        bias_row=_pad_cols(flat12(dt_bias), LANES), bias_col=_pad_cols(flat12(dt_bias), 2 * SUBLANES).T,
        alog_row=_pad_cols(flat12(a_log), LANES), alog_col=_pad_cols(flat12(a_log), 2 * SUBLANES).T,
        dskip_row=jnp.repeat(d_skip[i], SSD_HEAD_DIM)[None],
        ssdg=ssd_g[i][None], poolw_bd=poolw_bd.astype(BF16), pools=pool_scale[i][None],
        wout=w_out[i].astype(BF16), rw_p=_split_bf16(_pad_cols(router_w[i], LANES)),
        rb_p=_pad_cols(router_b[i][None], LANES))


def kernel(x, c, ctx, c_ctx, norm1_g, norm2_g, mod_w, mod_b, w_in, mla_q_norm_g, mla_kv_norm_g, mla_w_uq, mla_w_ukv, mla_qn_g, mla_kn_g, ssd_conv_w, ssd_conv_b, ssd_a_log, ssd_dt_bias, ssd_d, ssd_norm_g, pool_w, pool_scale, w_out, router_w, router_b, moe_w_gate, moe_b_gate, moe_w_up, moe_b_up, moe_w_down, moe_b_down):
    batch, n_lat, d = x.shape
    n_ctx = ctx.shape[1]
    depth = mod_w.shape[0]
    assert batch == 1 and d == D_MODEL and n_ctx == SSD_CHUNK and n_ctx <= ROW_TILE and n_lat % ROW_TILE == 0
    assert n_lat % ATT_TQ == 0 and n_lat % ATT_TK == 0 and n_lat % GRID_W == 0
    n_lat_tiles = n_lat // ROW_TILE

    cvec = jnp.zeros((SUBLANES, d), F32).at[0].set(c[0]).at[1].set(c_ctx)
    mods = _modulation(cvec, mod_w, mod_b)
    rope_c, rope_sa, rope_sb = _rope_tables(n_lat, n_ctx)

    xs = jnp.concatenate([x[0], ctx[0]], axis=0)
    moe = None
    for i in range(depth):
        lp = _pack_layer(i, w_in, mla_q_norm_g, mla_kv_norm_g, mla_w_uq, mla_w_ukv, mla_qn_g, mla_kn_g,
                         ssd_conv_w, ssd_conv_b, ssd_a_log, ssd_dt_bias, ssd_d, ssd_norm_g, pool_w, pool_scale,
                         w_out, router_w, router_b)
        modtab = jnp.pad(mods[i, :2].reshape(2, 6, d), ((0, 0), (0, SUBLANES - 6), (0, 0)))
        outs = _in_projection(
            xs, moe, modtab, norm1_g[i][None], lp['w_in_p'], lp['qg'], lp['kvg'], lp['wuq_p'], lp['wuk_p'], lp['wuv'],
            lp['qng_p'], lp['kng_p'], rope_c, rope_sa, rope_sb, n_lat_tiles)
        if moe is not None:
            xs, outs = outs[0], outs[1:]
        q, k, v, z, xbc, dt, pool_in = outs
        att = _attention(q, k, v, n_lat, n_ctx)
        xc = _conv(xbc, lp['conv_w'], lp['conv_b'], n_lat_tiles)
        dtt = dt[:, :2 * SUBLANES].T
        yf, yb = _ssd(xc, dt, dtt, lp['bias_row'], lp['bias_col'], lp['alog_row'], lp['alog_col'],
                      lp['dskip_row'], n_lat // SSD_CHUNK)
        x1, h2_slots, route, counts = _mixer_output(
            xs, modtab, att, yf, yb, z, pool_in, lp['ssdg'], lp['poolw_bd'], lp['pools'], lp['wout'],
            norm2_g[i][None], lp['rw_p'], lp['rb_p'], n_lat_tiles, n_lat, n_ctx)
        dest, row_src, block_expert, n_used = _dispatch_plan(route, counts)
        n_rows = row_src.shape[0]
        half = n_rows // MOE_TILE // 2 * MOE_TILE
        h2_flat = h2_slots.reshape(-1, d)
        out = None
        for lo, hi in ((0, half), (half, n_rows)):
            x_part = h2_flat.at[row_src[lo:hi]].get(mode='promise_in_bounds')
            out = _experts(i, x_part, lo // MOE_TILE, n_rows, out, block_expert, n_used,
                           moe_w_gate, moe_b_gate, moe_w_up, moe_b_up, moe_w_down, moe_b_down)
        picked = out.at[dest].get(mode='promise_in_bounds', unique_indices=True)
        xs, moe = x1, (picked.reshape(TOP_K, xs.shape[0], d), route, modtab)
    return _final_combine(xs, *moe, n_lat)[None]
```

```python
import functools

import jax
import jax.numpy as jnp
from jax import lax
from jax.experimental import pallas as pl
from jax.experimental.pallas import tpu as pltpu

F32 = jnp.float32
BF16 = jnp.bfloat16
HIGHEST = lax.Precision.HIGHEST

D_MODEL = 1024
GRID_W = 64
EPS = 1e-6
MLA_HEADS = 6
QK_NOPE = 64
QK_ROPE = 32
QK_DIM = QK_NOPE + QK_ROPE
V_HEAD = 64
Q_LORA = 256
KV_LORA = 128
MLA_WIDTH = MLA_HEADS * V_HEAD
ROPE_THETA = 10000.0
SSD_HEADS = 6
SSD_HEAD_DIM = 64
SSD_WIDTH = SSD_HEADS * SSD_HEAD_DIM
SSD_GROUPS = 2
SSD_STATE = 64
SSD_CONV = 4
SSD_CONV_DIM = SSD_WIDTH + 2 * SSD_GROUPS * SSD_STATE
POOL_WINDOWS = (2, 4, 8, 16)
POOL_GROUP = 64
POOL_WIDTH = len(POOL_WINDOWS) * POOL_GROUP
N_EXPERTS = 32
TOP_K = 4
D_FF = 1024
SWIGLU_LIMIT = 7.0
SWIGLU_ALPHA = 1.702

LANES = 128
SUBLANES = 8
HEAD_PAD = LANES
V_ROWS = HEAD_PAD
Q_SCALE = QK_DIM ** -0.5 * 1.4426950408889634

ROW_TILE = 512
SSD_CHUNK = 256
HALO = SUBLANES
ATT_TQ = 1024
ATT_TK = 512
MOE_TILE = 512
VMEM_LIMIT = 56 * 1024 * 1024

COL_QLAT = 0
COL_KVLAT = COL_QLAT + Q_LORA
COL_KROPE = COL_KVLAT + KV_LORA
COL_Z = COL_KROPE + LANES
COL_XBC = COL_Z + SSD_WIDTH
COL_DT = COL_XBC + SSD_CONV_DIM
COL_POOL = COL_DT + LANES
IN_COLS_PACKED = COL_POOL + POOL_WIDTH


def _row_sumsq(x):
    return jnp.dot((x * x).astype(BF16), jnp.ones((x.shape[1], LANES), BF16), preferred_element_type=F32)


def _rms(x):
    width = x.shape[1]
    inv = lax.rsqrt(_row_sumsq(x) * (1.0 / width) + EPS)
    return x * jnp.tile(inv, (1, width // LANES))


def _params(*sem):
    return pltpu.CompilerParams(dimension_semantics=sem, vmem_limit_bytes=VMEM_LIMIT)


def _mod_kernel(c_ref, w_ref, b_ref, o_ref):
    c = c_ref[...]
    s = c / (1.0 + jnp.exp(-c))
    o_ref[0] = jnp.dot(s, w_ref[0], precision=HIGHEST, preferred_element_type=F32) + b_ref[0]


def _modulation(cvec, mod_w, mod_b):
    depth, d, cols = mod_w.shape
    tn = 1536
    return pl.pallas_call(
        _mod_kernel,
        grid=(depth, cols // tn),
        in_specs=[pl.BlockSpec((SUBLANES, d), lambda l, j: (0, 0)),
                  pl.BlockSpec((1, d, tn), lambda l, j: (l, 0, j)),
                  pl.BlockSpec((1, 1, tn), lambda l, j: (l, 0, j))],
        out_specs=pl.BlockSpec((1, SUBLANES, tn), lambda l, j: (l, 0, j)),
        out_shape=jax.ShapeDtypeStruct((depth, SUBLANES, cols), F32),
        compiler_params=_params("arbitrary", "arbitrary"),
        name="modulation",
    )(cvec, mod_w, mod_b.reshape(depth, 1, cols))


def _moe_combine(x1_ref, picked_ref, route_ref, pmod_ref):
    route = route_ref[...]
    y = picked_ref[0].astype(F32) * route[:, TOP_K:TOP_K + 1]
    for kk in range(1, TOP_K):
        y = y + picked_ref[kk].astype(F32) * route[:, TOP_K + kk:TOP_K + kk + 1]
    return x1_ref[...] + pmod_ref[0][5:6] * y


def _combine_kernel(x1_ref, picked_ref, route_ref, pmod_ref, o_ref):
    o_ref[...] = _moe_combine(x1_ref, picked_ref, route_ref, pmod_ref)


def _final_combine(x1, picked, route, pmodtab, n_lat):
    d = x1.shape[1]
    row = lambda cols: pl.BlockSpec((ROW_TILE, cols), lambda i: (i, 0))
    return pl.pallas_call(
        _combine_kernel,
        grid=(n_lat // ROW_TILE,),
        in_specs=[row(d), pl.BlockSpec((TOP_K, ROW_TILE, d), lambda i: (0, i, 0)), row(LANES),
                  pl.BlockSpec((1, SUBLANES, d), lambda i: (0, 0, 0))],
        out_specs=row(d),
        out_shape=jax.ShapeDtypeStruct((n_lat, d), F32),
        compiler_params=_params("arbitrary"),
        name="moe_combine",
    )(x1, picked, route, pmodtab)


def _inproj_kernel(*refs, has_moe):
    if has_moe:
        x1_ref, picked_ref, route_ref, pmod_ref = refs[:4]
        refs = refs[4:]
    else:
        x_ref = refs[0]
        refs = refs[1:]
    (mod_ref, g1_ref, win_ref, qg_ref, kvg_ref, wuq_ref, wuk_ref, wuv_ref, qng_ref, kng_ref,
     rc_ref, rsa_ref, rsb_ref) = refs[:13]
    outs = refs[13:]
    if has_moe:
        xs_ref, outs = outs[0], outs[1:]
        x = _moe_combine(x1_ref, picked_ref, route_ref, pmod_ref)
        xs_ref[...] = x
    else:
        x = x_ref[...]
    q_ref, k_ref, v_ref, z_ref, xbc_ref, dt_ref, pool_ref = outs
    mod = mod_ref[0]
    shift, scale = mod[0:1], mod[1:2]
    h = _rms(x) * g1_ref[...] * (1.0 + scale) + shift
    proj = jnp.dot(h.astype(BF16), win_ref[...], preferred_element_type=F32)
    z_ref[...] = proj[:, COL_Z:COL_XBC]
    xbc_ref[...] = proj[:, COL_XBC:COL_DT]
    dt_ref[...] = proj[:, COL_DT:COL_POOL]
    pool_ref[...] = proj[:, COL_POOL:IN_COLS_PACKED]

    rc, rsa, rsb = rc_ref[...], rsa_ref[...], rsb_ref[...]

    def head_norm_rope(t, gain):
        ms = _row_sumsq(t) * (1.0 / QK_DIM)
        t = t * lax.rsqrt(ms + EPS) * gain
        return (t * rc + pltpu.roll(t, QK_ROPE // 4, axis=1) * rsa
                + pltpu.roll(t, HEAD_PAD - QK_ROPE // 4, axis=1) * rsb)

    qn = (_rms(proj[:, COL_QLAT:COL_KVLAT]) * qg_ref[...]).astype(BF16)
    q_all = jnp.dot(qn, wuq_ref[...], preferred_element_type=F32)
    kvn = (_rms(proj[:, COL_KVLAT:COL_KROPE]) * kvg_ref[...]).astype(BF16)
    k_all = jnp.dot(kvn, wuk_ref[...], preferred_element_type=F32)
    k_rope = proj[:, COL_KROPE:COL_Z]
    vt_all = lax.dot_general(wuv_ref[...], kvn, (((1,), (1,)), ((), ())), preferred_element_type=F32)
    vrow = lax.broadcasted_iota(jnp.int32, vt_all.shape, 0) % V_ROWS
    v_ref[0] = jnp.where(vrow == V_HEAD, 1.0, vt_all).astype(BF16)
    qng, kng = qng_ref[...], kng_ref[...]
    for hd in range(MLA_HEADS):
        sl = slice(hd * HEAD_PAD, (hd + 1) * HEAD_PAD)
        q_ref[:, sl] = (head_norm_rope(q_all[:, sl], qng) * Q_SCALE).astype(BF16)
        k_ref[:, sl] = head_norm_rope(k_all[:, sl] + k_rope, kng).astype(BF16)


def _in_projection(x, moe, modtab, g1, w_in_p, qg, kvg, wuq_p, wuk_p, wuv, qng_p, kng_p, rope_c, rope_sa, rope_sb,
                   n_lat_tiles):
    t_rows, d = x.shape
    nt = pl.cdiv(t_rows, ROW_TILE)
    row = lambda cols: pl.BlockSpec((ROW_TILE, cols), lambda i: (i, 0))
    full = lambda a: pl.BlockSpec(a.shape, lambda i: (0,) * a.ndim)
    seg = lambda: pl.BlockSpec((1, SUBLANES, d), lambda i: (i // n_lat_tiles, 0, 0))
    hw = MLA_HEADS * HEAD_PAD
    per = ATT_TK // ROW_TILE
    vw = MLA_HEADS * V_ROWS
    out_specs = [row(hw), row(hw), pl.BlockSpec((1, vw, ROW_TILE), lambda i: (i // per, 0, i % per)),
                 row(SSD_WIDTH), row(SSD_CONV_DIM), row(LANES), row(POOL_WIDTH)]
    sds = lambda cols, dt: jax.ShapeDtypeStruct((t_rows, cols), dt)
    out_shape = [sds(hw, BF16), sds(hw, BF16), jax.ShapeDtypeStruct((pl.cdiv(nt, per), vw, ATT_TK), BF16),
                 sds(SSD_WIDTH, F32), sds(SSD_CONV_DIM, F32), sds(LANES, F32), sds(POOL_WIDTH, F32)]
    lead_in, lead_specs = [x], [row(d)]
    if moe is not None:
        picked, route, pmodtab = moe
        lead_in += [picked, route, pmodtab]
        lead_specs += [pl.BlockSpec((TOP_K, ROW_TILE, d), lambda i: (0, i, 0)), row(LANES), seg()]
        out_specs = [row(d)] + out_specs
        out_shape = [sds(d, F32)] + out_shape
    return pl.pallas_call(
        functools.partial(_inproj_kernel, has_moe=moe is not None),
        grid=(nt,),
        in_specs=lead_specs + [seg(), full(g1), full(w_in_p), full(qg), full(kvg), full(wuq_p), full(wuk_p),
                               full(wuv), full(qng_p), full(kng_p), row(HEAD_PAD), row(HEAD_PAD), row(HEAD_PAD)],
        out_specs=out_specs,
        out_shape=out_shape,
        compiler_params=_params("arbitrary"),
        name="in_projection",
    )(*lead_in, modtab, g1, w_in_p, qg, kvg, wuq_p, wuk_p, wuv, qng_p, kng_p, rope_c, rope_sa, rope_sb)


def _attention_kernel(q_ref, k_ref, vt_ref, o_ref, sa_ref, sb_ref, st_ref, ca_ref, cb_ref, ct_ref, m_ref, acc_ref,
                      *, n_main, tk, tail):
    m_ref[...] = jnp.full(m_ref.shape, -jnp.inf, F32)
    acc_ref[...] = jnp.zeros(acc_ref.shape, F32)

    def scores(buf, start, size):
        s_ref, cmax_ref = buf
        for hh in range(2):
            q = q_ref[:, hh * HEAD_PAD:(hh + 1) * HEAD_PAD]
            k = k_ref[pl.ds(start, size), hh * HEAD_PAD:(hh + 1) * HEAD_PAD]
            s = lax.dot_general(k, q, (((1,), (1,)), ((), ())), preferred_element_type=F32)
            s_ref[hh] = s
            cmax_ref[hh] = jnp.max(s, axis=0, keepdims=True)

    def consume(buf, c, size):
        s_ref, cmax_ref = buf
        for hh in range(2):
            s = s_ref[hh]
            m_old = m_ref[hh]
            m_new = jnp.maximum(m_old, cmax_ref[hh])
            p = jnp.exp2((s - m_new).astype(BF16))
            vt = vt_ref[c, hh * V_ROWS:(hh + 1) * V_ROWS, 0:size]
            acc_ref[hh] = jnp.exp2(m_old - m_new) * acc_ref[hh] + jnp.dot(vt, p, preferred_element_type=F32)
            m_ref[hh] = m_new

    row0 = lambda c: pl.multiple_of(c * tk, tk)
    t_start, t_size, t_chunk = tail
    sa_ref, sb_ref, st_ref = (sa_ref, ca_ref), (sb_ref, cb_ref), (st_ref, ct_ref)
    scores(st_ref, t_start, t_size)
    if n_main > 0:
        scores(sa_ref, 0, tk)
    consume(st_ref, t_chunk, t_size)
    if n_main > 0:
        def body(j, carry):
            scores(sb_ref, row0(2 * j + 1), tk)
            consume(sa_ref, 2 * j, tk)
            scores(sa_ref, row0(2 * j + 2), tk)
            consume(sb_ref, 2 * j + 1, tk)
            return carry
        lax.fori_loop(0, n_main // 2 - 1, body, 0)
        scores(sb_ref, (n_main - 1) * tk, tk)
        consume(sa_ref, n_main - 2, tk)
        consume(sb_ref, n_main - 1, tk)
    outs = []
    for hh in range(2):
        acc = acc_ref[hh]
        outs.append(acc[0:V_HEAD] / acc[V_HEAD:V_HEAD + 1])
    o_ref[...] = jnp.concatenate(outs, axis=0).T.astype(o_ref.dtype)


def _attention(q, k, vt, n_lat, n_ctx):
    t_rows = q.shape[0]
    pairs = MLA_HEADS // 2
    n_main = n_lat // ATT_TK
    n_chunks = vt.shape[0]
    assert n_main % 2 == 0 and n_main >= 2 and n_chunks == n_main + 1

    def scratch(tq, tk, t_size):
        return ([pltpu.VMEM((2, tk, tq), F32), pltpu.VMEM((2, tk, tq), F32), pltpu.VMEM((2, t_size, tq), F32)]
                + [pltpu.VMEM((2, 1, tq), F32)] * 4 + [pltpu.VMEM((2, V_ROWS, tq), F32)])

    lat = pl.pallas_call(
        functools.partial(_attention_kernel, n_main=n_main, tk=ATT_TK, tail=(n_lat, n_ctx, n_main)),
        grid=(pairs, n_lat // ATT_TQ),
        in_specs=[pl.BlockSpec((ATT_TQ, 2 * HEAD_PAD), lambda p, i: (i, p)),
                  pl.BlockSpec((t_rows, 2 * HEAD_PAD), lambda p, i: (0, p), pipeline_mode=pl.Buffered(1)),
                  pl.BlockSpec((n_chunks, 2 * V_ROWS, ATT_TK), lambda p, i: (0, p, 0),
                               pipeline_mode=pl.Buffered(1))],
        out_specs=pl.BlockSpec((ATT_TQ, 2 * V_HEAD), lambda p, i: (i, p)),
        out_shape=jax.ShapeDtypeStruct((n_lat, MLA_WIDTH), BF16),
        scratch_shapes=scratch(ATT_TQ, ATT_TK, n_ctx),
        compiler_params=_params("arbitrary", "arbitrary"),
        name="attention_latent",
    )(q, k, vt)
    cblk = n_lat // n_ctx
    cspec = lambda: pl.BlockSpec((n_ctx, 2 * HEAD_PAD), lambda p: (cblk, p))
    ctx = pl.pallas_call(
        functools.partial(_attention_kernel, n_main=0, tk=SUBLANES, tail=(0, n_ctx, 0)),
        grid=(pairs,),
        in_specs=[cspec(), cspec(), pl.BlockSpec((1, 2 * V_ROWS, ATT_TK), lambda p: (n_main, p, 0))],
        out_specs=pl.BlockSpec((n_ctx, 2 * V_HEAD), lambda p: (0, p)),
        out_shape=jax.ShapeDtypeStruct((n_ctx, MLA_WIDTH), BF16),
        scratch_shapes=scratch(n_ctx, SUBLANES, n_ctx),
        compiler_params=_params("arbitrary"),
        name="attention_context",
    )(q, k, vt)
    return jnp.concatenate([lat, ctx], axis=0)


def _halo_specs(cols, tile_rows, t_rows):
    per = tile_rows // HALO
    last = t_rows // HALO - 1
    return [pl.BlockSpec((HALO, cols), lambda i: (jnp.maximum(i * per - 1, 0), 0)),
            pl.BlockSpec((tile_rows, cols), lambda i: (i, 0)),
            pl.BlockSpec((HALO, cols), lambda i: (jnp.minimum((i + 1) * per, last), 0))]


def _with_halo(prev_ref, cur_ref, next_ref, i, n_lat_tiles, t_rows):
    has_prev = jnp.logical_and(i != 0, i != n_lat_tiles)
    has_next = jnp.logical_and(i != n_lat_tiles - 1, i != pl.num_programs(0) - 1)
    prev = jnp.where(has_prev, prev_ref[...], 0.0)
    nxt = jnp.where(has_next, next_ref[...], 0.0)
    cur = cur_ref[...]
    row = lax.broadcasted_iota(jnp.int32, cur.shape, 0)
    cur = jnp.where(row < t_rows - i * cur.shape[0], cur, 0.0)
    return jnp.concatenate([prev, cur, nxt], axis=0)


def _conv_kernel(prev_ref, cur_ref, next_ref, w_ref, b_ref, o_ref, *, n_lat_tiles, t_rows):
    i = pl.program_id(0)
    ext = _with_halo(prev_ref, cur_ref, next_ref, i, n_lat_tiles, t_rows)
    rows = ext.shape[0]
    w = w_ref[...]
    y = (pltpu.roll(ext, 2, axis=0) * w[0:1] + pltpu.roll(ext, 1, axis=0) * w[1:2]
         + ext * w[2:3] + pltpu.roll(ext, rows - 1, axis=0) * w[3:4])
    y = y[HALO:rows - HALO] + b_ref[...]
    o_ref[...] = y / (1.0 + jnp.exp(-y))


def _conv(xbc, conv_w, conv_b, n_lat_tiles):
    t_rows, cols = xbc.shape
    nt = pl.cdiv(t_rows, ROW_TILE)
    return pl.pallas_call(
        functools.partial(_conv_kernel, n_lat_tiles=n_lat_tiles, t_rows=t_rows),
        grid=(nt,),
        in_specs=_halo_specs(cols, ROW_TILE, t_rows) + [
            pl.BlockSpec(conv_w.shape, lambda i: (0, 0)), pl.BlockSpec(conv_b.shape, lambda i: (0, 0))],
        out_specs=pl.BlockSpec((ROW_TILE, cols), lambda i: (i, 0)),
        out_shape=jax.ShapeDtypeStruct((t_rows, cols), F32),
        compiler_params=_params("arbitrary"),
        name="ssd_conv",
    )(xbc, xbc, xbc, conv_w, conv_b)


def _softplus(v):
    return jnp.maximum(v, 0.0) + jnp.log(1.0 + jnp.exp(-jnp.abs(v)))


def _ssd_kernel(xf_ref, xb_ref, dtf_ref, dtb_ref, dttf_ref, dttb_ref, bias_ref, biast_ref, alog_ref, alogt_ref,
                dskip_ref, expand_ref, yf_ref, yb_ref, state_ref):
    L = xf_ref.shape[0]
    P, N, H = SSD_HEAD_DIM, SSD_STATE, SSD_HEADS

    @pl.when(pl.program_id(0) == 0)
    def _():
        state_ref[...] = jnp.zeros(state_ref.shape, F32)

    r = lax.broadcasted_iota(jnp.int32, (L, L), 0)
    c = lax.broadcasted_iota(jnp.int32, (L, L), 1)
    lower = r >= c
    upper = r <= c
    lower_f = lower.astype(BF16)
    upper_f = upper.astype(BF16)

    def split3(a):
        a1 = a.astype(BF16)
        r1 = a - a1.astype(F32)
        a2 = r1.astype(BF16)
        return a1, a2, (r1 - a2.astype(F32)).astype(BF16)
    a_row = -jnp.exp(alog_ref[...])
    a_col = -jnp.exp(alogt_ref[...])

    def one_direction(x_ref, dt_ref, dtt_ref, y_ref, base, forward):
        x = x_ref[...]
        dt = _softplus(dt_ref[...] + bias_ref[...])
        dtt = _softplus(dtt_ref[...] + biast_ref[...])
        tri_col = lower_f if forward else upper_f
        tri_row = upper_f if forward else lower_f
        cs = sum(jnp.dot(tri_col, part, preferred_element_type=F32) for part in split3(dt * a_row))
        cst = sum(jnp.dot(part, tri_row, preferred_element_type=F32) for part in split3(dtt * a_col))
        mask = lower if forward else upper
        end = L - 1 if forward else 0
        outs = []
        cb = []
        for g in range(SSD_GROUPS):
            bg = x[:, SSD_WIDTH + g * N:SSD_WIDTH + (g + 1) * N].astype(BF16)
            cg = x[:, SSD_WIDTH + (SSD_GROUPS + g) * N:SSD_WIDTH + (SSD_GROUPS + g + 1) * N].astype(BF16)
            cb.append((bg, cg, lax.dot_general(cg, bg, (((1,), (1,)), ((), ())),
                                               preferred_element_type=F32).astype(BF16)))
        spread = expand_ref[0 if forward else 1]

        def replicate(a):
            hi = a.astype(BF16)
            lo = (a - hi.astype(F32)).astype(BF16)
            return (jnp.dot(hi, spread, preferred_element_type=F32)
                    + jnp.dot(lo, spread, preferred_element_type=F32))

        cs_rep, dt_rep = replicate(cs), replicate(dt)
        for hd in range(H):
            j = base + hd
            bg, cg, cbg = cb[hd // (H // SSD_GROUPS)]
            col = cs_rep[:, hd * LANES:(hd + 1) * LANES]
            rowv = cst[j:j + 1, :]
            total = col[end:end + 1, :]
            gap = jnp.tile(col, (1, L // LANES)) - rowv
            decay = jnp.where(mask, jnp.exp(jnp.minimum(gap, 0.0).astype(BF16)), 0.0)
            xh = (x[:, hd * P:(hd + 1) * P] * dt_rep[:, hd * LANES:hd * LANES + P]).astype(BF16)
            y = jnp.dot(cbg * decay, xh, preferred_element_type=F32)
            st = state_ref[j]
            y = y + jnp.dot(cg, st.astype(BF16), preferred_element_type=F32) * jnp.exp(col[:, :P])
            bw = (bg.astype(F32) * jnp.exp(total[:, :N] - col[:, :N])).astype(BF16)
            state_ref[j] = st * jnp.exp(total[:, :P]) + lax.dot_general(
                bw, xh, (((0,), (0,)), ((), ())), preferred_element_type=F32)
            outs.append(y)
        y_all = jnp.concatenate(outs, axis=1)
        if forward:
            y_all = y_all + x[:, :SSD_WIDTH] * dskip_ref[...]
        y_ref[...] = y_all

    one_direction(xf_ref, dtf_ref, dttf_ref, yf_ref, 0, True)
    one_direction(xb_ref, dtb_ref, dttb_ref, yb_ref, H, False)


def _ssd(xc, dt, dtt, bias_row, bias_col, alog_row, alog_col, dskip_row, n_lat_chunks):
    t_rows, cols = xc.shape
    L = SSD_CHUNK
    nc = t_rows // L
    fwd = lambda j: jnp.where(j == 0, n_lat_chunks, j - 1)
    bwd = lambda j: jnp.where(j == 0, n_lat_chunks, n_lat_chunks - j)
    small = lambda a: pl.BlockSpec(a.shape, lambda j: (0,) * a.ndim)
    r = jnp.arange(LANES)[None, :, None] - jnp.arange(2)[:, None, None] * SSD_HEADS
    spread = (r == jnp.arange(SSD_HEADS * LANES)[None, None, :] // LANES).astype(BF16)
    return pl.pallas_call(
        _ssd_kernel,
        grid=(nc,),
        in_specs=[pl.BlockSpec((L, cols), lambda j: (fwd(j), 0)),
                  pl.BlockSpec((L, cols), lambda j: (bwd(j), 0)),
                  pl.BlockSpec((L, LANES), lambda j: (fwd(j), 0)),
                  pl.BlockSpec((L, LANES), lambda j: (bwd(j), 0)),
                  pl.BlockSpec((2 * SUBLANES, L), lambda j: (0, fwd(j))),
                  pl.BlockSpec((2 * SUBLANES, L), lambda j: (0, bwd(j))),
                  small(bias_row), small(bias_col), small(alog_row), small(alog_col), small(dskip_row),
                  small(spread)],
        out_specs=[pl.BlockSpec((L, SSD_WIDTH), lambda j: (fwd(j), 0)),
                   pl.BlockSpec((L, SSD_WIDTH), lambda j: (bwd(j), 0))],
        out_shape=[jax.ShapeDtypeStruct((t_rows, SSD_WIDTH), F32)] * 2,
        scratch_shapes=[pltpu.VMEM((2 * SSD_HEADS, SSD_STATE, SSD_HEAD_DIM), F32)],
        compiler_params=_params("arbitrary"),
        name="ssd_scan",
    )(xc, xc, dt, dt, dtt, dtt, bias_row, bias_col, alog_row, alog_col, dskip_row, spread)


def _mixout_kernel(x_ref, mod_ref, att_ref, yf_ref, yb_ref, z_ref, pprev_ref, pcur_ref, pnext_ref,
                   ssdg_ref, poolw_ref, pools_ref, wout_ref, g2_ref, rw_ref, rb_ref,
                   x1_ref, h2_ref, route_ref, cnt_ref, base_ref, *, n_lat_tiles, n_lat, n_ctx, t_rows):
    i = pl.program_id(0)
    mod = mod_ref[0]
    z = z_ref[...]
    ssd = _rms((yf_ref[...] + yb_ref[...]) * (z / (1.0 + jnp.exp(-z)))) * ssdg_ref[...]
    ext = _with_halo(pprev_ref, pcur_ref, pnext_ref, i, n_lat_tiles, t_rows)
    rows = ext.shape[0]
    tm = rows - 2 * HALO
    w2 = ext + pltpu.roll(ext, 1, axis=0)
    w4 = pltpu.roll(w2, 1, axis=0) + pltpu.roll(w2, rows - 1, axis=0)
    w8 = pltpu.roll(w4, 2, axis=0) + pltpu.roll(w4, rows - 2, axis=0)
    w16 = pltpu.roll(w8, 4, axis=0) + pltpu.roll(w8, rows - 4, axis=0)
    lane = lax.broadcasted_iota(jnp.int32, (tm, POOL_WIDTH), 1)
    grp = lane // POOL_GROUP
    sl = slice(HALO, rows - HALO)
    wsum = jnp.where(grp == 0, w2[sl], jnp.where(grp == 1, w4[sl], jnp.where(grp == 2, w8[sl], w16[sl])))
    is_ctx = i >= n_lat_tiles
    seg_len = jnp.where(is_ctx, n_ctx, n_lat)
    t = lax.broadcasted_iota(jnp.int32, (tm, POOL_WIDTH), 0) + jnp.where(is_ctx, i - n_lat_tiles, i) * tm
    half = jnp.left_shift(1, grp)
    lo = jnp.clip(t - half, 0, seg_len)
    hi = jnp.clip(t + half, 0, seg_len)
    p = wsum / (hi - lo).astype(F32) - pcur_ref[...]
    pool = jnp.dot(p.astype(BF16), poolw_ref[...], preferred_element_type=F32) * pools_ref[...]
    mix = (jnp.dot(att_ref[...], wout_ref[0:MLA_WIDTH], preferred_element_type=F32)
           + jnp.dot(ssd.astype(BF16), wout_ref[MLA_WIDTH:MLA_WIDTH + SSD_WIDTH], preferred_element_type=F32)
           + jnp.dot(pool.astype(BF16), wout_ref[MLA_WIDTH + SSD_WIDTH:], preferred_element_type=F32))
    x1 = x_ref[...] + mod[2:3] * mix
    x1_ref[...] = x1
    h2 = _rms(x1) * g2_ref[...] * (1.0 + mod[4:5]) + mod[3:4]
    h2b = h2.astype(BF16)
    for kk in range(TOP_K):
        h2_ref[kk] = h2b
    h2_lo = (h2 - h2b.astype(F32)).astype(BF16)
    logits = (jnp.dot(h2b, rw_ref[0], preferred_element_type=F32)
              + jnp.dot(h2_lo, rw_ref[0], preferred_element_type=F32)
              + jnp.dot(h2b, rw_ref[1], preferred_element_type=F32)) + rb_ref[...]

    @pl.when(i == 0)
    def _():
        base_ref[...] = jnp.zeros(base_ref.shape, F32)

    elane = lax.broadcasted_iota(jnp.int32, logits.shape, 1)
    row_ok = lax.broadcasted_iota(jnp.int32, logits.shape, 0) < t_rows - i * tm
    lg = jnp.where(elane < N_EXPERTS, logits, -jnp.inf)
    rr = lax.broadcasted_iota(jnp.int32, (tm, tm), 0)
    cc = lax.broadcasted_iota(jnp.int32, (tm, tm), 1)
    earlier = (rr > cc).astype(BF16)
    offset = base_ref[...]
    tops, ids, ranks = [], [], []
    for kk in range(TOP_K):
        top = jnp.max(lg, axis=-1, keepdims=True)
        idx = jnp.min(jnp.where(lg == top, elane, LANES), axis=-1, keepdims=True)
        sel = jnp.logical_and(elane == idx, row_ok)
        lg = jnp.where(sel, -jnp.inf, lg)
        onehot = sel.astype(BF16)
        before = jnp.dot(earlier, onehot, preferred_element_type=F32) + offset
        ranks.append(jnp.sum(jnp.where(sel, before, 0.0), axis=-1, keepdims=True))
        offset = offset + jnp.sum(sel.astype(F32), axis=0, keepdims=True)
        tops.append(top)
        ids.append(idx.astype(F32))
    base_ref[...] = offset
    cnt_ref[...] = jnp.broadcast_to(offset, cnt_ref.shape)
    exps = [jnp.exp(tp - tops[0]) for tp in tops]
    denom = exps[0] + exps[1] + exps[2] + exps[3]
    route = jnp.zeros(logits.shape, F32)
    for kk in range(TOP_K):
        route = jnp.where(elane == kk, ids[kk], route)
        route = jnp.where(elane == TOP_K + kk, exps[kk] / denom, route)
        route = jnp.where(elane == 2 * TOP_K + kk, ranks[kk], route)
    route_ref[...] = route


def _mixer_output(x, modtab, att, yf, yb, z, pool_in, ssdg, poolw_bd, pools, wout, g2, rw_p, rb_p,
                  n_lat_tiles, n_lat, n_ctx):
    t_rows, d = x.shape
    nt = pl.cdiv(t_rows, ROW_TILE)
    row = lambda cols: pl.BlockSpec((ROW_TILE, cols), lambda i: (i, 0))
    full = lambda a: pl.BlockSpec(a.shape, lambda i: (0,) * a.ndim)
    return pl.pallas_call(
        functools.partial(_mixout_kernel, n_lat_tiles=n_lat_tiles, n_lat=n_lat, n_ctx=n_ctx, t_rows=t_rows),
        grid=(nt,),
        in_specs=[row(d), pl.BlockSpec((1, SUBLANES, d), lambda i: (i // n_lat_tiles, 0, 0)),
                  row(MLA_WIDTH), row(SSD_WIDTH), row(SSD_WIDTH), row(SSD_WIDTH)]
                 + _halo_specs(POOL_WIDTH, ROW_TILE, t_rows)
                 + [full(ssdg), full(poolw_bd), full(pools), full(wout), full(g2), full(rw_p), full(rb_p)],
        out_specs=[row(d), pl.BlockSpec((TOP_K, ROW_TILE, d), lambda i: (0, i, 0)), row(LANES),
                   pl.BlockSpec((SUBLANES, LANES), lambda i: (0, 0))],
        out_shape=[jax.ShapeDtypeStruct((t_rows, d), F32), jax.ShapeDtypeStruct((TOP_K, t_rows, d), BF16),
                   jax.ShapeDtypeStruct((t_rows, LANES), F32), jax.ShapeDtypeStruct((SUBLANES, LANES), F32)],
        scratch_shapes=[pltpu.VMEM((1, LANES), F32)],
        compiler_params=_params("arbitrary"),
        name="mixer_output",
    )(x, modtab, att, yf, yb, z, pool_in, pool_in, pool_in, ssdg, poolw_bd, pools, wout, g2, rw_p, rb_p)


def _expert_kernel(be_ref, nu_ref, x_ref, wg_ref, bg_ref, wu_ref, bu_ref, wd_ref, bd_ref, *rest, first_block):
    o_ref, wg_s, wu_s, wd_s = rest[-4:]
    i = pl.program_id(0)
    blk = i + first_block

    @pl.when(jnp.logical_or(i == 0, be_ref[blk] != be_ref[jnp.maximum(blk - 1, 0)]))
    def _():
        wg_s[...] = wg_ref[0, 0].astype(BF16)
        wu_s[...] = wu_ref[0, 0].astype(BF16)
        wd_s[...] = wd_ref[0, 0].astype(BF16)

    @pl.when(blk < nu_ref[0])
    def _():
        x = x_ref[...]
        g = jnp.minimum(jnp.dot(x, wg_s[...], preferred_element_type=F32) + bg_ref[0, 0], SWIGLU_LIMIT)
        u = jnp.clip(jnp.dot(x, wu_s[...], preferred_element_type=F32) + bu_ref[0, 0],
                     -SWIGLU_LIMIT, SWIGLU_LIMIT)
        a = g / (1.0 + jnp.exp(-SWIGLU_ALPHA * g)) * (u + 1.0)
        y = jnp.dot(a.astype(BF16), wd_s[...], preferred_element_type=F32) + bd_ref[0, 0]
        o_ref[...] = y.astype(o_ref.dtype)

    @pl.when(blk >= nu_ref[0])
    def _():
        o_ref[...] = jnp.zeros(o_ref.shape, o_ref.dtype)


def _experts(layer, x_part, first_block, n_rows, out_prev, block_expert, n_used, wg, bg, wu, bu, wd, bd):
    d = x_part.shape[1]
    depth, ne, _, dff = wg.shape
    wspec = lambda a, b: pl.BlockSpec((1, 1, a, b), lambda i, be, nu: (layer, be[i + first_block], 0, 0))
    in_specs = [pl.BlockSpec((MOE_TILE, d), lambda i, be, nu: (i, 0)),
                wspec(d, dff), wspec(1, dff), wspec(d, dff), wspec(1, dff), wspec(dff, d), wspec(1, d)]
    operands = [block_expert, n_used, x_part, wg, bg.reshape(depth, ne, 1, dff), wu, bu.reshape(depth, ne, 1, dff),
                wd, bd.reshape(depth, ne, 1, d)]
    aliases = {}
    if out_prev is not None:
        in_specs.append(pl.BlockSpec(memory_space=pl.ANY))
        aliases = {len(operands): 0}
        operands.append(out_prev)
    grid_spec = pltpu.PrefetchScalarGridSpec(
        num_scalar_prefetch=2,
        grid=(x_part.shape[0] // MOE_TILE,),
        in_specs=in_specs,
        out_specs=pl.BlockSpec((MOE_TILE, d), lambda i, be, nu: (i + first_block, 0)),
        scratch_shapes=[pltpu.VMEM((d, dff), BF16), pltpu.VMEM((d, dff), BF16), pltpu.VMEM((dff, d), BF16)],
    )
    return pl.pallas_call(
        functools.partial(_expert_kernel, first_block=first_block),
        grid_spec=grid_spec,
        out_shape=jax.ShapeDtypeStruct((n_rows, d), BF16),
        input_output_aliases=aliases,
        compiler_params=_params("arbitrary"),
        name="moe_experts",
    )(*operands)


def _dispatch_plan(route, counts_f):
    t_rows = route.shape[0]
    n_assign = t_rows * TOP_K
    ids = route[:, 0:TOP_K].astype(jnp.int32)
    rank =route[:, 2 * TOP_K:3 * TOP_K].astype(jnp.int32)
    counts = counts_f[0, :N_EXPERTS].astype(jnp.int32)
    padded = (counts + MOE_TILE - 1) // MOE_TILE * MOE_TILE
    pad_end = jnp.cumsum(padded)
    pad_start = pad_end - padded
    experts = jnp.arange(N_EXPERTS, dtype=jnp.int32)
    first_row = jnp.sum(jnp.where(ids[:, :, None] == experts[None, None, :], pad_start[None, None, :], 0), axis=-1)
    dest = (first_row + rank).T.reshape(-1)
    n_rows = n_assign + N_EXPERTS * MOE_TILE
    n_blocks = n_rows // MOE_TILE
    block_expert = jnp.minimum(
        jnp.sum((jnp.arange(n_blocks, dtype=jnp.int32)[:, None] * MOE_TILE >= pad_end[None, :]).astype(jnp.int32),
                axis=1), N_EXPERTS - 1).astype(jnp.int32)
    n_pad = n_rows - n_assign
    pads = jnp.concatenate([padded - counts, n_rows - pad_end[-1:]])
    pads_end = jnp.cumsum(pads)
    pad_first = jnp.concatenate([pad_start + counts, pad_end[-1:]])
    j = jnp.arange(n_pad, dtype=jnp.int32)
    grp = jnp.sum((j[:, None] >= pads_end[None, :]).astype(jnp.int32), axis=1)
    hit = grp[:, None] == jnp.arange(N_EXPERTS + 1, dtype=jnp.int32)[None, :]
    pad_rows = j + jnp.sum(jnp.where(hit, (pad_first - (pads_end - pads))[None, :], 0), axis=1)
    keys = jnp.concatenate([dest, pad_rows]).astype(jnp.int32)
    vals = jnp.concatenate([jnp.arange(n_assign, dtype=jnp.int32), j])
    row_src = lax.sort((keys, vals), num_keys=1)[1]
    n_used = (pad_end[-1:] // MOE_TILE).astype(jnp.int32)
    return dest, row_src, block_expert, n_used


def _rope_tables(n_lat, n_ctx):
    rows = n_lat // GRID_W
    row = jnp.repeat(jnp.arange(rows), GRID_W).astype(F32)
    col = jnp.tile(jnp.arange(GRID_W), rows).astype(F32)
    n_freq = QK_ROPE // 4
    inv_freq = ROPE_THETA ** (-jnp.arange(n_freq, dtype=F32) / n_freq)
    ang = jnp.stack([row[:, None] * inv_freq, col[:, None] * inv_freq], axis=1)
    cos, sin = jnp.cos(ang), jnp.sin(ang)
    zero = jnp.zeros_like(sin)
    ones = lambda w: jnp.ones((n_lat, w), F32)
    zeros = lambda w: jnp.zeros((n_lat, w), F32)
    per_axis = lambda a, b: jnp.stack([a, b], axis=2).reshape(n_lat, QK_ROPE)
    tail = HEAD_PAD - QK_DIM
    c = jnp.concatenate([ones(QK_NOPE), per_axis(cos, cos), ones(tail)], axis=1)
    sa = jnp.concatenate([zeros(QK_NOPE), per_axis(zero, sin), zeros(tail)], axis=1)
    sb = jnp.concatenate([zeros(QK_NOPE), per_axis(-sin, zero), zeros(tail)], axis=1)
    pad = lambda a, v: jnp.concatenate([a, jnp.full((n_ctx, HEAD_PAD), v, F32)], axis=0)
    return pad(c, 1.0), pad(sa, 0.0), pad(sb, 0.0)


def _pad_cols(a, width):
    return jnp.pad(a, ((0, 0), (0, width - a.shape[1])))


def _split_bf16(a):
    hi = a.astype(BF16)
    return jnp.stack([hi, (a - hi.astype(F32)).astype(BF16)])


def _pack_layer(i, w_in, q_g, kv_g, w_uq, w_ukv, qn_g, kn_g, conv_w, conv_b, a_log, dt_bias, d_skip, ssd_g,
                pool_w, pool_scale, w_out, router_w, router_b):
    d = w_in.shape[1]
    pts = [0, Q_LORA, KV_LORA, QK_ROPE, SSD_WIDTH, SSD_CONV_DIM, 2 * SSD_HEADS, POOL_WIDTH]
    offs = [sum(pts[:j + 1]) for j in range(len(pts))]
    seg = lambda j: w_in[i][:, offs[j]:offs[j + 1]]
    zc = lambda w: jnp.zeros((d, w), F32)
    w_in_p = jnp.concatenate([
        seg(0), seg(1), zc(QK_NOPE), seg(2), zc(LANES - QK_DIM), seg(3), seg(4),
        seg(5), zc(LANES - 2 * SSD_HEADS), seg(6)], axis=1).astype(BF16)
    wuq = w_uq[i].reshape(Q_LORA, MLA_HEADS, QK_DIM)
    wuq_p = jnp.pad(wuq, ((0, 0), (0, 0), (0, HEAD_PAD - QK_DIM))).reshape(Q_LORA, MLA_HEADS * HEAD_PAD)
    wukv = w_ukv[i].reshape(KV_LORA, MLA_HEADS, QK_NOPE + V_HEAD)
    wuk_p = jnp.pad(wukv[:, :, :QK_NOPE], ((0, 0), (0, 0), (0, HEAD_PAD - QK_NOPE))).reshape(
        KV_LORA, MLA_HEADS * HEAD_PAD)
    wuv = jnp.pad(wukv[:, :, QK_NOPE:], ((0, 0), (0, 0), (0, V_ROWS - V_HEAD))).reshape(
        KV_LORA, MLA_HEADS * V_ROWS).T
    flat12 = lambda a: a[i].reshape(1, 2 * SSD_HEADS)
    poolw_bd = jax.scipy.linalg.block_diag(*[pool_w[i][g] for g in range(len(POOL_WINDOWS))])
    return dict(
        w_in_p=w_in_p, qg=q_g[i][None], kvg=kv_g[i][None],
        wuq_p=wuq_p.astype(BF16), wuk_p=wuk_p.astype(BF16), wuv=wuv.astype(BF16),
        qng_p=_pad_cols(qn_g[i][None], HEAD_PAD), kng_p=_pad_cols(kn_g[i][None], HEAD_PAD),
        conv_w=conv_w[i], conv_b=conv_b[i][None],
        bias_row=_pad_cols(flat12(dt_bias), LANES), bias_col=_pad_cols(flat12(dt_bias), 2 * SUBLANES).T,
        alog_row=_pad_cols(flat12(a_log), LANES), alog_col=_pad_cols(flat12(a_log), 2 * SUBLANES).T,
        dskip_row=jnp.repeat(d_skip[i], SSD_HEAD_DIM)[None],
        ssdg=ssd_g[i][None], poolw_bd=poolw_bd.astype(BF16), pools=pool_scale[i][None],
        wout=w_out[i].astype(BF16), rw_p=_split_bf16(_pad_cols(router_w[i], LANES)),
        rb_p=_pad_cols(router_b[i][None], LANES))


def kernel(x, c, ctx, c_ctx, norm1_g, norm2_g, mod_w, mod_b, w_in, mla_q_norm_g, mla_kv_norm_g, mla_w_uq, mla_w_ukv, mla_qn_g, mla_kn_g, ssd_conv_w, ssd_conv_b, ssd_a_log, ssd_dt_bias, ssd_d, ssd_norm_g, pool_w, pool_scale, w_out, router_w, router_b, moe_w_gate, moe_b_gate, moe_w_up, moe_b_up, moe_w_down, moe_b_down):
    batch, n_lat, d = x.shape
    n_ctx = ctx.shape[1]
    depth = mod_w.shape[0]
    assert batch == 1 and d == D_MODEL and n_ctx == SSD_CHUNK and n_ctx <= ROW_TILE and n_lat % ROW_TILE == 0
    assert n_lat % ATT_TQ == 0 and n_lat % ATT_TK == 0 and n_lat % GRID_W == 0
    n_lat_tiles = n_lat // ROW_TILE

    cvec = jnp.zeros((SUBLANES, d), F32).at[0].set(c[0]).at[1].set(c_ctx)
    mods = _modulation(cvec, mod_w, mod_b)
    rope_c, rope_sa, rope_sb = _rope_tables(n_lat, n_ctx)

    xs = jnp.concatenate([x[0], ctx[0]], axis=0)
    moe = None
    for i in range(depth):
        lp = _pack_layer(i, w_in, mla_q_norm_g, mla_kv_norm_g, mla_w_uq, mla_w_ukv, mla_qn_g, mla_kn_g,
                         ssd_conv_w, ssd_conv_b, ssd_a_log, ssd_dt_bias, ssd_d, ssd_norm_g, pool_w, pool_scale,
                         w_out, router_w, router_b)
        modtab = jnp.pad(mods[i, :2].reshape(2, 6, d), ((0, 0), (0, SUBLANES - 6), (0, 0)))
        outs = _in_projection(
            xs, moe, modtab, norm1_g[i][None], lp['w_in_p'], lp['qg'], lp['kvg'], lp['wuq_p'], lp['wuk_p'], lp['wuv'],
            lp['qng_p'], lp['kng_p'], rope_c, rope_sa, rope_sb, n_lat_tiles)
        if moe is not None:
            xs, outs = outs[0], outs[1:]
        q, k, v, z, xbc, dt, pool_in = outs
        att = _attention(q, k, v, n_lat, n_ctx)
        xc = _conv(xbc, lp['conv_w'], lp['conv_b'], n_lat_tiles)
        dtt = dt[:, :2 * SUBLANES].T
        yf, yb = _ssd(xc, dt, dtt, lp['bias_row'], lp['bias_col'], lp['alog_row'], lp['alog_col'],
                      lp['dskip_row'], n_lat // SSD_CHUNK)
        x1, h2_slots, route, counts = _mixer_output(
            xs, modtab, att, yf, yb, z, pool_in, lp['ssdg'], lp['poolw_bd'], lp['pools'], lp['wout'],
            norm2_g[i][None], lp['rw_p'], lp['rb_p'], n_lat_tiles, n_lat, n_ctx)
        dest, row_src, block_expert, n_used = _dispatch_plan(route, counts)
        n_rows = row_src.shape[0]
        split = n_rows // MOE_TILE // 4 * MOE_TILE
        h2_flat = h2_slots.reshape(-1, d)
        out = None
        for lo, hi in ((0, split), (split, n_rows)):
            x_part = h2_flat.at[row_src[lo:hi]].get(mode='promise_in_bounds')
            out = _experts(i, x_part, lo // MOE_TILE, n_rows, out, block_expert, n_used,
                           moe_w_gate, moe_b_gate, moe_w_up, moe_b_up, moe_w_down, moe_b_down)
        picked = out.at[dest].get(mode='promise_in_bounds', unique_indices=True)
        xs, moe = x1, (picked.reshape(TOP_K, xs.shape[0], d), route, modtab)
    return _final_combine(xs, *moe, n_lat)[None]
```

```python
import functools

import jax
import jax.numpy as jnp
from jax import lax
from jax.experimental import pallas as pl
from jax.experimental.pallas import tpu as pltpu

F32 = jnp.float32
BF16 = jnp.bfloat16
HIGHEST = lax.Precision.HIGHEST

D_MODEL = 1024
GRID_W = 64
EPS = 1e-6
MLA_HEADS = 6
QK_NOPE = 64
QK_ROPE = 32
QK_DIM = QK_NOPE + QK_ROPE
V_HEAD = 64
Q_LORA = 256
KV_LORA = 128
MLA_WIDTH = MLA_HEADS * V_HEAD
ROPE_THETA = 10000.0
SSD_HEADS = 6
SSD_HEAD_DIM = 64
SSD_WIDTH = SSD_HEADS * SSD_HEAD_DIM
SSD_GROUPS = 2
SSD_STATE = 64
SSD_CONV = 4
SSD_CONV_DIM = SSD_WIDTH + 2 * SSD_GROUPS * SSD_STATE
POOL_WINDOWS = (2, 4, 8, 16)
POOL_GROUP = 64
POOL_WIDTH = len(POOL_WINDOWS) * POOL_GROUP
N_EXPERTS = 32
TOP_K = 4
D_FF = 1024
SWIGLU_LIMIT = 7.0
SWIGLU_ALPHA = 1.702

LANES = 128
SUBLANES = 8
HEAD_PAD = LANES
V_ROWS = HEAD_PAD
Q_SCALE = QK_DIM ** -0.5 * 1.4426950408889634

ROW_TILE = 512
SSD_CHUNK = 256
HALO = SUBLANES
ATT_TQ = 1024
ATT_TK = 512
MOE_TILE = 512
VMEM_LIMIT = 56 * 1024 * 1024

COL_QLAT = 0
COL_KVLAT = COL_QLAT + Q_LORA
COL_KROPE = COL_KVLAT + KV_LORA
COL_Z = COL_KROPE + LANES
COL_XBC = COL_Z + SSD_WIDTH
COL_DT = COL_XBC + SSD_CONV_DIM
COL_POOL = COL_DT + LANES
IN_COLS_PACKED = COL_POOL + POOL_WIDTH


def _row_sumsq(x):
    return jnp.dot((x * x).astype(BF16), jnp.ones((x.shape[1], LANES), BF16), preferred_element_type=F32)


def _rms(x):
    width = x.shape[1]
    inv = lax.rsqrt(_row_sumsq(x) * (1.0 / width) + EPS)
    return x * jnp.tile(inv, (1, width // LANES))


def _params(*sem):
    return pltpu.CompilerParams(dimension_semantics=sem, vmem_limit_bytes=VMEM_LIMIT)


def _mod_kernel(c_ref, w_ref, b_ref, o_ref):
    c = c_ref[...]
    s = c / (1.0 + jnp.exp(-c))
    o_ref[0] = jnp.dot(s, w_ref[0], precision=HIGHEST, preferred_element_type=F32) + b_ref[0]


def _modulation(cvec, mod_w, mod_b):
    depth, d, cols = mod_w.shape
    tn = 1536
    return pl.pallas_call(
        _mod_kernel,
        grid=(depth, cols // tn),
        in_specs=[pl.BlockSpec((SUBLANES, d), lambda l, j: (0, 0)),
                  pl.BlockSpec((1, d, tn), lambda l, j: (l, 0, j)),
                  pl.BlockSpec((1, 1, tn), lambda l, j: (l, 0, j))],
        out_specs=pl.BlockSpec((1, SUBLANES, tn), lambda l, j: (l, 0, j)),
        out_shape=jax.ShapeDtypeStruct((depth, SUBLANES, cols), F32),
        compiler_params=_params("arbitrary", "arbitrary"),
        name="modulation",
    )(cvec, mod_w, mod_b.reshape(depth, 1, cols))


def _moe_combine(x1_ref, picked_ref, route_ref, pmod_ref):
    route = route_ref[...]
    y = picked_ref[0].astype(F32) * route[:, TOP_K:TOP_K + 1]
    for kk in range(1, TOP_K):
        y = y + picked_ref[kk].astype(F32) * route[:, TOP_K + kk:TOP_K + kk + 1]
    return x1_ref[...] + pmod_ref[0][5:6] * y


def _combine_kernel(x1_ref, picked_ref, route_ref, pmod_ref, o_ref):
    o_ref[...] = _moe_combine(x1_ref, picked_ref, route_ref, pmod_ref)


def _final_combine(x1, picked, route, pmodtab, n_lat):
    d = x1.shape[1]
    row = lambda cols: pl.BlockSpec((ROW_TILE, cols), lambda i: (i, 0))
    return pl.pallas_call(
        _combine_kernel,
        grid=(n_lat // ROW_TILE,),
        in_specs=[row(d), pl.BlockSpec((TOP_K, ROW_TILE, d), lambda i: (0, i, 0)), row(LANES),
                  pl.BlockSpec((1, SUBLANES, d), lambda i: (0, 0, 0))],
        out_specs=row(d),
        out_shape=jax.ShapeDtypeStruct((n_lat, d), F32),
        compiler_params=_params("arbitrary"),
        name="moe_combine",
    )(x1, picked, route, pmodtab)


def _inproj_kernel(*refs, has_moe):
    if has_moe:
        x1_ref, picked_ref, route_ref, pmod_ref = refs[:4]
        refs = refs[4:]
    else:
        x_ref = refs[0]
        refs = refs[1:]
    (mod_ref, g1_ref, win_ref, qg_ref, kvg_ref, wuq_ref, wuk_ref, wuv_ref, qng_ref, kng_ref,
     rc_ref, rsa_ref, rsb_ref) = refs[:13]
    outs = refs[13:]
    if has_moe:
        xs_ref, outs = outs[0], outs[1:]
        x = _moe_combine(x1_ref, picked_ref, route_ref, pmod_ref)
        xs_ref[...] = x
    else:
        x = x_ref[...]
    q_ref, k_ref, v_ref, z_ref, xbc_ref, dt_ref, pool_ref = outs
    mod = mod_ref[0]
    shift, scale = mod[0:1], mod[1:2]
    h = _rms(x) * g1_ref[...] * (1.0 + scale) + shift
    proj = jnp.dot(h.astype(BF16), win_ref[...], preferred_element_type=F32)
    z_ref[...] = proj[:, COL_Z:COL_XBC]
    xbc_ref[...] = proj[:, COL_XBC:COL_DT]
    dt_ref[...] = proj[:, COL_DT:COL_POOL]
    pool_ref[...] = proj[:, COL_POOL:IN_COLS_PACKED]

    rc, rsa, rsb = rc_ref[...], rsa_ref[...], rsb_ref[...]

    def head_norm_rope(t, gain):
        ms = _row_sumsq(t) * (1.0 / QK_DIM)
        t = t * lax.rsqrt(ms + EPS) * gain
        return (t * rc + pltpu.roll(t, QK_ROPE // 4, axis=1) * rsa
                + pltpu.roll(t, HEAD_PAD - QK_ROPE // 4, axis=1) * rsb)

    qn = (_rms(proj[:, COL_QLAT:COL_KVLAT]) * qg_ref[...]).astype(BF16)
    q_all = jnp.dot(qn, wuq_ref[...], preferred_element_type=F32)
    kvn = (_rms(proj[:, COL_KVLAT:COL_KROPE]) * kvg_ref[...]).astype(BF16)
    k_all = jnp.dot(kvn, wuk_ref[...], preferred_element_type=F32)
    k_rope = proj[:, COL_KROPE:COL_Z]
    vt_all = lax.dot_general(wuv_ref[...], kvn, (((1,), (1,)), ((), ())), preferred_element_type=F32)
    vrow = lax.broadcasted_iota(jnp.int32, vt_all.shape, 0) % V_ROWS
    v_ref[0] = jnp.where(vrow == V_HEAD, 1.0, vt_all).astype(BF16)
    qng, kng = qng_ref[...], kng_ref[...]
    for hd in range(MLA_HEADS):
        sl = slice(hd * HEAD_PAD, (hd + 1) * HEAD_PAD)
        q_ref[:, sl] = (head_norm_rope(q_all[:, sl], qng) * Q_SCALE).astype(BF16)
        k_ref[:, sl] = head_norm_rope(k_all[:, sl] + k_rope, kng).astype(BF16)


def _in_projection(x, moe, modtab, g1, w_in_p, qg, kvg, wuq_p, wuk_p, wuv, qng_p, kng_p, rope_c, rope_sa, rope_sb,
                   n_lat_tiles):
    t_rows, d = x.shape
    nt = pl.cdiv(t_rows, ROW_TILE)
    row = lambda cols: pl.BlockSpec((ROW_TILE, cols), lambda i: (i, 0))
    full = lambda a: pl.BlockSpec(a.shape, lambda i: (0,) * a.ndim)
    seg = lambda: pl.BlockSpec((1, SUBLANES, d), lambda i: (i // n_lat_tiles, 0, 0))
    hw = MLA_HEADS * HEAD_PAD
    per = ATT_TK // ROW_TILE
    vw = MLA_HEADS * V_ROWS
    out_specs = [row(hw), row(hw), pl.BlockSpec((1, vw, ROW_TILE), lambda i: (i // per, 0, i % per)),
                 row(SSD_WIDTH), row(SSD_CONV_DIM), row(LANES), row(POOL_WIDTH)]
    sds = lambda cols, dt: jax.ShapeDtypeStruct((t_rows, cols), dt)
    out_shape = [sds(hw, BF16), sds(hw, BF16), jax.ShapeDtypeStruct((pl.cdiv(nt, per), vw, ATT_TK), BF16),
                 sds(SSD_WIDTH, F32), sds(SSD_CONV_DIM, F32), sds(LANES, F32), sds(POOL_WIDTH, F32)]
    lead_in, lead_specs = [x], [row(d)]
    if moe is not None:
        picked, route, pmodtab = moe
        lead_in += [picked, route, pmodtab]
        lead_specs += [pl.BlockSpec((TOP_K, ROW_TILE, d), lambda i: (0, i, 0)), row(LANES), seg()]
        out_specs = [row(d)] + out_specs
        out_shape = [sds(d, F32)] + out_shape
    return pl.pallas_call(
        functools.partial(_inproj_kernel, has_moe=moe is not None),
        grid=(nt,),
        in_specs=lead_specs + [seg(), full(g1), full(w_in_p), full(qg), full(kvg), full(wuq_p), full(wuk_p),
                               full(wuv), full(qng_p), full(kng_p), row(HEAD_PAD), row(HEAD_PAD), row(HEAD_PAD)],
        out_specs=out_specs,
        out_shape=out_shape,
        compiler_params=_params("arbitrary"),
        name="in_projection",
    )(*lead_in, modtab, g1, w_in_p, qg, kvg, wuq_p, wuk_p, wuv, qng_p, kng_p, rope_c, rope_sa, rope_sb)


def _attention_kernel(q_ref, k_ref, vt_ref, o_ref, sa_ref, sb_ref, st_ref, ca_ref, cb_ref, ct_ref, m_ref, acc_ref,
                      *, n_main, tk, tail):
    m_ref[...] = jnp.full(m_ref.shape, -jnp.inf, F32)
    acc_ref[...] = jnp.zeros(acc_ref.shape, F32)

    def scores(buf, start, size):
        s_ref, cmax_ref = buf
        for hh in range(2):
            q = q_ref[:, hh * HEAD_PAD:(hh + 1) * HEAD_PAD]
            k = k_ref[pl.ds(start, size), hh * HEAD_PAD:(hh + 1) * HEAD_PAD]
            s = lax.dot_general(k, q, (((1,), (1,)), ((), ())), preferred_element_type=F32)
            s_ref[hh] = s
            cmax_ref[hh] = jnp.max(s, axis=0, keepdims=True)

    def consume(buf, c, size):
        s_ref, cmax_ref = buf
        for hh in range(2):
            s = s_ref[hh]
            m_old = m_ref[hh]
            m_new = jnp.maximum(m_old, cmax_ref[hh])
            p = jnp.exp2((s - m_new).astype(BF16))
            vt = vt_ref[c, hh * V_ROWS:(hh + 1) * V_ROWS, 0:size]
            acc_ref[hh] = jnp.exp2(m_old - m_new) * acc_ref[hh] + jnp.dot(vt, p, preferred_element_type=F32)
            m_ref[hh] = m_new

    row0 = lambda c: pl.multiple_of(c * tk, tk)
    t_start, t_size, t_chunk = tail
    sa_ref, sb_ref, st_ref = (sa_ref, ca_ref), (sb_ref, cb_ref), (st_ref, ct_ref)
    scores(st_ref, t_start, t_size)
    if n_main > 0:
        scores(sa_ref, 0, tk)
    consume(st_ref, t_chunk, t_size)
    if n_main > 0:
        def body(j, carry):
            scores(sb_ref, row0(2 * j + 1), tk)
            consume(sa_ref, 2 * j, tk)
            scores(sa_ref, row0(2 * j + 2), tk)
            consume(sb_ref, 2 * j + 1, tk)
            return carry
        lax.fori_loop(0, n_main // 2 - 1, body, 0)
        scores(sb_ref, (n_main - 1) * tk, tk)
        consume(sa_ref, n_main - 2, tk)
        consume(sb_ref, n_main - 1, tk)
    outs = []
    for hh in range(2):
        acc = acc_ref[hh]
        outs.append(acc[0:V_HEAD] / acc[V_HEAD:V_HEAD + 1])
    o_ref[...] = jnp.concatenate(outs, axis=0).T.astype(o_ref.dtype)


def _attention(q, k, vt, n_lat, n_ctx):
    t_rows = q.shape[0]
    pairs = MLA_HEADS // 2
    n_main = n_lat // ATT_TK
    n_chunks = vt.shape[0]
    assert n_main % 2 == 0 and n_main >= 2 and n_chunks == n_main + 1

    def scratch(tq, tk, t_size):
        return ([pltpu.VMEM((2, tk, tq), F32), pltpu.VMEM((2, tk, tq), F32), pltpu.VMEM((2, t_size, tq), F32)]
                + [pltpu.VMEM((2, 1, tq), F32)] * 4 + [pltpu.VMEM((2, V_ROWS, tq), F32)])

    lat = pl.pallas_call(
        functools.partial(_attention_kernel, n_main=n_main, tk=ATT_TK, tail=(n_lat, n_ctx, n_main)),
        grid=(pairs, n_lat // ATT_TQ),
        in_specs=[pl.BlockSpec((ATT_TQ, 2 * HEAD_PAD), lambda p, i: (i, p)),
                  pl.BlockSpec((t_rows, 2 * HEAD_PAD), lambda p, i: (0, p), pipeline_mode=pl.Buffered(1)),
                  pl.BlockSpec((n_chunks, 2 * V_ROWS, ATT_TK), lambda p, i: (0, p, 0),
                               pipeline_mode=pl.Buffered(1))],
        out_specs=pl.BlockSpec((ATT_TQ, 2 * V_HEAD), lambda p, i: (i, p)),
        out_shape=jax.ShapeDtypeStruct((n_lat, MLA_WIDTH), BF16),
        scratch_shapes=scratch(ATT_TQ, ATT_TK, n_ctx),
        compiler_params=_params("arbitrary", "arbitrary"),
        name="attention_latent",
    )(q, k, vt)
    cblk = n_lat // n_ctx
    cspec = lambda: pl.BlockSpec((n_ctx, 2 * HEAD_PAD), lambda p: (cblk, p))
    ctx = pl.pallas_call(
        functools.partial(_attention_kernel, n_main=0, tk=SUBLANES, tail=(0, n_ctx, 0)),
        grid=(pairs,),
        in_specs=[cspec(), cspec(), pl.BlockSpec((1, 2 * V_ROWS, ATT_TK), lambda p: (n_main, p, 0))],
        out_specs=pl.BlockSpec((n_ctx, 2 * V_HEAD), lambda p: (0, p)),
        out_shape=jax.ShapeDtypeStruct((n_ctx, MLA_WIDTH), BF16),
        scratch_shapes=scratch(n_ctx, SUBLANES, n_ctx),
        compiler_params=_params("arbitrary"),
        name="attention_context",
    )(q, k, vt)
    return jnp.concatenate([lat, ctx], axis=0)


def _halo_specs(cols, tile_rows, t_rows):
    per = tile_rows // HALO
    last = t_rows // HALO - 1
    return [pl.BlockSpec((HALO, cols), lambda i: (jnp.maximum(i * per - 1, 0), 0)),
            pl.BlockSpec((tile_rows, cols), lambda i: (i, 0)),
            pl.BlockSpec((HALO, cols), lambda i: (jnp.minimum((i + 1) * per, last), 0))]


def _with_halo(prev_ref, cur_ref, next_ref, i, n_lat_tiles, t_rows):
    has_prev = jnp.logical_and(i != 0, i != n_lat_tiles)
    has_next = jnp.logical_and(i != n_lat_tiles - 1, i != pl.num_programs(0) - 1)
    prev = jnp.where(has_prev, prev_ref[...], 0.0)
    nxt = jnp.where(has_next, next_ref[...], 0.0)
    cur = cur_ref[...]
    row = lax.broadcasted_iota(jnp.int32, cur.shape, 0)
    cur = jnp.where(row < t_rows - i * cur.shape[0], cur, 0.0)
    return jnp.concatenate([prev, cur, nxt], axis=0)


def _conv_kernel(prev_ref, cur_ref, next_ref, w_ref, b_ref, o_ref, *, n_lat_tiles, t_rows):
    i = pl.program_id(0)
    ext = _with_halo(prev_ref, cur_ref, next_ref, i, n_lat_tiles, t_rows)
    rows = ext.shape[0]
    w = w_ref[...]
    y = (pltpu.roll(ext, 2, axis=0) * w[0:1] + pltpu.roll(ext, 1, axis=0) * w[1:2]
         + ext * w[2:3] + pltpu.roll(ext, rows - 1, axis=0) * w[3:4])
    y = y[HALO:rows - HALO] + b_ref[...]
    o_ref[...] = y / (1.0 + jnp.exp(-y))


def _conv(xbc, conv_w, conv_b, n_lat_tiles):
    t_rows, cols = xbc.shape
    nt = pl.cdiv(t_rows, ROW_TILE)
    return pl.pallas_call(
        functools.partial(_conv_kernel, n_lat_tiles=n_lat_tiles, t_rows=t_rows),
        grid=(nt,),
        in_specs=_halo_specs(cols, ROW_TILE, t_rows) + [
            pl.BlockSpec(conv_w.shape, lambda i: (0, 0)), pl.BlockSpec(conv_b.shape, lambda i: (0, 0))],
        out_specs=pl.BlockSpec((ROW_TILE, cols), lambda i: (i, 0)),
        out_shape=jax.ShapeDtypeStruct((t_rows, cols), F32),
        compiler_params=_params("arbitrary"),
        name="ssd_conv",
    )(xbc, xbc, xbc, conv_w, conv_b)


def _softplus(v):
    return jnp.maximum(v, 0.0) + jnp.log(1.0 + jnp.exp(-jnp.abs(v)))


def _ssd_kernel(xf_ref, xb_ref, dtf_ref, dtb_ref, dttf_ref, dttb_ref, bias_ref, biast_ref, alog_ref, alogt_ref,
                dskip_ref, expand_ref, yf_ref, yb_ref, state_ref):
    L = xf_ref.shape[0]
    P, N, H = SSD_HEAD_DIM, SSD_STATE, SSD_HEADS

    @pl.when(pl.program_id(0) == 0)
    def _():
        state_ref[...] = jnp.zeros(state_ref.shape, F32)

    r = lax.broadcasted_iota(jnp.int32, (L, L), 0)
    c = lax.broadcasted_iota(jnp.int32, (L, L), 1)
    lower = r >= c
    upper = r <= c
    lower_f = lower.astype(BF16)
    upper_f = upper.astype(BF16)

    def split3(a):
        a1 = a.astype(BF16)
        r1 = a - a1.astype(F32)
        a2 = r1.astype(BF16)
        return a1, a2, (r1 - a2.astype(F32)).astype(BF16)
    a_row = -jnp.exp(alog_ref[...])
    a_col = -jnp.exp(alogt_ref[...])

    def one_direction(x_ref, dt_ref, dtt_ref, y_ref, base, forward):
        x = x_ref[...]
        dt = _softplus(dt_ref[...] + bias_ref[...])
        dtt = _softplus(dtt_ref[...] + biast_ref[...])
        tri_col = lower_f if forward else upper_f
        tri_row = upper_f if forward else lower_f
        cs = sum(jnp.dot(tri_col, part, preferred_element_type=F32) for part in split3(dt * a_row))
        cst = sum(jnp.dot(part, tri_row, preferred_element_type=F32) for part in split3(dtt * a_col))
        mask = lower if forward else upper
        end = L - 1 if forward else 0
        outs = []
        cb = []
        for g in range(SSD_GROUPS):
            bg = x[:, SSD_WIDTH + g * N:SSD_WIDTH + (g + 1) * N].astype(BF16)
            cg = x[:, SSD_WIDTH + (SSD_GROUPS + g) * N:SSD_WIDTH + (SSD_GROUPS + g + 1) * N].astype(BF16)
            cb.append((bg, cg, lax.dot_general(cg, bg, (((1,), (1,)), ((), ())),
                                               preferred_element_type=F32).astype(BF16)))
        spread = expand_ref[0 if forward else 1]

        def replicate(a):
            hi = a.astype(BF16)
            lo = (a - hi.astype(F32)).astype(BF16)
            return (jnp.dot(hi, spread, preferred_element_type=F32)
                    + jnp.dot(lo, spread, preferred_element_type=F32))

        cs_rep, dt_rep = replicate(cs), replicate(dt)
        for hd in range(H):
            j = base + hd
            bg, cg, cbg = cb[hd // (H // SSD_GROUPS)]
            col = cs_rep[:, hd * LANES:(hd + 1) * LANES]
            rowv = cst[j:j + 1, :]
            total = col[end:end + 1, :]
            gap = jnp.tile(col, (1, L // LANES)) - rowv
            decay = jnp.where(mask, jnp.exp(jnp.minimum(gap, 0.0).astype(BF16)), 0.0)
            xh = (x[:, hd * P:(hd + 1) * P] * dt_rep[:, hd * LANES:hd * LANES + P]).astype(BF16)
            y = jnp.dot(cbg * decay, xh, preferred_element_type=F32)
            st = state_ref[j]
            y = y + jnp.dot(cg, st.astype(BF16), preferred_element_type=F32) * jnp.exp(col[:, :P])
            bw = (bg.astype(F32) * jnp.exp(total[:, :N] - col[:, :N])).astype(BF16)
            state_ref[j] = st * jnp.exp(total[:, :P]) + lax.dot_general(
                bw, xh, (((0,), (0,)), ((), ())), preferred_element_type=F32)
            outs.append(y)
        y_all = jnp.concatenate(outs, axis=1)
        if forward:
            y_all = y_all + x[:, :SSD_WIDTH] * dskip_ref[...]
        y_ref[...] = y_all

    one_direction(xf_ref, dtf_ref, dttf_ref, yf_ref, 0, True)
    one_direction(xb_ref, dtb_ref, dttb_ref, yb_ref, H, False)


def _ssd(xc, dt, dtt, bias_row, bias_col, alog_row, alog_col, dskip_row, n_lat_chunks):
    t_rows, cols = xc.shape
    L = SSD_CHUNK
    nc = t_rows // L
    fwd = lambda j: jnp.where(j == 0, n_lat_chunks, j - 1)
    bwd = lambda j: jnp.where(j == 0, n_lat_chunks, n_lat_chunks - j)
    small = lambda a: pl.BlockSpec(a.shape, lambda j: (0,) * a.ndim)
    r = jnp.arange(LANES)[None, :, None] - jnp.arange(2)[:, None, None] * SSD_HEADS
    spread = (r == jnp.arange(SSD_HEADS * LANES)[None, None, :] // LANES).astype(BF16)
    return pl.pallas_call(
        _ssd_kernel,
        grid=(nc,),
        in_specs=[pl.BlockSpec((L, cols), lambda j: (fwd(j), 0)),
                  pl.BlockSpec((L, cols), lambda j: (bwd(j), 0)),
                  pl.BlockSpec((L, LANES), lambda j: (fwd(j), 0)),
                  pl.BlockSpec((L, LANES), lambda j: (bwd(j), 0)),
                  pl.BlockSpec((2 * SUBLANES, L), lambda j: (0, fwd(j))),
                  pl.BlockSpec((2 * SUBLANES, L), lambda j: (0, bwd(j))),
                  small(bias_row), small(bias_col), small(alog_row), small(alog_col), small(dskip_row),
                  small(spread)],
        out_specs=[pl.BlockSpec((L, SSD_WIDTH), lambda j: (fwd(j), 0)),
                   pl.BlockSpec((L, SSD_WIDTH), lambda j: (bwd(j), 0))],
        out_shape=[jax.ShapeDtypeStruct((t_rows, SSD_WIDTH), F32)] * 2,
        scratch_shapes=[pltpu.VMEM((2 * SSD_HEADS, SSD_STATE, SSD_HEAD_DIM), F32)],
        compiler_params=_params("arbitrary"),
        name="ssd_scan",
    )(xc, xc, dt, dt, dtt, dtt, bias_row, bias_col, alog_row, alog_col, dskip_row, spread)


def _mixout_kernel(x_ref, mod_ref, att_ref, yf_ref, yb_ref, z_ref, pprev_ref, pcur_ref, pnext_ref,
                   ssdg_ref, poolw_ref, pools_ref, wout_ref, g2_ref, rw_ref, rb_ref,
                   x1_ref, h2_ref, route_ref, cnt_ref, base_ref, *, n_lat_tiles, n_lat, n_ctx, t_rows):
    i = pl.program_id(0)
    mod = mod_ref[0]
    z = z_ref[...]
    ssd = _rms((yf_ref[...] + yb_ref[...]) * (z / (1.0 + jnp.exp(-z)))) * ssdg_ref[...]
    ext = _with_halo(pprev_ref, pcur_ref, pnext_ref, i, n_lat_tiles, t_rows)
    rows = ext.shape[0]
    tm = rows - 2 * HALO
    w2 = ext + pltpu.roll(ext, 1, axis=0)
    w4 = pltpu.roll(w2, 1, axis=0) + pltpu.roll(w2, rows - 1, axis=0)
    w8 = pltpu.roll(w4, 2, axis=0) + pltpu.roll(w4, rows - 2, axis=0)
    w16 = pltpu.roll(w8, 4, axis=0) + pltpu.roll(w8, rows - 4, axis=0)
    lane = lax.broadcasted_iota(jnp.int32, (tm, POOL_WIDTH), 1)
    grp = lane // POOL_GROUP
    sl = slice(HALO, rows - HALO)
    wsum = jnp.where(grp == 0, w2[sl], jnp.where(grp == 1, w4[sl], jnp.where(grp == 2, w8[sl], w16[sl])))
    is_ctx = i >= n_lat_tiles
    seg_len = jnp.where(is_ctx, n_ctx, n_lat)
    t = lax.broadcasted_iota(jnp.int32, (tm, POOL_WIDTH), 0) + jnp.where(is_ctx, i - n_lat_tiles, i) * tm
    half = jnp.left_shift(1, grp)
    lo = jnp.clip(t - half, 0, seg_len)
    hi = jnp.clip(t + half, 0, seg_len)
    p = wsum / (hi - lo).astype(F32) - pcur_ref[...]
    pool = jnp.dot(p.astype(BF16), poolw_ref[...], preferred_element_type=F32) * pools_ref[...]
    mix = (jnp.dot(att_ref[...], wout_ref[0:MLA_WIDTH], preferred_element_type=F32)
           + jnp.dot(ssd.astype(BF16), wout_ref[MLA_WIDTH:MLA_WIDTH + SSD_WIDTH], preferred_element_type=F32)
           + jnp.dot(pool.astype(BF16), wout_ref[MLA_WIDTH + SSD_WIDTH:], preferred_element_type=F32))
    x1 = x_ref[...] + mod[2:3] * mix
    x1_ref[...] = x1
    h2 = _rms(x1) * g2_ref[...] * (1.0 + mod[4:5]) + mod[3:4]
    h2b = h2.astype(BF16)
    for kk in range(TOP_K):
        h2_ref[kk] = h2b
    h2_lo = (h2 - h2b.astype(F32)).astype(BF16)
    logits = (jnp.dot(h2b, rw_ref[0], preferred_element_type=F32)
              + jnp.dot(h2_lo, rw_ref[0], preferred_element_type=F32)
              + jnp.dot(h2b, rw_ref[1], preferred_element_type=F32)) + rb_ref[...]

    @pl.when(i == 0)
    def _():
        base_ref[...] = jnp.zeros(base_ref.shape, F32)

    elane = lax.broadcasted_iota(jnp.int32, logits.shape, 1)
    row_ok = lax.broadcasted_iota(jnp.int32, logits.shape, 0) < t_rows - i * tm
    lg = jnp.where(elane < N_EXPERTS, logits, -jnp.inf)
    rr = lax.broadcasted_iota(jnp.int32, (tm, tm), 0)
    cc = lax.broadcasted_iota(jnp.int32, (tm, tm), 1)
    earlier = (rr > cc).astype(BF16)
    offset = base_ref[...]
    tops, ids, ranks = [], [], []
    for kk in range(TOP_K):
        top = jnp.max(lg, axis=-1, keepdims=True)
        idx = jnp.min(jnp.where(lg == top, elane, LANES), axis=-1, keepdims=True)
        sel = jnp.logical_and(elane == idx, row_ok)
        lg = jnp.where(sel, -jnp.inf, lg)
        onehot = sel.astype(BF16)
        before = jnp.dot(earlier, onehot, preferred_element_type=F32) + offset
        ranks.append(jnp.sum(jnp.where(sel, before, 0.0), axis=-1, keepdims=True))
        offset = offset + jnp.sum(sel.astype(F32), axis=0, keepdims=True)
        tops.append(top)
        ids.append(idx.astype(F32))
    base_ref[...] = offset
    cnt_ref[...] = jnp.broadcast_to(offset, cnt_ref.shape)
    exps = [jnp.exp(tp - tops[0]) for tp in tops]
    denom = exps[0] + exps[1] + exps[2] + exps[3]
    route = jnp.zeros(logits.shape, F32)
    for kk in range(TOP_K):
        route = jnp.where(elane == kk, ids[kk], route)
        route = jnp.where(elane == TOP_K + kk, exps[kk] / denom, route)
        route = jnp.where(elane == 2 * TOP_K + kk, ranks[kk], route)
    route_ref[...] = route


def _mixer_output(x, modtab, att, yf, yb, z, pool_in, ssdg, poolw_bd, pools, wout, g2, rw_p, rb_p,
                  n_lat_tiles, n_lat, n_ctx):
    t_rows, d = x.shape
    nt = pl.cdiv(t_rows, ROW_TILE)
    row = lambda cols: pl.BlockSpec((ROW_TILE, cols), lambda i: (i, 0))
    full = lambda a: pl.BlockSpec(a.shape, lambda i: (0,) * a.ndim)
    return pl.pallas_call(
        functools.partial(_mixout_kernel, n_lat_tiles=n_lat_tiles, n_lat=n_lat, n_ctx=n_ctx, t_rows=t_rows),
        grid=(nt,),
        in_specs=[row(d), pl.BlockSpec((1, SUBLANES, d), lambda i: (i // n_lat_tiles, 0, 0)),
                  row(MLA_WIDTH), row(SSD_WIDTH), row(SSD_WIDTH), row(SSD_WIDTH)]
                 + _halo_specs(POOL_WIDTH, ROW_TILE, t_rows)
                 + [full(ssdg), full(poolw_bd), full(pools), full(wout), full(g2), full(rw_p), full(rb_p)],
        out_specs=[row(d), pl.BlockSpec((TOP_K, ROW_TILE, d), lambda i: (0, i, 0)), row(LANES),
                   pl.BlockSpec((SUBLANES, LANES), lambda i: (0, 0))],
        out_shape=[jax.ShapeDtypeStruct((t_rows, d), F32), jax.ShapeDtypeStruct((TOP_K, t_rows, d), BF16),
                   jax.ShapeDtypeStruct((t_rows, LANES), F32), jax.ShapeDtypeStruct((SUBLANES, LANES), F32)],
        scratch_shapes=[pltpu.VMEM((1, LANES), F32)],
        compiler_params=_params("arbitrary"),
        name="mixer_output",
    )(x, modtab, att, yf, yb, z, pool_in, pool_in, pool_in, ssdg, poolw_bd, pools, wout, g2, rw_p, rb_p)


def _expert_kernel(be_ref, nu_ref, x_ref, wg_ref, bg_ref, wu_ref, bu_ref, wd_ref, bd_ref, *rest, first_block):
    o_ref, wg_s, wu_s, wd_s = rest[-4:]
    i = pl.program_id(0)
    blk = i + first_block

    @pl.when(jnp.logical_or(i == 0, be_ref[blk] != be_ref[jnp.maximum(blk - 1, 0)]))
    def _():
        wg_s[...] = wg_ref[0, 0].astype(BF16)
        wu_s[...] = wu_ref[0, 0].astype(BF16)
        wd_s[...] = wd_ref[0, 0].astype(BF16)

    @pl.when(blk < nu_ref[0])
    def _():
        x = x_ref[...]
        g = jnp.minimum(jnp.dot(x, wg_s[...], preferred_element_type=F32) + bg_ref[0, 0], SWIGLU_LIMIT)
        u = jnp.clip(jnp.dot(x, wu_s[...], preferred_element_type=F32) + bu_ref[0, 0],
                     -SWIGLU_LIMIT, SWIGLU_LIMIT)
        a = g / (1.0 + jnp.exp(-SWIGLU_ALPHA * g)) * (u + 1.0)
        y = jnp.dot(a.astype(BF16), wd_s[...], preferred_element_type=F32) + bd_ref[0, 0]
        o_ref[...] = y.astype(o_ref.dtype)

    @pl.when(blk >= nu_ref[0])
    def _():
        o_ref[...] = jnp.zeros(o_ref.shape, o_ref.dtype)


def _experts(layer, x_part, first_block, n_rows, out_prev, block_expert, n_used, wg, bg, wu, bu, wd, bd):
    d = x_part.shape[1]
    depth, ne, _, dff = wg.shape
    wspec = lambda a, b: pl.BlockSpec((1, 1, a, b), lambda i, be, nu: (layer, be[i + first_block], 0, 0))
    in_specs = [pl.BlockSpec((MOE_TILE, d), lambda i, be, nu: (i, 0)),
                wspec(d, dff), wspec(1, dff), wspec(d, dff), wspec(1, dff), wspec(dff, d), wspec(1, d)]
    operands = [block_expert, n_used, x_part, wg, bg.reshape(depth, ne, 1, dff), wu, bu.reshape(depth, ne, 1, dff),
                wd, bd.reshape(depth, ne, 1, d)]
    aliases = {}
    if out_prev is not None:
        in_specs.append(pl.BlockSpec(memory_space=pl.ANY))
        aliases = {len(operands): 0}
        operands.append(out_prev)
    grid_spec = pltpu.PrefetchScalarGridSpec(
        num_scalar_prefetch=2,
        grid=(x_part.shape[0] // MOE_TILE,),
        in_specs=in_specs,
        out_specs=pl.BlockSpec((MOE_TILE, d), lambda i, be, nu: (i + first_block, 0)),
        scratch_shapes=[pltpu.VMEM((d, dff), BF16), pltpu.VMEM((d, dff), BF16), pltpu.VMEM((dff, d), BF16)],
    )
    return pl.pallas_call(
        functools.partial(_expert_kernel, first_block=first_block),
        grid_spec=grid_spec,
        out_shape=jax.ShapeDtypeStruct((n_rows, d), BF16),
        input_output_aliases=aliases,
        compiler_params=_params("arbitrary"),
        name="moe_experts",
    )(*operands)


def _dispatch_plan(route, counts_f):
    t_rows = route.shape[0]
    n_assign = t_rows * TOP_K
    ids = route[:, 0:TOP_K].astype(jnp.int32)
    rank =route[:, 2 * TOP_K:3 * TOP_K].astype(jnp.int32)
    counts = counts_f[0, :N_EXPERTS].astype(jnp.int32)
    padded = (counts + MOE_TILE - 1) // MOE_TILE * MOE_TILE
    pad_end = jnp.cumsum(padded)
    pad_start = pad_end - padded
    experts = jnp.arange(N_EXPERTS, dtype=jnp.int32)
    first_row = jnp.sum(jnp.where(ids[:, :, None] == experts[None, None, :], pad_start[None, None, :], 0), axis=-1)
    dest = (first_row + rank).T.reshape(-1)
    n_rows = n_assign + N_EXPERTS * MOE_TILE
    n_blocks = n_rows // MOE_TILE
    block_expert = jnp.minimum(
        jnp.sum((jnp.arange(n_blocks, dtype=jnp.int32)[:, None] * MOE_TILE >= pad_end[None, :]).astype(jnp.int32),
                axis=1), N_EXPERTS - 1).astype(jnp.int32)
    marked = jnp.zeros((n_rows,), jnp.int32).at[dest].add(jnp.arange(n_assign, dtype=jnp.int32) + 1)
    row_src = jnp.where(marked > 0, marked - 1, jnp.arange(n_rows, dtype=jnp.int32) % n_assign)
    n_used = (pad_end[-1:] // MOE_TILE).astype(jnp.int32)
    return dest, row_src, block_expert, n_used


def _rope_tables(n_lat, n_ctx):
    rows = n_lat // GRID_W
    row = jnp.repeat(jnp.arange(rows), GRID_W).astype(F32)
    col = jnp.tile(jnp.arange(GRID_W), rows).astype(F32)
    n_freq = QK_ROPE // 4
    inv_freq = ROPE_THETA ** (-jnp.arange(n_freq, dtype=F32) / n_freq)
    ang = jnp.stack([row[:, None] * inv_freq, col[:, None] * inv_freq], axis=1)
    cos, sin = jnp.cos(ang), jnp.sin(ang)
    zero = jnp.zeros_like(sin)
    ones = lambda w: jnp.ones((n_lat, w), F32)
    zeros = lambda w: jnp.zeros((n_lat, w), F32)
    per_axis = lambda a, b: jnp.stack([a, b], axis=2).reshape(n_lat, QK_ROPE)
    tail = HEAD_PAD - QK_DIM
    c = jnp.concatenate([ones(QK_NOPE), per_axis(cos, cos), ones(tail)], axis=1)
    sa = jnp.concatenate([zeros(QK_NOPE), per_axis(zero, sin), zeros(tail)], axis=1)
    sb = jnp.concatenate([zeros(QK_NOPE), per_axis(-sin, zero), zeros(tail)], axis=1)
    pad = lambda a, v: jnp.concatenate([a, jnp.full((n_ctx, HEAD_PAD), v, F32)], axis=0)
    return pad(c, 1.0), pad(sa, 0.0), pad(sb, 0.0)


def _pad_cols(a, width):
    return jnp.pad(a, ((0, 0), (0, width - a.shape[1])))


def _split_bf16(a):
    hi = a.astype(BF16)
    return jnp.stack([hi, (a - hi.astype(F32)).astype(BF16)])


def _pack_layer(i, w_in, q_g, kv_g, w_uq, w_ukv, qn_g, kn_g, conv_w, conv_b, a_log, dt_bias, d_skip, ssd_g,
                pool_w, pool_scale, w_out, router_w, router_b):
    d = w_in.shape[1]
    pts = [0, Q_LORA, KV_LORA, QK_ROPE, SSD_WIDTH, SSD_CONV_DIM, 2 * SSD_HEADS, POOL_WIDTH]
    offs = [sum(pts[:j + 1]) for j in range(len(pts))]
    seg = lambda j: w_in[i][:, offs[j]:offs[j + 1]]
    zc = lambda w: jnp.zeros((d, w), F32)
    w_in_p = jnp.concatenate([
        seg(0), seg(1), zc(QK_NOPE), seg(2), zc(LANES - QK_DIM), seg(3), seg(4),
        seg(5), zc(LANES - 2 * SSD_HEADS), seg(6)], axis=1).astype(BF16)
    wuq = w_uq[i].reshape(Q_LORA, MLA_HEADS, QK_DIM)
    wuq_p = jnp.pad(wuq, ((0, 0), (0, 0), (0, HEAD_PAD - QK_DIM))).reshape(Q_LORA, MLA_HEADS * HEAD_PAD)
    wukv = w_ukv[i].reshape(KV_LORA, MLA_HEADS, QK_NOPE + V_HEAD)
    wuk_p = jnp.pad(wukv[:, :, :QK_NOPE], ((0, 0), (0, 0), (0, HEAD_PAD - QK_NOPE))).reshape(
        KV_LORA, MLA_HEADS * HEAD_PAD)
    wuv = jnp.pad(wukv[:, :, QK_NOPE:], ((0, 0), (0, 0), (0, V_ROWS - V_HEAD))).reshape(
        KV_LORA, MLA_HEADS * V_ROWS).T
    flat12 = lambda a: a[i].reshape(1, 2 * SSD_HEADS)
    poolw_bd = jax.scipy.linalg.block_diag(*[pool_w[i][g] for g in range(len(POOL_WINDOWS))])
    return dict(
        w_in_p=w_in_p, qg=q_g[i][None], kvg=kv_g[i][None],
        wuq_p=wuq_p.astype(BF16), wuk_p=wuk_p.astype(BF16), wuv=wuv.astype(BF16),
        qng_p=_pad_cols(qn_g[i][None], HEAD_PAD), kng_p=_pad_cols(kn_g[i][None], HEAD_PAD),
        conv_w=conv_w[i], conv_b=conv_b[i][None],
        bias_row=_pad_cols(flat12(dt_bias), LANES), bias_col=_pad_cols(flat12(dt_bias), 2 * SUBLANES).T,
        alog_row=_pad_cols(flat12(a_log), LANES), alog_col=_pad_cols(flat12(a_log), 2 * SUBLANES).T,
        dskip_row=jnp.repeat(d_skip[i], SSD_HEAD_DIM)[None],
        ssdg=ssd_g[i][None], poolw_bd=poolw_bd.astype(BF16), pools=pool_scale[i][None],
        wout=w_out[i].astype(BF16), rw_p=_split_bf16(_pad_cols(router_w[i], LANES)),
        rb_p=_pad_cols(router_b[i][None], LANES))


def kernel(x, c, ctx, c_ctx, norm1_g, norm2_g, mod_w, mod_b, w_in, mla_q_norm_g, mla_kv_norm_g, mla_w_uq, mla_w_ukv, mla_qn_g, mla_kn_g, ssd_conv_w, ssd_conv_b, ssd_a_log, ssd_dt_bias, ssd_d, ssd_norm_g, pool_w, pool_scale, w_out, router_w, router_b, moe_w_gate, moe_b_gate, moe_w_up, moe_b_up, moe_w_down, moe_b_down):
    batch, n_lat, d = x.shape
    n_ctx = ctx.shape[1]
    depth = mod_w.shape[0]
    assert batch == 1 and d == D_MODEL and n_ctx == SSD_CHUNK and n_ctx <= ROW_TILE and n_lat % ROW_TILE == 0
    assert n_lat % ATT_TQ == 0 and n_lat % ATT_TK == 0 and n_lat % GRID_W == 0
    n_lat_tiles = n_lat // ROW_TILE

    cvec = jnp.zeros((SUBLANES, d), F32).at[0].set(c[0]).at[1].set(c_ctx)
    mods = _modulation(cvec, mod_w, mod_b)
    rope_c, rope_sa, rope_sb = _rope_tables(n_lat, n_ctx)

    xs = jnp.concatenate([x[0], ctx[0]], axis=0)
    moe = None
    for i in range(depth):
        lp = _pack_layer(i, w_in, mla_q_norm_g, mla_kv_norm_g, mla_w_uq, mla_w_ukv, mla_qn_g, mla_kn_g,
                         ssd_conv_w, ssd_conv_b, ssd_a_log, ssd_dt_bias, ssd_d, ssd_norm_g, pool_w, pool_scale,
                         w_out, router_w, router_b)
        modtab = jnp.pad(mods[i, :2].reshape(2, 6, d), ((0, 0), (0, SUBLANES - 6), (0, 0)))
        outs = _in_projection(
            xs, moe, modtab, norm1_g[i][None], lp['w_in_p'], lp['qg'], lp['kvg'], lp['wuq_p'], lp['wuk_p'], lp['wuv'],
            lp['qng_p'], lp['kng_p'], rope_c, rope_sa, rope_sb, n_lat_tiles)
        if moe is not None:
            xs, outs = outs[0], outs[1:]
        q, k, v, z, xbc, dt, pool_in = outs
        att = _attention(q, k, v, n_lat, n_ctx)
        xc = _conv(xbc, lp['conv_w'], lp['conv_b'], n_lat_tiles)
        dtt = dt[:, :2 * SUBLANES].T
        yf, yb = _ssd(xc, dt, dtt, lp['bias_row'], lp['bias_col'], lp['alog_row'], lp['alog_col'],
                      lp['dskip_row'], n_lat // SSD_CHUNK)
        x1, h2_slots, route, counts = _mixer_output(
            xs, modtab, att, yf, yb, z, pool_in, lp['ssdg'], lp['poolw_bd'], lp['pools'], lp['wout'],
            norm2_g[i][None], lp['rw_p'], lp['rb_p'], n_lat_tiles, n_lat, n_ctx)
        dest, row_src, block_expert, n_used = _dispatch_plan(route, counts)
        n_rows = row_src.shape[0]
        split = n_rows // MOE_TILE // 4 * MOE_TILE
        h2_flat = h2_slots.reshape(-1, d)
        out = None
        for lo, hi in ((0, split), (split, n_rows)):
            x_part = h2_flat.at[row_src[lo:hi]].get(mode='promise_in_bounds')
            out = _experts(i, x_part, lo // MOE_TILE, n_rows, out, block_expert, n_used,
                           moe_w_gate, moe_b_gate, moe_w_up, moe_b_up, moe_w_down, moe_b_down)
        picked = out.at[dest].get(mode='promise_in_bounds', unique_indices=True)
        xs, moe = x1, (picked.reshape(TOP_K, xs.shape[0], d), route, modtab)
    return _final_combine(xs, *moe, n_lat)[None]
```

```python
import functools

import jax
import jax.numpy as jnp
from jax import lax
from jax.experimental import pallas as pl
from jax.experimental.pallas import tpu as pltpu

F32 = jnp.float32
BF16 = jnp.bfloat16
HIGHEST = lax.Precision.HIGHEST

D_MODEL = 1024
GRID_W = 64
EPS = 1e-6
MLA_HEADS = 6
QK_NOPE = 64
QK_ROPE = 32
QK_DIM = QK_NOPE + QK_ROPE
V_HEAD = 64
Q_LORA = 256
KV_LORA = 128
MLA_WIDTH = MLA_HEADS * V_HEAD
ROPE_THETA = 10000.0
SSD_HEADS = 6
SSD_HEAD_DIM = 64
SSD_WIDTH = SSD_HEADS * SSD_HEAD_DIM
SSD_GROUPS = 2
SSD_STATE = 64
SSD_CONV = 4
SSD_CONV_DIM = SSD_WIDTH + 2 * SSD_GROUPS * SSD_STATE
POOL_WINDOWS = (2, 4, 8, 16)
POOL_GROUP = 64
POOL_WIDTH = len(POOL_WINDOWS) * POOL_GROUP
N_EXPERTS = 32
TOP_K = 4
D_FF = 1024
SWIGLU_LIMIT = 7.0
SWIGLU_ALPHA = 1.702

LANES = 128
SUBLANES = 8
HEAD_PAD = LANES
V_ROWS = HEAD_PAD
Q_SCALE = QK_DIM ** -0.5 * 1.4426950408889634

ROW_TILE = 512
SSD_CHUNK = 256
HALO = SUBLANES
ATT_TQ = 1024
ATT_TK = 512
MOE_TILE = 512
VMEM_LIMIT = 56 * 1024 * 1024

COL_QLAT = 0
COL_KVLAT = COL_QLAT + Q_LORA
COL_KROPE = COL_KVLAT + KV_LORA
COL_Z = COL_KROPE + LANES
COL_XBC = COL_Z + SSD_WIDTH
COL_DT = COL_XBC + SSD_CONV_DIM
COL_POOL = COL_DT + LANES
IN_COLS_PACKED = COL_POOL + POOL_WIDTH


def _row_sumsq(x):
    return jnp.dot((x * x).astype(BF16), jnp.ones((x.shape[1], LANES), BF16), preferred_element_type=F32)


def _rms(x):
    width = x.shape[1]
    inv = lax.rsqrt(_row_sumsq(x) * (1.0 / width) + EPS)
    return x * jnp.tile(inv, (1, width // LANES))


def _params(*sem):
    return pltpu.CompilerParams(dimension_semantics=sem, vmem_limit_bytes=VMEM_LIMIT)


def _mod_kernel(c_ref, w_ref, b_ref, o_ref):
    c = c_ref[...]
    s = c / (1.0 + jnp.exp(-c))
    o_ref[0] = jnp.dot(s, w_ref[0], precision=HIGHEST, preferred_element_type=F32) + b_ref[0]


def _modulation(cvec, mod_w, mod_b):
    depth, d, cols = mod_w.shape
    tn = 1536
    return pl.pallas_call(
        _mod_kernel,
        grid=(depth, cols // tn),
        in_specs=[pl.BlockSpec((SUBLANES, d), lambda l, j: (0, 0)),
                  pl.BlockSpec((1, d, tn), lambda l, j: (l, 0, j)),
                  pl.BlockSpec((1, 1, tn), lambda l, j: (l, 0, j))],
        out_specs=pl.BlockSpec((1, SUBLANES, tn), lambda l, j: (l, 0, j)),
        out_shape=jax.ShapeDtypeStruct((depth, SUBLANES, cols), F32),
        compiler_params=_params("arbitrary", "arbitrary"),
        name="modulation",
    )(cvec, mod_w, mod_b.reshape(depth, 1, cols))


def _moe_combine(x1_ref, picked_ref, route_ref, pmod_ref):
    route = route_ref[...]
    y = picked_ref[0].astype(F32) * route[:, TOP_K:TOP_K + 1]
    for kk in range(1, TOP_K):
        y = y + picked_ref[kk].astype(F32) * route[:, TOP_K + kk:TOP_K + kk + 1]
    return x1_ref[...] + pmod_ref[0][5:6] * y


def _combine_kernel(x1_ref, picked_ref, route_ref, pmod_ref, o_ref):
    o_ref[...] = _moe_combine(x1_ref, picked_ref, route_ref, pmod_ref)


def _final_combine(x1, picked, route, pmodtab, n_lat):
    d = x1.shape[1]
    row = lambda cols: pl.BlockSpec((ROW_TILE, cols), lambda i: (i, 0))
    return pl.pallas_call(
        _combine_kernel,
        grid=(n_lat // ROW_TILE,),
        in_specs=[row(d), pl.BlockSpec((TOP_K, ROW_TILE, d), lambda i: (0, i, 0)), row(LANES),
                  pl.BlockSpec((1, SUBLANES, d), lambda i: (0, 0, 0))],
        out_specs=row(d),
        out_shape=jax.ShapeDtypeStruct((n_lat, d), F32),
        compiler_params=_params("arbitrary"),
        name="moe_combine",
    )(x1, picked, route, pmodtab)


def _inproj_kernel(*refs, has_moe):
    if has_moe:
        x1_ref, picked_ref, route_ref, pmod_ref = refs[:4]
        refs = refs[4:]
    else:
        x_ref = refs[0]
        refs = refs[1:]
    (mod_ref, g1_ref, win_ref, qg_ref, kvg_ref, wuq_ref, wuk_ref, wuv_ref, qng_ref, kng_ref,
     rc_ref, rsa_ref, rsb_ref) = refs[:13]
    outs = refs[13:]
    if has_moe:
        xs_ref, outs = outs[0], outs[1:]
        x = _moe_combine(x1_ref, picked_ref, route_ref, pmod_ref)
        xs_ref[...] = x
    else:
        x = x_ref[...]
    q_ref, k_ref, v_ref, z_ref, xbc_ref, dt_ref, pool_ref = outs
    mod = mod_ref[0]
    shift, scale = mod[0:1], mod[1:2]
    h = _rms(x) * g1_ref[...] * (1.0 + scale) + shift
    proj = jnp.dot(h.astype(BF16), win_ref[...], preferred_element_type=F32)
    z_ref[...] = proj[:, COL_Z:COL_XBC]
    xbc_ref[...] = proj[:, COL_XBC:COL_DT]
    dt_ref[...] = proj[:, COL_DT:COL_POOL]
    pool_ref[...] = proj[:, COL_POOL:IN_COLS_PACKED]

    rc, rsa, rsb = rc_ref[...], rsa_ref[...], rsb_ref[...]

    def head_norm_rope(t, gain):
        ms = _row_sumsq(t) * (1.0 / QK_DIM)
        t = t * lax.rsqrt(ms + EPS) * gain
        return (t * rc + pltpu.roll(t, QK_ROPE // 4, axis=1) * rsa
                + pltpu.roll(t, HEAD_PAD - QK_ROPE // 4, axis=1) * rsb)

    qn = (_rms(proj[:, COL_QLAT:COL_KVLAT]) * qg_ref[...]).astype(BF16)
    q_all = jnp.dot(qn, wuq_ref[...], preferred_element_type=F32)
    kvn = (_rms(proj[:, COL_KVLAT:COL_KROPE]) * kvg_ref[...]).astype(BF16)
    k_all = jnp.dot(kvn, wuk_ref[...], preferred_element_type=F32)
    k_rope = proj[:, COL_KROPE:COL_Z]
    vt_all = lax.dot_general(wuv_ref[...], kvn, (((1,), (1,)), ((), ())), preferred_element_type=F32)
    vrow = lax.broadcasted_iota(jnp.int32, vt_all.shape, 0) % V_ROWS
    v_ref[0] = jnp.where(vrow == V_HEAD, 1.0, vt_all).astype(BF16)
    qng, kng = qng_ref[...], kng_ref[...]
    for hd in range(MLA_HEADS):
        sl = slice(hd * HEAD_PAD, (hd + 1) * HEAD_PAD)
        q_ref[:, sl] = (head_norm_rope(q_all[:, sl], qng) * Q_SCALE).astype(BF16)
        k_ref[:, sl] = head_norm_rope(k_all[:, sl] + k_rope, kng).astype(BF16)


def _in_projection(x, moe, modtab, g1, w_in_p, qg, kvg, wuq_p, wuk_p, wuv, qng_p, kng_p, rope_c, rope_sa, rope_sb,
                   n_lat_tiles):
    t_rows, d = x.shape
    nt = pl.cdiv(t_rows, ROW_TILE)
    row = lambda cols: pl.BlockSpec((ROW_TILE, cols), lambda i: (i, 0))
    full = lambda a: pl.BlockSpec(a.shape, lambda i: (0,) * a.ndim)
    seg = lambda: pl.BlockSpec((1, SUBLANES, d), lambda i: (i // n_lat_tiles, 0, 0))
    hw = MLA_HEADS * HEAD_PAD
    per = ATT_TK // ROW_TILE
    vw = MLA_HEADS * V_ROWS
    out_specs = [row(hw), row(hw), pl.BlockSpec((1, vw, ROW_TILE), lambda i: (i // per, 0, i % per)),
                 row(SSD_WIDTH), row(SSD_CONV_DIM), row(LANES), row(POOL_WIDTH)]
    sds = lambda cols, dt: jax.ShapeDtypeStruct((t_rows, cols), dt)
    out_shape = [sds(hw, BF16), sds(hw, BF16), jax.ShapeDtypeStruct((pl.cdiv(nt, per), vw, ATT_TK), BF16),
                 sds(SSD_WIDTH, F32), sds(SSD_CONV_DIM, F32), sds(LANES, F32), sds(POOL_WIDTH, F32)]
    lead_in, lead_specs = [x], [row(d)]
    if moe is not None:
        picked, route, pmodtab = moe
        lead_in += [picked, route, pmodtab]
        lead_specs += [pl.BlockSpec((TOP_K, ROW_TILE, d), lambda i: (0, i, 0)), row(LANES), seg()]
        out_specs = [row(d)] + out_specs
        out_shape = [sds(d, F32)] + out_shape
    return pl.pallas_call(
        functools.partial(_inproj_kernel, has_moe=moe is not None),
        grid=(nt,),
        in_specs=lead_specs + [seg(), full(g1), full(w_in_p), full(qg), full(kvg), full(wuq_p), full(wuk_p),
                               full(wuv), full(qng_p), full(kng_p), row(HEAD_PAD), row(HEAD_PAD), row(HEAD_PAD)],
        out_specs=out_specs,
        out_shape=out_shape,
        compiler_params=_params("arbitrary"),
        name="in_projection",
    )(*lead_in, modtab, g1, w_in_p, qg, kvg, wuq_p, wuk_p, wuv, qng_p, kng_p, rope_c, rope_sa, rope_sb)


def _attention_kernel(q_ref, k_ref, vt_ref, o_ref, sa_ref, sb_ref, st_ref, ca_ref, cb_ref, ct_ref, m_ref, acc_ref,
                      *, n_main, tk, tail):
    m_ref[...] = jnp.full(m_ref.shape, -jnp.inf, F32)
    acc_ref[...] = jnp.zeros(acc_ref.shape, F32)

    def scores(buf, start, size):
        s_ref, cmax_ref = buf
        for hh in range(2):
            q = q_ref[:, hh * HEAD_PAD:(hh + 1) * HEAD_PAD]
            k = k_ref[pl.ds(start, size), hh * HEAD_PAD:(hh + 1) * HEAD_PAD]
            s = lax.dot_general(k, q, (((1,), (1,)), ((), ())), preferred_element_type=F32)
            s_ref[hh] = s
            cmax_ref[hh] = jnp.max(s, axis=0, keepdims=True)

    def consume(buf, c, size):
        s_ref, cmax_ref = buf
        for hh in range(2):
            s = s_ref[hh]
            m_old = m_ref[hh]
            m_new = jnp.maximum(m_old, cmax_ref[hh])
            p = jnp.exp2((s - m_new).astype(BF16))
            vt = vt_ref[c, hh * V_ROWS:(hh + 1) * V_ROWS, 0:size]
            acc_ref[hh] = jnp.exp2(m_old - m_new) * acc_ref[hh] + jnp.dot(vt, p, preferred_element_type=F32)
            m_ref[hh] = m_new

    row0 = lambda c: pl.multiple_of(c * tk, tk)
    t_start, t_size, t_chunk = tail
    sa_ref, sb_ref, st_ref = (sa_ref, ca_ref), (sb_ref, cb_ref), (st_ref, ct_ref)
    scores(st_ref, t_start, t_size)
    if n_main > 0:
        scores(sa_ref, 0, tk)
    consume(st_ref, t_chunk, t_size)
    if n_main > 0:
        def body(j, carry):
            scores(sb_ref, row0(2 * j + 1), tk)
            consume(sa_ref, 2 * j, tk)
            scores(sa_ref, row0(2 * j + 2), tk)
            consume(sb_ref, 2 * j + 1, tk)
            return carry
        lax.fori_loop(0, n_main // 2 - 1, body, 0)
        scores(sb_ref, (n_main - 1) * tk, tk)
        consume(sa_ref, n_main - 2, tk)
        consume(sb_ref, n_main - 1, tk)
    outs = []
    for hh in range(2):
        acc = acc_ref[hh]
        outs.append(acc[0:V_HEAD] / acc[V_HEAD:V_HEAD + 1])
    o_ref[...] = jnp.concatenate(outs, axis=0).T.astype(o_ref.dtype)


def _attention(q, k, vt, n_lat, n_ctx):
    t_rows = q.shape[0]
    pairs = MLA_HEADS // 2
    n_main = n_lat // ATT_TK
    n_chunks = vt.shape[0]
    assert n_main % 2 == 0 and n_main >= 2 and n_chunks == n_main + 1

    def scratch(tq, tk, t_size):
        return ([pltpu.VMEM((2, tk, tq), F32), pltpu.VMEM((2, tk, tq), F32), pltpu.VMEM((2, t_size, tq), F32)]
                + [pltpu.VMEM((2, 1, tq), F32)] * 4 + [pltpu.VMEM((2, V_ROWS, tq), F32)])

    lat = pl.pallas_call(
        functools.partial(_attention_kernel, n_main=n_main, tk=ATT_TK, tail=(n_lat, n_ctx, n_main)),
        grid=(pairs, n_lat // ATT_TQ),
        in_specs=[pl.BlockSpec((ATT_TQ, 2 * HEAD_PAD), lambda p, i: (i, p)),
                  pl.BlockSpec((t_rows, 2 * HEAD_PAD), lambda p, i: (0, p), pipeline_mode=pl.Buffered(1)),
                  pl.BlockSpec((n_chunks, 2 * V_ROWS, ATT_TK), lambda p, i: (0, p, 0),
                               pipeline_mode=pl.Buffered(1))],
        out_specs=pl.BlockSpec((ATT_TQ, 2 * V_HEAD), lambda p, i: (i, p)),
        out_shape=jax.ShapeDtypeStruct((t_rows, MLA_WIDTH), BF16),
        scratch_shapes=scratch(ATT_TQ, ATT_TK, n_ctx),
        compiler_params=_params("arbitrary", "arbitrary"),
        name="attention_latent",
    )(q, k, vt)
    cblk = n_lat // n_ctx
    cspec = lambda: pl.BlockSpec((n_ctx, 2 * HEAD_PAD), lambda p: (cblk, p))

    def context_kernel(q_ref, k_ref, vt_ref, lat_ref, o_ref, *scratch_refs):
        _attention_kernel(q_ref, k_ref, vt_ref, o_ref, *scratch_refs, n_main=0, tk=SUBLANES, tail=(0, n_ctx, 0))

    return pl.pallas_call(
        context_kernel,
        grid=(pairs,),
        in_specs=[cspec(), cspec(), pl.BlockSpec((1, 2 * V_ROWS, ATT_TK), lambda p: (n_main, p, 0)),
                  pl.BlockSpec(memory_space=pl.ANY)],
        out_specs=pl.BlockSpec((n_ctx, 2 * V_HEAD), lambda p: (cblk, p)),
        out_shape=jax.ShapeDtypeStruct((t_rows, MLA_WIDTH), BF16),
        input_output_aliases={3: 0},
        scratch_shapes=scratch(n_ctx, SUBLANES, n_ctx),
        compiler_params=_params("arbitrary"),
        name="attention_context",
    )(q, k, vt, lat)


def _halo_specs(cols, tile_rows, t_rows):
    per = tile_rows // HALO
    last = t_rows // HALO - 1
    return [pl.BlockSpec((HALO, cols), lambda i: (jnp.maximum(i * per - 1, 0), 0)),
            pl.BlockSpec((tile_rows, cols), lambda i: (i, 0)),
            pl.BlockSpec((HALO, cols), lambda i: (jnp.minimum((i + 1) * per, last), 0))]


def _with_halo(prev_ref, cur_ref, next_ref, i, n_lat_tiles, t_rows):
    has_prev = jnp.logical_and(i != 0, i != n_lat_tiles)
    has_next = jnp.logical_and(i != n_lat_tiles - 1, i != pl.num_programs(0) - 1)
    prev = jnp.where(has_prev, prev_ref[...], 0.0)
    nxt = jnp.where(has_next, next_ref[...], 0.0)
    cur = cur_ref[...]
    row = lax.broadcasted_iota(jnp.int32, cur.shape, 0)
    cur = jnp.where(row < t_rows - i * cur.shape[0], cur, 0.0)
    return jnp.concatenate([prev, cur, nxt], axis=0)


def _conv_kernel(prev_ref, cur_ref, next_ref, w_ref, b_ref, o_ref, *, n_lat_tiles, t_rows):
    i = pl.program_id(0)
    ext = _with_halo(prev_ref, cur_ref, next_ref, i, n_lat_tiles, t_rows)
    rows = ext.shape[0]
    w = w_ref[...]
    y = (pltpu.roll(ext, 2, axis=0) * w[0:1] + pltpu.roll(ext, 1, axis=0) * w[1:2]
         + ext * w[2:3] + pltpu.roll(ext, rows - 1, axis=0) * w[3:4])
    y = y[HALO:rows - HALO] + b_ref[...]
    o_ref[...] = y / (1.0 + jnp.exp(-y))


def _conv(xbc, conv_w, conv_b, n_lat_tiles):
    t_rows, cols = xbc.shape
    nt = pl.cdiv(t_rows, ROW_TILE)
    return pl.pallas_call(
        functools.partial(_conv_kernel, n_lat_tiles=n_lat_tiles, t_rows=t_rows),
        grid=(nt,),
        in_specs=_halo_specs(cols, ROW_TILE, t_rows) + [
            pl.BlockSpec(conv_w.shape, lambda i: (0, 0)), pl.BlockSpec(conv_b.shape, lambda i: (0, 0))],
        out_specs=pl.BlockSpec((ROW_TILE, cols), lambda i: (i, 0)),
        out_shape=jax.ShapeDtypeStruct((t_rows, cols), F32),
        compiler_params=_params("arbitrary"),
        name="ssd_conv",
    )(xbc, xbc, xbc, conv_w, conv_b)


def _softplus(v):
    return jnp.maximum(v, 0.0) + jnp.log(1.0 + jnp.exp(-jnp.abs(v)))


def _ssd_kernel(xf_ref, xb_ref, dtf_ref, dtb_ref, dttf_ref, dttb_ref, bias_ref, biast_ref, alog_ref, alogt_ref,
                dskip_ref, expand_ref, yf_ref, yb_ref, state_ref):
    L = xf_ref.shape[0]
    P, N, H = SSD_HEAD_DIM, SSD_STATE, SSD_HEADS

    @pl.when(pl.program_id(0) == 0)
    def _():
        state_ref[...] = jnp.zeros(state_ref.shape, F32)

    r = lax.broadcasted_iota(jnp.int32, (L, L), 0)
    c = lax.broadcasted_iota(jnp.int32, (L, L), 1)
    lower = r >= c
    upper = r <= c
    lower_f = lower.astype(BF16)
    upper_f = upper.astype(BF16)

    def split3(a):
        a1 = a.astype(BF16)
        r1 = a - a1.astype(F32)
        a2 = r1.astype(BF16)
        return a1, a2, (r1 - a2.astype(F32)).astype(BF16)
    a_row = -jnp.exp(alog_ref[...])
    a_col = -jnp.exp(alogt_ref[...])

    def one_direction(x_ref, dt_ref, dtt_ref, y_ref, base, forward):
        x = x_ref[...]
        dt = _softplus(dt_ref[...] + bias_ref[...])
        dtt = _softplus(dtt_ref[...] + biast_ref[...])
        tri_col = lower_f if forward else upper_f
        tri_row = upper_f if forward else lower_f
        cs = sum(jnp.dot(tri_col, part, preferred_element_type=F32) for part in split3(dt * a_row))
        cst = sum(jnp.dot(part, tri_row, preferred_element_type=F32) for part in split3(dtt * a_col))
        mask = lower if forward else upper
        end = L - 1 if forward else 0
        outs = []
        cb = []
        for g in range(SSD_GROUPS):
            bg = x[:, SSD_WIDTH + g * N:SSD_WIDTH + (g + 1) * N].astype(BF16)
            cg = x[:, SSD_WIDTH + (SSD_GROUPS + g) * N:SSD_WIDTH + (SSD_GROUPS + g + 1) * N].astype(BF16)
            cb.append((bg, cg, lax.dot_general(cg, bg, (((1,), (1,)), ((), ())),
                                               preferred_element_type=F32).astype(BF16)))
        spread = expand_ref[0 if forward else 1]

        def replicate(a):
            hi = a.astype(BF16)
            lo = (a - hi.astype(F32)).astype(BF16)
            return (jnp.dot(hi, spread, preferred_element_type=F32)
                    + jnp.dot(lo, spread, preferred_element_type=F32))

        cs_rep, dt_rep = replicate(cs), replicate(dt)
        for hd in range(H):
            j = base + hd
            bg, cg, cbg = cb[hd // (H // SSD_GROUPS)]
            col = cs_rep[:, hd * LANES:(hd + 1) * LANES]
            rowv = cst[j:j + 1, :]
            total = col[end:end + 1, :]
            gap = jnp.tile(col, (1, L // LANES)) - rowv
            decay = jnp.where(mask, jnp.exp(jnp.minimum(gap, 0.0).astype(BF16)), 0.0)
            xh = (x[:, hd * P:(hd + 1) * P] * dt_rep[:, hd * LANES:hd * LANES + P]).astype(BF16)
            y = jnp.dot(cbg * decay, xh, preferred_element_type=F32)
            st = state_ref[j]
            y = y + jnp.dot(cg, st.astype(BF16), preferred_element_type=F32) * jnp.exp(col[:, :P])
            bw = (bg.astype(F32) * jnp.exp(total[:, :N] - col[:, :N])).astype(BF16)
            state_ref[j] = st * jnp.exp(total[:, :P]) + lax.dot_general(
                bw, xh, (((0,), (0,)), ((), ())), preferred_element_type=F32)
            outs.append(y)
        y_all = jnp.concatenate(outs, axis=1)
        if forward:
            y_all = y_all + x[:, :SSD_WIDTH] * dskip_ref[...]
        y_ref[...] = y_all

    one_direction(xf_ref, dtf_ref, dttf_ref, yf_ref, 0, True)
    one_direction(xb_ref, dtb_ref, dttb_ref, yb_ref, H, False)


def _ssd(xc, dt, dtt, bias_row, bias_col, alog_row, alog_col, dskip_row, n_lat_chunks):
    t_rows, cols = xc.shape
    L = SSD_CHUNK
    nc = t_rows // L
    fwd = lambda j: jnp.where(j == 0, n_lat_chunks, j - 1)
    bwd = lambda j: jnp.where(j == 0, n_lat_chunks, n_lat_chunks - j)
    small = lambda a: pl.BlockSpec(a.shape, lambda j: (0,) * a.ndim)
    r = jnp.arange(LANES)[None, :, None] - jnp.arange(2)[:, None, None] * SSD_HEADS
    spread = (r == jnp.arange(SSD_HEADS * LANES)[None, None, :] // LANES).astype(BF16)
    return pl.pallas_call(
        _ssd_kernel,
        grid=(nc,),
        in_specs=[pl.BlockSpec((L, cols), lambda j: (fwd(j), 0)),
                  pl.BlockSpec((L, cols), lambda j: (bwd(j), 0)),
                  pl.BlockSpec((L, LANES), lambda j: (fwd(j), 0)),
                  pl.BlockSpec((L, LANES), lambda j: (bwd(j), 0)),
                  pl.BlockSpec((2 * SUBLANES, L), lambda j: (0, fwd(j))),
                  pl.BlockSpec((2 * SUBLANES, L), lambda j: (0, bwd(j))),
                  small(bias_row), small(bias_col), small(alog_row), small(alog_col), small(dskip_row),
                  small(spread)],
        out_specs=[pl.BlockSpec((L, SSD_WIDTH), lambda j: (fwd(j), 0)),
                   pl.BlockSpec((L, SSD_WIDTH), lambda j: (bwd(j), 0))],
        out_shape=[jax.ShapeDtypeStruct((t_rows, SSD_WIDTH), F32)] * 2,
        scratch_shapes=[pltpu.VMEM((2 * SSD_HEADS, SSD_STATE, SSD_HEAD_DIM), F32)],
        compiler_params=_params("arbitrary"),
        name="ssd_scan",
    )(xc, xc, dt, dt, dtt, dtt, bias_row, bias_col, alog_row, alog_col, dskip_row, spread)


def _mixout_kernel(x_ref, mod_ref, att_ref, yf_ref, yb_ref, z_ref, pprev_ref, pcur_ref, pnext_ref,
                   ssdg_ref, poolw_ref, pools_ref, wout_ref, g2_ref, rw_ref, rb_ref,
                   x1_ref, h2_ref, route_ref, cnt_ref, base_ref, *, n_lat_tiles, n_lat, n_ctx, t_rows):
    i = pl.program_id(0)
    mod = mod_ref[0]
    z = z_ref[...]
    ssd = _rms((yf_ref[...] + yb_ref[...]) * (z / (1.0 + jnp.exp(-z)))) * ssdg_ref[...]
    ext = _with_halo(pprev_ref, pcur_ref, pnext_ref, i, n_lat_tiles, t_rows)
    rows = ext.shape[0]
    tm = rows - 2 * HALO
    w2 = ext + pltpu.roll(ext, 1, axis=0)
    w4 = pltpu.roll(w2, 1, axis=0) + pltpu.roll(w2, rows - 1, axis=0)
    w8 = pltpu.roll(w4, 2, axis=0) + pltpu.roll(w4, rows - 2, axis=0)
    w16 = pltpu.roll(w8, 4, axis=0) + pltpu.roll(w8, rows - 4, axis=0)
    lane = lax.broadcasted_iota(jnp.int32, (tm, POOL_WIDTH), 1)
    grp = lane // POOL_GROUP
    sl = slice(HALO, rows - HALO)
    wsum = jnp.where(grp == 0, w2[sl], jnp.where(grp == 1, w4[sl], jnp.where(grp == 2, w8[sl], w16[sl])))
    is_ctx = i >= n_lat_tiles
    seg_len = jnp.where(is_ctx, n_ctx, n_lat)
    t = lax.broadcasted_iota(jnp.int32, (tm, POOL_WIDTH), 0) + jnp.where(is_ctx, i - n_lat_tiles, i) * tm
    half = jnp.left_shift(1, grp)
    lo = jnp.clip(t - half, 0, seg_len)
    hi = jnp.clip(t + half, 0, seg_len)
    p = wsum / (hi - lo).astype(F32) - pcur_ref[...]
    pool = jnp.dot(p.astype(BF16), poolw_ref[...], preferred_element_type=F32) * pools_ref[...]
    mix = (jnp.dot(att_ref[...], wout_ref[0:MLA_WIDTH], preferred_element_type=F32)
           + jnp.dot(ssd.astype(BF16), wout_ref[MLA_WIDTH:MLA_WIDTH + SSD_WIDTH], preferred_element_type=F32)
           + jnp.dot(pool.astype(BF16), wout_ref[MLA_WIDTH + SSD_WIDTH:], preferred_element_type=F32))
    x1 = x_ref[...] + mod[2:3] * mix
    x1_ref[...] = x1
    h2 = _rms(x1) * g2_ref[...] * (1.0 + mod[4:5]) + mod[3:4]
    h2b = h2.astype(BF16)
    for kk in range(TOP_K):
        h2_ref[kk] = h2b
    h2_lo = (h2 - h2b.astype(F32)).astype(BF16)
    logits = (jnp.dot(h2b, rw_ref[0], preferred_element_type=F32)
              + jnp.dot(h2_lo, rw_ref[0], preferred_element_type=F32)
              + jnp.dot(h2b, rw_ref[1], preferred_element_type=F32)) + rb_ref[...]

    @pl.when(i == 0)
    def _():
        base_ref[...] = jnp.zeros(base_ref.shape, F32)

    elane = lax.broadcasted_iota(jnp.int32, logits.shape, 1)
    row_ok = lax.broadcasted_iota(jnp.int32, logits.shape, 0) < t_rows - i * tm
    lg = jnp.where(elane < N_EXPERTS, logits, -jnp.inf)
    rr = lax.broadcasted_iota(jnp.int32, (tm, tm), 0)
    cc = lax.broadcasted_iota(jnp.int32, (tm, tm), 1)
    earlier = (rr > cc).astype(BF16)
    offset = base_ref[...]
    tops, ids, ranks = [], [], []
    for kk in range(TOP_K):
        top = jnp.max(lg, axis=-1, keepdims=True)
        idx = jnp.min(jnp.where(lg == top, elane, LANES), axis=-1, keepdims=True)
        sel = jnp.logical_and(elane == idx, row_ok)
        lg = jnp.where(sel, -jnp.inf, lg)
        onehot = sel.astype(BF16)
        before = jnp.dot(earlier, onehot, preferred_element_type=F32) + offset
        ranks.append(jnp.sum(jnp.where(sel, before, 0.0), axis=-1, keepdims=True))
        offset = offset + jnp.sum(sel.astype(F32), axis=0, keepdims=True)
        tops.append(top)
        ids.append(idx.astype(F32))
    base_ref[...] = offset
    cnt_ref[...] = jnp.broadcast_to(offset, cnt_ref.shape)
    exps = [jnp.exp(tp - tops[0]) for tp in tops]
    denom = exps[0] + exps[1] + exps[2] + exps[3]
    route = jnp.zeros(logits.shape, F32)
    for kk in range(TOP_K):
        route = jnp.where(elane == kk, ids[kk], route)
        route = jnp.where(elane == TOP_K + kk, exps[kk] / denom, route)
        route = jnp.where(elane == 2 * TOP_K + kk, ranks[kk], route)
    route_ref[...] = route


def _mixer_output(x, modtab, att, yf, yb, z, pool_in, ssdg, poolw_bd, pools, wout, g2, rw_p, rb_p,
                  n_lat_tiles, n_lat, n_ctx):
    t_rows, d = x.shape
    nt = pl.cdiv(t_rows, ROW_TILE)
    row = lambda cols: pl.BlockSpec((ROW_TILE, cols), lambda i: (i, 0))
    full = lambda a: pl.BlockSpec(a.shape, lambda i: (0,) * a.ndim)
    return pl.pallas_call(
        functools.partial(_mixout_kernel, n_lat_tiles=n_lat_tiles, n_lat=n_lat, n_ctx=n_ctx, t_rows=t_rows),
        grid=(nt,),
        in_specs=[row(d), pl.BlockSpec((1, SUBLANES, d), lambda i: (i // n_lat_tiles, 0, 0)),
                  row(MLA_WIDTH), row(SSD_WIDTH), row(SSD_WIDTH), row(SSD_WIDTH)]
                 + _halo_specs(POOL_WIDTH, ROW_TILE, t_rows)
                 + [full(ssdg), full(poolw_bd), full(pools), full(wout), full(g2), full(rw_p), full(rb_p)],
        out_specs=[row(d), pl.BlockSpec((TOP_K, ROW_TILE, d), lambda i: (0, i, 0)), row(LANES),
                   pl.BlockSpec((SUBLANES, LANES), lambda i: (0, 0))],
        out_shape=[jax.ShapeDtypeStruct((t_rows, d), F32), jax.ShapeDtypeStruct((TOP_K, t_rows, d), BF16),
                   jax.ShapeDtypeStruct((t_rows, LANES), F32), jax.ShapeDtypeStruct((SUBLANES, LANES), F32)],
        scratch_shapes=[pltpu.VMEM((1, LANES), F32)],
        compiler_params=_params("arbitrary"),
        name="mixer_output",
    )(x, modtab, att, yf, yb, z, pool_in, pool_in, pool_in, ssdg, poolw_bd, pools, wout, g2, rw_p, rb_p)


def _expert_kernel(be_ref, nu_ref, x_ref, wg_ref, bg_ref, wu_ref, bu_ref, wd_ref, bd_ref, *rest, first_block):
    o_ref, wg_s, wu_s, wd_s = rest[-4:]
    i = pl.program_id(0)
    blk = i + first_block

    @pl.when(jnp.logical_or(i == 0, be_ref[blk] != be_ref[jnp.maximum(blk - 1, 0)]))
    def _():
        wg_s[...] = wg_ref[0, 0].astype(BF16)
        wu_s[...] = wu_ref[0, 0].astype(BF16)
        wd_s[...] = wd_ref[0, 0].astype(BF16)

    @pl.when(blk < nu_ref[0])
    def _():
        x = x_ref[...]
        g = jnp.minimum(jnp.dot(x, wg_s[...], preferred_element_type=F32) + bg_ref[0, 0], SWIGLU_LIMIT)
        u = jnp.clip(jnp.dot(x, wu_s[...], preferred_element_type=F32) + bu_ref[0, 0],
                     -SWIGLU_LIMIT, SWIGLU_LIMIT)
        a = g / (1.0 + jnp.exp(-SWIGLU_ALPHA * g)) * (u + 1.0)
        y = jnp.dot(a.astype(BF16), wd_s[...], preferred_element_type=F32) + bd_ref[0, 0]
        o_ref[...] = y.astype(o_ref.dtype)

    @pl.when(blk >= nu_ref[0])
    def _():
        o_ref[...] = jnp.zeros(o_ref.shape, o_ref.dtype)


def _experts(layer, x_part, first_block, n_rows, out_prev, block_expert, n_used, wg, bg, wu, bu, wd, bd):
    d = x_part.shape[1]
    depth, ne, _, dff = wg.shape
    wspec = lambda a, b: pl.BlockSpec((1, 1, a, b), lambda i, be, nu: (layer, be[i + first_block], 0, 0))
    in_specs = [pl.BlockSpec((MOE_TILE, d), lambda i, be, nu: (i, 0)),
                wspec(d, dff), wspec(1, dff), wspec(d, dff), wspec(1, dff), wspec(dff, d), wspec(1, d)]
    operands = [block_expert, n_used, x_part, wg, bg.reshape(depth, ne, 1, dff), wu, bu.reshape(depth, ne, 1, dff),
                wd, bd.reshape(depth, ne, 1, d)]
    aliases = {}
    if out_prev is not None:
        in_specs.append(pl.BlockSpec(memory_space=pl.ANY))
        aliases = {len(operands): 0}
        operands.append(out_prev)
    grid_spec = pltpu.PrefetchScalarGridSpec(
        num_scalar_prefetch=2,
        grid=(x_part.shape[0] // MOE_TILE,),
        in_specs=in_specs,
        out_specs=pl.BlockSpec((MOE_TILE, d), lambda i, be, nu: (i + first_block, 0)),
        scratch_shapes=[pltpu.VMEM((d, dff), BF16), pltpu.VMEM((d, dff), BF16), pltpu.VMEM((dff, d), BF16)],
    )
    return pl.pallas_call(
        functools.partial(_expert_kernel, first_block=first_block),
        grid_spec=grid_spec,
        out_shape=jax.ShapeDtypeStruct((n_rows, d), BF16),
        input_output_aliases=aliases,
        compiler_params=_params("arbitrary"),
        name="moe_experts",
    )(*operands)


def _dispatch_plan(route, counts_f):
    t_rows = route.shape[0]
    n_assign = t_rows * TOP_K
    ids = route[:, 0:TOP_K].astype(jnp.int32)
    rank =route[:, 2 * TOP_K:3 * TOP_K].astype(jnp.int32)
    counts = counts_f[0, :N_EXPERTS].astype(jnp.int32)
    padded = (counts + MOE_TILE - 1) // MOE_TILE * MOE_TILE
    pad_end = jnp.cumsum(padded)
    pad_start = pad_end - padded
    experts = jnp.arange(N_EXPERTS, dtype=jnp.int32)
    first_row = jnp.sum(jnp.where(ids[:, :, None] == experts[None, None, :], pad_start[None, None, :], 0), axis=-1)
    dest = (first_row + rank).T.reshape(-1)
    n_rows = n_assign + N_EXPERTS * MOE_TILE
    n_blocks = n_rows // MOE_TILE
    block_expert = jnp.minimum(
        jnp.sum((jnp.arange(n_blocks, dtype=jnp.int32)[:, None] * MOE_TILE >= pad_end[None, :]).astype(jnp.int32),
                axis=1), N_EXPERTS - 1).astype(jnp.int32)
    marked = jnp.zeros((n_rows,), jnp.int32).at[dest].add(jnp.arange(n_assign, dtype=jnp.int32) + 1)
    row_src = jnp.where(marked > 0, marked - 1, jnp.arange(n_rows, dtype=jnp.int32) % n_assign)
    n_used = (pad_end[-1:] // MOE_TILE).astype(jnp.int32)
    return dest, row_src, block_expert, n_used


def _rope_tables(n_lat, n_ctx):
    rows = n_lat // GRID_W
    row = jnp.repeat(jnp.arange(rows), GRID_W).astype(F32)
    col = jnp.tile(jnp.arange(GRID_W), rows).astype(F32)
    n_freq = QK_ROPE // 4
    inv_freq = ROPE_THETA ** (-jnp.arange(n_freq, dtype=F32) / n_freq)
    ang = jnp.stack([row[:, None] * inv_freq, col[:, None] * inv_freq], axis=1)
    cos, sin = jnp.cos(ang), jnp.sin(ang)
    zero = jnp.zeros_like(sin)
    ones = lambda w: jnp.ones((n_lat, w), F32)
    zeros = lambda w: jnp.zeros((n_lat, w), F32)
    per_axis = lambda a, b: jnp.stack([a, b], axis=2).reshape(n_lat, QK_ROPE)
    tail = HEAD_PAD - QK_DIM
    c = jnp.concatenate([ones(QK_NOPE), per_axis(cos, cos), ones(tail)], axis=1)
    sa = jnp.concatenate([zeros(QK_NOPE), per_axis(zero, sin), zeros(tail)], axis=1)
    sb = jnp.concatenate([zeros(QK_NOPE), per_axis(-sin, zero), zeros(tail)], axis=1)
    pad = lambda a, v: jnp.concatenate([a, jnp.full((n_ctx, HEAD_PAD), v, F32)], axis=0)
    return pad(c, 1.0), pad(sa, 0.0), pad(sb, 0.0)


def _pad_cols(a, width):
    return jnp.pad(a, ((0, 0), (0, width - a.shape[1])))


def _split_bf16(a):
    hi = a.astype(BF16)
    return jnp.stack([hi, (a - hi.astype(F32)).astype(BF16)])


def _pack_layer(i, w_in, q_g, kv_g, w_uq, w_ukv, qn_g, kn_g, conv_w, conv_b, a_log, dt_bias, d_skip, ssd_g,
                pool_w, pool_scale, w_out, router_w, router_b):
    d = w_in.shape[1]
    pts = [0, Q_LORA, KV_LORA, QK_ROPE, SSD_WIDTH, SSD_CONV_DIM, 2 * SSD_HEADS, POOL_WIDTH]
    offs = [sum(pts[:j + 1]) for j in range(len(pts))]
    seg = lambda j: w_in[i][:, offs[j]:offs[j + 1]]
    zc = lambda w: jnp.zeros((d, w), F32)
    w_in_p = jnp.concatenate([
        seg(0), seg(1), zc(QK_NOPE), seg(2), zc(LANES - QK_DIM), seg(3), seg(4),
        seg(5), zc(LANES - 2 * SSD_HEADS), seg(6)], axis=1).astype(BF16)
    wuq = w_uq[i].reshape(Q_LORA, MLA_HEADS, QK_DIM)
    wuq_p = jnp.pad(wuq, ((0, 0), (0, 0), (0, HEAD_PAD - QK_DIM))).reshape(Q_LORA, MLA_HEADS * HEAD_PAD)
    wukv = w_ukv[i].reshape(KV_LORA, MLA_HEADS, QK_NOPE + V_HEAD)
    wuk_p = jnp.pad(wukv[:, :, :QK_NOPE], ((0, 0), (0, 0), (0, HEAD_PAD - QK_NOPE))).reshape(
        KV_LORA, MLA_HEADS * HEAD_PAD)
    wuv = jnp.pad(wukv[:, :, QK_NOPE:], ((0, 0), (0, 0), (0, V_ROWS - V_HEAD))).reshape(
        KV_LORA, MLA_HEADS * V_ROWS).T
    flat12 = lambda a: a[i].reshape(1, 2 * SSD_HEADS)
    poolw_bd = jax.scipy.linalg.block_diag(*[pool_w[i][g] for g in range(len(POOL_WINDOWS))])
    return dict(
        w_in_p=w_in_p, qg=q_g[i][None], kvg=kv_g[i][None],
        wuq_p=wuq_p.astype(BF16), wuk_p=wuk_p.astype(BF16), wuv=wuv.astype(BF16),
        qng_p=_pad_cols(qn_g[i][None], HEAD_PAD), kng_p=_pad_cols(kn_g[i][None], HEAD_PAD),
        conv_w=conv_w[i], conv_b=conv_b[i][None],
        bias_row=_pad_cols(flat12(dt_bias), LANES), bias_col=_pad_cols(flat12(dt_bias), 2 * SUBLANES).T,
        alog_row=_pad_cols(flat12(a_log), LANES), alog_col=_pad_cols(flat12(a_log), 2 * SUBLANES).T,
        dskip_row=jnp.repeat(d_skip[i], SSD_HEAD_DIM)[None],
        ssdg=ssd_g[i][None], poolw_bd=poolw_bd.astype(BF16), pools=pool_scale[i][None],
        wout=w_out[i].astype(BF16), rw_p=_split_bf16(_pad_cols(router_w[i], LANES)),
        rb_p=_pad_cols(router_b[i][None], LANES))


def kernel(x, c, ctx, c_ctx, norm1_g, norm2_g, mod_w, mod_b, w_in, mla_q_norm_g, mla_kv_norm_g, mla_w_uq, mla_w_ukv, mla_qn_g, mla_kn_g, ssd_conv_w, ssd_conv_b, ssd_a_log, ssd_dt_bias, ssd_d, ssd_norm_g, pool_w, pool_scale, w_out, router_w, router_b, moe_w_gate, moe_b_gate, moe_w_up, moe_b_up, moe_w_down, moe_b_down):
    batch, n_lat, d = x.shape
    n_ctx = ctx.shape[1]
    depth = mod_w.shape[0]
    assert batch == 1 and d == D_MODEL and n_ctx == SSD_CHUNK and n_ctx <= ROW_TILE and n_lat % ROW_TILE == 0
    assert n_lat % ATT_TQ == 0 and n_lat % ATT_TK == 0 and n_lat % GRID_W == 0
    n_lat_tiles = n_lat // ROW_TILE

    cvec = jnp.zeros((SUBLANES, d), F32).at[0].set(c[0]).at[1].set(c_ctx)
    mods = _modulation(cvec, mod_w, mod_b)
    rope_c, rope_sa, rope_sb = _rope_tables(n_lat, n_ctx)

    xs = jnp.concatenate([x[0], ctx[0]], axis=0)
    moe = None
    for i in range(depth):
        lp = _pack_layer(i, w_in, mla_q_norm_g, mla_kv_norm_g, mla_w_uq, mla_w_ukv, mla_qn_g, mla_kn_g,
                         ssd_conv_w, ssd_conv_b, ssd_a_log, ssd_dt_bias, ssd_d, ssd_norm_g, pool_w, pool_scale,
                         w_out, router_w, router_b)
        modtab = jnp.pad(mods[i, :2].reshape(2, 6, d), ((0, 0), (0, SUBLANES - 6), (0, 0)))
        outs = _in_projection(
            xs, moe, modtab, norm1_g[i][None], lp['w_in_p'], lp['qg'], lp['kvg'], lp['wuq_p'], lp['wuk_p'], lp['wuv'],
            lp['qng_p'], lp['kng_p'], rope_c, rope_sa, rope_sb, n_lat_tiles)
        if moe is not None:
            xs, outs = outs[0], outs[1:]
        q, k, v, z, xbc, dt, pool_in = outs
        att = _attention(q, k, v, n_lat, n_ctx)
        xc = _conv(xbc, lp['conv_w'], lp['conv_b'], n_lat_tiles)
        dtt = dt[:, :2 * SUBLANES].T
        yf, yb = _ssd(xc, dt, dtt, lp['bias_row'], lp['bias_col'], lp['alog_row'], lp['alog_col'],
                      lp['dskip_row'], n_lat // SSD_CHUNK)
        x1, h2_slots, route, counts = _mixer_output(
            xs, modtab, att, yf, yb, z, pool_in, lp['ssdg'], lp['poolw_bd'], lp['pools'], lp['wout'],
            norm2_g[i][None], lp['rw_p'], lp['rb_p'], n_lat_tiles, n_lat, n_ctx)
        dest, row_src, block_expert, n_used = _dispatch_plan(route, counts)
        n_rows = row_src.shape[0]
        split = n_rows // MOE_TILE // 4 * MOE_TILE
        h2_flat = h2_slots.reshape(-1, d)
        out = None
        for lo, hi in ((0, split), (split, n_rows)):
            x_part = h2_flat.at[row_src[lo:hi]].get(mode='promise_in_bounds')
            out = _experts(i, x_part, lo // MOE_TILE, n_rows, out, block_expert, n_used,
                           moe_w_gate, moe_b_gate, moe_w_up, moe_b_up, moe_w_down, moe_b_down)
        picked = out.at[dest].get(mode='promise_in_bounds', unique_indices=True)
        xs, moe = x1, (picked.reshape(TOP_K, xs.shape[0], d), route, modtab)
    return _final_combine(xs, *moe, n_lat)[None]
```
